```python
import jax, jax.numpy as jnp
from jax import lax
import numpy as np

D_MODEL = 1024
BATCH = 8
SEQ = 4096
DEPTH = 1
DEC_BATCH = 16
DEC_SEQ = 16
PAST_LEN = 2048

CHUNK = 64
N_MEM = 256
FOX_HEAD_DIM = 64
FOX_HEADS = D_MODEL // 128
FOX_WIDTH = FOX_HEADS * FOX_HEAD_DIM
POOL_WINDOWS = (2, 4, 8, 16)
POOL_GROUPS = 4
POOL_WIDTH = D_MODEL // 2
POOL_GROUP_DIM = POOL_WIDTH // POOL_GROUPS
POOL_HIST = 15
MEM_HEADS = 4
MEM_WIDTH = D_MODEL // 2
MEM_HEAD_DIM = MEM_WIDTH // MEM_HEADS
N_BRANCH = 3
D_FF = 4 * D_MODEL
Q_BLOCK = 128
EPS = 1e-6
FORGET_BIAS = 2.0
SPLIT_SIZES = (FOX_WIDTH, FOX_WIDTH, FOX_WIDTH, FOX_HEADS, POOL_WIDTH, MEM_WIDTH, N_BRANCH * D_MODEL)
IN_DIM = 3 * FOX_WIDTH + FOX_HEADS + POOL_WIDTH + MEM_WIDTH + N_BRANCH * D_MODEL

kernel_name = 'fox_pool_memory_hybrid_stream_step'


def _rmsnorm(x, g):
    xf = x.astype(jnp.float32)
    y = xf * lax.rsqrt(jnp.mean(xf * xf, axis=-1, keepdims=True) + EPS)
    return (y * g.astype(jnp.float32)).astype(x.dtype)


def _split_points():
    return np.cumsum(SPLIT_SIZES)[:-1].tolist()


def _in_proj(x, g, w_in, b_f):
    B, T, _ = x.shape
    z = _rmsnorm(x, g) @ w_in
    q, k, v, f, u, qm, gates = jnp.split(z, _split_points(), axis=-1)
    logf = jax.nn.log_sigmoid((f + b_f).astype(jnp.float32))
    q = q.reshape(B, T, FOX_HEADS, FOX_HEAD_DIM)
    k = k.reshape(B, T, FOX_HEADS, FOX_HEAD_DIM)
    v = v.reshape(B, T, FOX_HEADS, FOX_HEAD_DIM)
    qm = qm.reshape(B, T, MEM_HEADS, MEM_HEAD_DIM)
    return q, k, v, logf, u, qm, gates


def _fox_attend(q, k, v, cq, ck, qpos, kpos):
    scale = FOX_HEAD_DIM ** -0.5
    s = jnp.einsum('bqhd,bkhd->bhqk', q.astype(jnp.float32), k.astype(jnp.float32)) * scale
    s = s + jnp.transpose(cq, (0, 2, 1))[:, :, :, None] - jnp.transpose(ck, (0, 2, 1))[:, :, None, :]
    s = jnp.where(kpos[None, :] <= qpos[:, None], s, -jnp.inf)
    p = jax.nn.softmax(s, axis=-1)
    o = jnp.einsum('bhqk,bkhd->bqhd', p, v.astype(jnp.float32))
    return o.astype(q.dtype)


def _fox_prompt(q, k, v, logf):
    B, S, H, Dh = q.shape
    c = jnp.cumsum(logf, axis=1)
    kpos = jnp.arange(S)

    def block(i):
        start = i * Q_BLOCK
        qb = lax.dynamic_slice_in_dim(q, start, Q_BLOCK, axis=1)
        cqb = lax.dynamic_slice_in_dim(c, start, Q_BLOCK, axis=1)
        qpos = start + jnp.arange(Q_BLOCK)
        return _fox_attend(qb, k, v, cqb, c, qpos, kpos)

    out = lax.map(block, jnp.arange(S // Q_BLOCK))
    return jnp.transpose(out, (1, 0, 2, 3, 4)).reshape(B, S, H, Dh)


def _pool_mix(u_ext, pos0, w_pool, s_pool):
    B, L, _ = u_ext.shape
    T = L - POOL_HIST
    uf = u_ext.astype(jnp.float32).reshape(B, L, POOL_GROUPS, POOL_GROUP_DIM)
    cs = jnp.concatenate([jnp.zeros((B, 1, POOL_GROUPS, POOL_GROUP_DIM), jnp.float32),
                          jnp.cumsum(uf, axis=1)], axis=1)
    end = cs[:, POOL_HIST + 1:]
    pos = pos0 + jnp.arange(T)
    means = []
    for g, w in enumerate(POOL_WINDOWS):
        start = cs[:, POOL_HIST + 1 - w: POOL_HIST + 1 - w + T, g]
        cnt = jnp.minimum(w, pos + 1).astype(jnp.float32)
        means.append((end[:, :, g] - start) / cnt[None, :, None])
    pooled = jnp.stack(means, axis=2) - uf[:, POOL_HIST:]
    mixed = jnp.einsum('btgc,gcd->btgd', pooled, w_pool.astype(jnp.float32))
    mixed = mixed * s_pool.astype(jnp.float32).reshape(POOL_GROUPS, POOL_GROUP_DIM)
    return mixed.reshape(B, T, POOL_WIDTH).astype(u_ext.dtype)


def _mem_kv(mem, g, w_mkv):
    B, M, _ = mem.shape
    kv = _rmsnorm(mem, g) @ w_mkv
    mk, mv = jnp.split(kv, 2, axis=-1)
    return (mk.reshape(B, M, MEM_HEADS, MEM_HEAD_DIM), mv.reshape(B, M, MEM_HEADS, MEM_HEAD_DIM))


def _mem_attend(qm, mk, mv):
    scale = MEM_HEAD_DIM ** -0.5
    s = jnp.einsum('bqhd,bmhd->bhqm', qm.astype(jnp.float32), mk.astype(jnp.float32)) * scale
    p = jax.nn.softmax(s, axis=-1)
    return jnp.einsum('bhqm,bmhd->bqhd', p, mv.astype(jnp.float32)).astype(qm.dtype)


def _merge(x, o_fox, o_pool, o_mem, gates, w_br_fox, w_br_pool, w_br_mem, w_out):
    B, T, _ = x.shape
    g = jax.nn.sigmoid(gates.astype(jnp.float32)).astype(x.dtype).reshape(B, T, N_BRANCH, D_MODEL)
    m = (g[:, :, 0] * (o_fox.reshape(B, T, FOX_WIDTH) @ w_br_fox)
         + g[:, :, 1] * (o_pool @ w_br_pool)
         + g[:, :, 2] * (o_mem.reshape(B, T, MEM_WIDTH) @ w_br_mem))
    return x + m @ w_out


def _ffn(h, g, w_up, w_down):
    a = jax.nn.relu(_rmsnorm(h, g) @ w_up)
    return h + (a * a) @ w_down


def setup_inputs(seed: int = 0) -> dict:
    key = jax.random.key(seed)
    ks = jax.random.split(key, 24)

    def nrm(k, shape, scale=1.0):
        return scale * jax.random.normal(k, shape, jnp.float32)

    D = D_MODEL
    return {
        'x_prompt': nrm(ks[0], (BATCH, SEQ, D)),
        'x_sample': nrm(ks[1], (DEC_BATCH, DEC_SEQ, D)),
        'mem_prompt': nrm(ks[2], (BATCH, N_MEM, D)),
        'cache_fox_k': nrm(ks[3], (DEPTH, DEC_BATCH, PAST_LEN, FOX_HEADS, FOX_HEAD_DIM)),
        'cache_fox_v': nrm(ks[4], (DEPTH, DEC_BATCH, PAST_LEN, FOX_HEADS, FOX_HEAD_DIM)),
        'cache_fox_logf': jax.nn.log_sigmoid(FORGET_BIAS + nrm(ks[5], (DEPTH, DEC_BATCH, PAST_LEN, FOX_HEADS), 0.5)),
        'cache_pool': nrm(ks[6], (DEPTH, DEC_BATCH, POOL_HIST, POOL_WIDTH)),
        'cache_mem_k': nrm(ks[7], (DEPTH, DEC_BATCH, N_MEM, MEM_HEADS, MEM_HEAD_DIM)),
        'cache_mem_v': nrm(ks[8], (DEPTH, DEC_BATCH, N_MEM, MEM_HEADS, MEM_HEAD_DIM)),
        'g_attn': 1.0 + nrm(ks[9], (DEPTH, D), 0.1),
        'w_in': nrm(ks[10], (DEPTH, D, IN_DIM), D ** -0.5),
        'b_f': FORGET_BIAS + nrm(ks[11], (DEPTH, FOX_HEADS), 0.1),
        'w_pool': nrm(ks[12], (DEPTH, POOL_GROUPS, POOL_GROUP_DIM, POOL_GROUP_DIM), POOL_GROUP_DIM ** -0.5),
        's_pool': 1.0 + nrm(ks[13], (DEPTH, POOL_WIDTH), 0.1),
        'g_mem': 1.0 + nrm(ks[14], (DEPTH, D), 0.1),
        'w_mkv': nrm(ks[15], (DEPTH, D, 2 * MEM_WIDTH), D ** -0.5),
        'w_br_fox': nrm(ks[16], (DEPTH, FOX_WIDTH, D), FOX_WIDTH ** -0.5),
        'w_br_pool': nrm(ks[17], (DEPTH, POOL_WIDTH, D), POOL_WIDTH ** -0.5),
        'w_br_mem': nrm(ks[18], (DEPTH, MEM_WIDTH, D), MEM_WIDTH ** -0.5),
        'w_out': nrm(ks[19], (DEPTH, D, D), D ** -0.5),
        'g_mlp': 1.0 + nrm(ks[20], (DEPTH, D), 0.1),
        'w_up': nrm(ks[21], (DEPTH, D, D_FF), D ** -0.5),
        'w_down': nrm(ks[22], (DEPTH, D_FF, D), D_FF ** -0.5),
        'g_final': 1.0 + nrm(ks[23], (D,), 0.1),
    }


def reference(x_prompt, x_sample, mem_prompt, cache_fox_k, cache_fox_v, cache_fox_logf, cache_pool,
              cache_mem_k, cache_mem_v, g_attn, w_in, b_f, w_pool, s_pool, g_mem, w_mkv,
              w_br_fox, w_br_pool, w_br_mem, w_out, g_mlp, w_up, w_down, g_final):
    xp, xs = x_prompt, x_sample
    B = xp.shape[0]
    T = xs.shape[1]
    P = cache_fox_k.shape[2]
    fk_p, fv_p, fl_p, pl_p, mk_p, mv_p = [], [], [], [], [], []
    fk_s, fv_s, fl_s, pl_s = [], [], [], []
    for l in range(DEPTH):
        q, k, v, logf, u, qm, gates = _in_proj(xp, g_attn[l], w_in[l], b_f[l])
        o_fox = _fox_prompt(q, k, v, logf)
        u_ext = jnp.concatenate([jnp.zeros((B, POOL_HIST, POOL_WIDTH), u.dtype), u], axis=1)
        o_pool = _pool_mix(u_ext, 0, w_pool[l], s_pool[l])
        mk, mv = _mem_kv(mem_prompt, g_mem[l], w_mkv[l])
        o_mem = _mem_attend(qm, mk, mv)
        h = _merge(xp, o_fox, o_pool, o_mem, gates, w_br_fox[l], w_br_pool[l], w_br_mem[l], w_out[l])
        xp = _ffn(h, g_mlp[l], w_up[l], w_down[l])
        fk_p.append(k); fv_p.append(v); fl_p.append(logf)
        pl_p.append(u_ext[:, -POOL_HIST:]); mk_p.append(mk); mv_p.append(mv)

        q, k, v, logf, u, qm, gates = _in_proj(xs, g_attn[l], w_in[l], b_f[l])
        k_all = jnp.concatenate([cache_fox_k[l].astype(k.dtype), k], axis=1)
        v_all = jnp.concatenate([cache_fox_v[l].astype(v.dtype), v], axis=1)
        logf_all = jnp.concatenate([cache_fox_logf[l].astype(jnp.float32), logf], axis=1)
        c_all = jnp.cumsum(logf_all, axis=1)
        o_fox = _fox_attend(q, k_all, v_all, c_all[:, P:], c_all, P + jnp.arange(T), jnp.arange(P + T))
        u_ext = jnp.concatenate([cache_pool[l].astype(u.dtype), u], axis=1)
        o_pool = _pool_mix(u_ext, P, w_pool[l], s_pool[l])
        o_mem = _mem_attend(qm, cache_mem_k[l], cache_mem_v[l])
        h = _merge(xs, o_fox, o_pool, o_mem, gates, w_br_fox[l], w_br_pool[l], w_br_mem[l], w_out[l])
        xs = _ffn(h, g_mlp[l], w_up[l], w_down[l])
        fk_s.append(k); fv_s.append(v); fl_s.append(logf); pl_s.append(u_ext[:, -POOL_HIST:])

    y_prompt = _rmsnorm(xp, g_final)
    y_sample = _rmsnorm(xs, g_final)
    new_fox_k_p = jnp.stack(fk_p)
    new_fox_v_p = jnp.stack(fv_p)
    new_fox_logf_p = jnp.stack(fl_p)
    new_pool_p = jnp.stack(pl_p)
    new_mem_k_p = jnp.stack(mk_p)
    new_mem_v_p = jnp.stack(mv_p)
    new_fox_k_s = jnp.stack(fk_s)
    new_fox_v_s = jnp.stack(fv_s)
    new_fox_logf_s = jnp.stack(fl_s)
    new_pool_s = jnp.stack(pl_s)
    return (y_prompt, y_sample, new_fox_k_p, new_fox_v_p, new_fox_logf_p, new_pool_p, new_mem_k_p, new_mem_v_p,
            new_fox_k_s, new_fox_v_s, new_fox_logf_s, new_pool_s)
```

```python
import functools
import math

import jax
import jax.numpy as jnp
from jax import lax
from jax.experimental import pallas as pl
from jax.experimental.pallas import tpu as pltpu

F32 = jnp.float32
BF16 = jnp.bfloat16

EPS = 1e-6
LANES = 128
FOX_HEADS = 8
FOX_HEAD_DIM = 64
FOX_WIDTH = FOX_HEADS * FOX_HEAD_DIM
POOL_WINDOWS = (2, 4, 8, 16)
POOL_GROUPS = 4
POOL_GROUP_DIM = 128
POOL_WIDTH = POOL_GROUPS * POOL_GROUP_DIM
POOL_HIST = 15
HIST_ROWS = 16
MEM_HEADS = 4
MEM_HEAD_DIM = 128
MEM_WIDTH = MEM_HEADS * MEM_HEAD_DIM
N_BRANCH = 3
LOG2E = math.log2(math.e)
BIAS_PIECES = 3
BIAS_LANES = BIAS_PIECES * FOX_HEADS
VMEM_LIMIT = 56 * 1024 * 1024


def _dot(a, b):
    return jnp.dot(a, b, preferred_element_type=F32)


def _dot_nt(a, b):
    return lax.dot_general(a, b, (((1,), (1,)), ((), ())), preferred_element_type=F32)


def _rms(x, g):
    return x * lax.rsqrt(jnp.mean(x * x, axis=-1, keepdims=True) + EPS) * g


def _iota(shape, dim):
    return lax.broadcasted_iota(jnp.int32, shape, dim)


def _split3(x):
    p0 = x.astype(BF16)
    r1 = x - p0.astype(F32)
    p1 = r1.astype(BF16)
    p2 = (r1 - p1.astype(F32)).astype(BF16)
    return p0, p1, p2


def _cumsum_rows(x, rows_out):
    n = x.shape[0]
    tril = (_iota((rows_out, n), 1) <= _iota((rows_out, n), 0)).astype(BF16)
    p0, p1, p2 = _split3(x)
    return _dot(tril, p0) + _dot(tril, p1) + _dot(tril, p2)


def _bias_pieces(c):
    b = c * (-LOG2E)
    p0 = b.astype(BF16).astype(F32)
    r1 = b - p0
    p1 = r1.astype(BF16).astype(F32)
    p2 = (r1 - p1).astype(BF16).astype(F32)
    grp = _iota(c.shape, 1) >> 3
    return jnp.where(grp == 0, p0, jnp.where(grp == 1, p1, jnp.where(grp == 2, p2, 0.0)))


def _mem_kv_kernel(mem_ref, g_ref, w_ref, mk_ref, mv_ref, mkb_ref, mvb_ref):
    xn = _rms(mem_ref[0], g_ref[...]).astype(BF16)
    mk = _dot(xn, w_ref[:, :MEM_WIDTH])
    mv = _dot(xn, w_ref[:, MEM_WIDTH:])
    mk_ref[0] = mk
    mv_ref[0] = mv
    mkb_ref[0] = mk.astype(BF16)
    mvb_ref[0] = mv.astype(BF16)


def _mem_kv(mem, g, w_mkv):
    B, M, D = mem.shape
    blk = lambda w: pl.BlockSpec((1, M, w), lambda b: (b, 0, 0))
    return pl.pallas_call(
        _mem_kv_kernel,
        grid=(B,),
        in_specs=[blk(D), pl.BlockSpec((1, D), lambda b: (0, 0)),
                  pl.BlockSpec((D, 2 * MEM_WIDTH), lambda b: (0, 0))],
        out_specs=[blk(MEM_WIDTH)] * 4,
        out_shape=[jax.ShapeDtypeStruct((B, M, MEM_WIDTH), F32)] * 2
        + [jax.ShapeDtypeStruct((B, M, MEM_WIDTH), BF16)] * 2,
        compiler_params=pltpu.CompilerParams(dimension_semantics=("parallel",),
                                             vmem_limit_bytes=VMEM_LIMIT),
        name="mem_kv",
    )(mem, g, w_mkv)


_OFF_Q, _OFF_K, _OFF_V, _OFF_U, _OFF_QM, _OFF_F = 0, 512, 1024, 1536, 2048, 2560
_IN_COLS = _OFF_F + LANES


def _in_proj_kernel(x_ref, g_ref, w_ref, bf_ref, hist_ref, mk_ref, mv_ref, wpool_ref, spool_ref,
                    k_ref, v_ref, logf_ref, pnew_ref, opool_ref, omem_ref, *rest, tm, pos0, augment):
    extra, (uext_ref, carry_ref) = rest[:-2], rest[-2:]
    t = pl.program_id(1)
    last = pl.num_programs(1) - 1

    @pl.when(t == 0)
    def _():
        uext_ref[0:HIST_ROWS, :] = hist_ref[0]
        carry_ref[...] = jnp.zeros_like(carry_ref)

    xn = _rms(x_ref[0], g_ref[...]).astype(BF16)
    lane = _iota((tm, LANES), 1)

    q = _dot(xn, w_ref[:, _OFF_Q:_OFF_K]) * (FOX_HEAD_DIM ** -0.5 * LOG2E)
    k = _dot(xn, w_ref[:, _OFF_K:_OFF_V])
    v = _dot(xn, w_ref[:, _OFF_V:_OFF_U])
    k_ref[0] = k
    v_ref[0] = v
    zf = _dot(xn, w_ref[:, _OFF_F:_IN_COLS]) + bf_ref[...]
    logf = jnp.minimum(zf, 0.0) - jnp.log1p(jnp.exp(-jnp.abs(zf)))
    logf_ref[0] = logf[:, :FOX_HEADS]

    if augment:
        c = _cumsum_rows(logf, tm) + carry_ref[...]
        carry_ref[...] = c[tm - 1:tm, :]
        bias = pltpu.roll(_bias_pieces(c), FOX_HEAD_DIM, axis=1)
        in_bias = (lane >= FOX_HEAD_DIM) & (lane < FOX_HEAD_DIM + BIAS_LANES)
        q_tail = jnp.where(in_bias, 1.0, 0.0)
        v_tail = jnp.where(lane == FOX_HEAD_DIM, 1.0, 0.0)
        for pair in range(FOX_HEADS // 2):
            cols = slice(pair * LANES, (pair + 1) * LANES)
            for (src, kind), dst in zip(((q, 0), (k, 1), (v, 2)), extra):
                even = src[:, cols]
                odd = pltpu.roll(even, FOX_HEAD_DIM, axis=1)
                for par, data in ((0, even), (1, odd)):
                    h = 2 * pair + par
                    if kind == 0:
                        tail = q_tail
                    elif kind == 1:
                        tail = jnp.where(in_bias & ((lane & 7) == h), bias, 0.0)
                    else:
                        tail = v_tail
                    blk = jnp.where(lane < FOX_HEAD_DIM, data, tail)
                    dst[0, :, h * LANES:(h + 1) * LANES] = blk.astype(BF16)
    else:
        q_ref, lfrep_ref = extra
        q_ref[0] = q
        lfrep_ref[0] = logf

    u = _dot(xn, w_ref[:, _OFF_U:_OFF_QM])
    uext_ref[HIST_ROWS:HIST_ROWS + tm, :] = u
    pos = pos0 + t * tm + _iota((tm, 1), 0)
    for gi, w in enumerate(POOL_WINDOWS):
        cols = slice(gi * POOL_GROUP_DIM, (gi + 1) * POOL_GROUP_DIM)
        ug = u[:, cols]
        sw = ug
        for back in range(1, w):
            sw = sw + uext_ref[HIST_ROWS - back:HIST_ROWS - back + tm, cols]
        cnt = jnp.minimum(w, pos + 1).astype(F32)
        pooled = sw / cnt - ug
        mixed = _dot(pooled.astype(BF16), wpool_ref[gi]) * spool_ref[:, cols]
        opool_ref[0, :, cols] = mixed.astype(BF16)

    @pl.when(t == last)
    def _():
        pnew_ref[0] = uext_ref[tm + 1:tm + HIST_ROWS, :]

    uext_ref[0:HIST_ROWS, :] = uext_ref[tm:tm + HIST_ROWS, :]

    for h in range(MEM_HEADS):
        cols = slice(h * MEM_HEAD_DIM, (h + 1) * MEM_HEAD_DIM)
        qm = _dot(xn, w_ref[:, _OFF_QM + h * MEM_HEAD_DIM:_OFF_QM + (h + 1) * MEM_HEAD_DIM])
        s = _dot_nt(qm.astype(BF16), mk_ref[0, :, cols]) * MEM_HEAD_DIM ** -0.5
        e = jnp.exp(s - jnp.max(s, axis=-1, keepdims=True))
        o = _dot(e.astype(BF16), mv_ref[0, :, cols]) / jnp.sum(e, axis=-1, keepdims=True)
        omem_ref[0, :, cols] = o.astype(BF16)


def _in_proj(x, g, w_rep, bf_rep, hist, mkb, mvb, wpool, spool, *, tm, pos0, augment):
    B, S, D = x.shape
    M = mkb.shape[1]
    nt = S // tm
    tile = lambda w: pl.BlockSpec((1, tm, w), lambda b, t: (b, t, 0))
    per_b = lambda r, w: pl.BlockSpec((1, r, w), lambda b, t: (b, 0, 0))
    const = lambda shape: pl.BlockSpec(shape, lambda b, t: (0,) * len(shape))
    if augment:
        abc_specs = [tile(FOX_HEADS * LANES)] * 3
        abc_shapes = [jax.ShapeDtypeStruct((B, S, FOX_HEADS * LANES), BF16)] * 3
    else:
        abc_specs = [tile(FOX_WIDTH), tile(LANES)]
        abc_shapes = [jax.ShapeDtypeStruct((B, S, FOX_WIDTH), F32),
                      jax.ShapeDtypeStruct((B, S, LANES), F32)]
    return pl.pallas_call(
        functools.partial(_in_proj_kernel, tm=tm, pos0=pos0, augment=augment),
        grid=(B, nt),
        in_specs=[tile(D), const((1, D)), const((D, _IN_COLS)), const((1, LANES)),
                  per_b(HIST_ROWS, POOL_WIDTH), per_b(M, MEM_WIDTH), per_b(M, MEM_WIDTH),
                  const((POOL_GROUPS, POOL_GROUP_DIM, POOL_GROUP_DIM)), const((1, POOL_WIDTH))],
        out_specs=[tile(FOX_WIDTH), tile(FOX_WIDTH), tile(FOX_HEADS), per_b(POOL_HIST, POOL_WIDTH),
                   tile(POOL_WIDTH), tile(MEM_WIDTH)] + abc_specs,
        out_shape=[jax.ShapeDtypeStruct((B, S, FOX_WIDTH), F32),
                   jax.ShapeDtypeStruct((B, S, FOX_WIDTH), F32),
                   jax.ShapeDtypeStruct((B, S, FOX_HEADS), F32),
                   jax.ShapeDtypeStruct((B, POOL_HIST, POOL_WIDTH), F32),
                   jax.ShapeDtypeStruct((B, S, POOL_WIDTH), BF16),
                   jax.ShapeDtypeStruct((B, S, MEM_WIDTH), BF16)] + abc_shapes,
        scratch_shapes=[pltpu.VMEM((HIST_ROWS + tm, POOL_WIDTH), F32), pltpu.VMEM((1, LANES), F32)],
        compiler_params=pltpu.CompilerParams(dimension_semantics=("parallel", "arbitrary"),
                                             vmem_limit_bytes=VMEM_LIMIT),
        name="in_proj",
    )(x, g, w_rep, bf_rep, hist, mkb, mvb, wpool, spool)


def _fox_prompt_kernel(q_ref, k_ref, v_ref, o_ref, *, blk):
    i = pl.program_id(2)
    lane = _iota((blk, LANES), 1)
    outs = []
    for hh in range(2):
        cols = slice(hh * LANES, (hh + 1) * LANES)
        q = q_ref[0, :, cols]

        def step(start, m, acc, masked):
            s = _dot_nt(q, k_ref[0, pl.ds(start, blk), cols])
            if masked:
                s = jnp.where(_iota((blk, blk), 1) <= _iota((blk, blk), 0), s, -jnp.inf)
            m_new = jnp.maximum(m, jnp.max(s, axis=-1, keepdims=True))
            p = jnp.exp2(s - m_new)
            acc = jnp.exp2(m - m_new) * acc + _dot(p.astype(BF16), v_ref[0, pl.ds(start, blk), cols])
            return m_new, acc

        def body(j, carry):
            return step(pl.multiple_of(j * blk, blk), *carry, masked=False)

        m0 = jnp.full((blk, 1), -jnp.inf, F32)
        acc0 = jnp.zeros((blk, LANES), F32)
        m, acc = lax.fori_loop(0, i, body, (m0, acc0))
        m, acc = step(pl.multiple_of(i * blk, blk), m, acc, masked=True)
        outs.append(acc / acc[:, FOX_HEAD_DIM:FOX_HEAD_DIM + 1])
    o = jnp.where(lane < FOX_HEAD_DIM, outs[0], pltpu.roll(outs[1], FOX_HEAD_DIM, axis=1))
    o_ref[0] = o.astype(BF16)


def _fox_prompt(qa, ka, va, *, blk):
    B, S, _ = qa.shape
    pairs = FOX_HEADS // 2
    return pl.pallas_call(
        functools.partial(_fox_prompt_kernel, blk=blk),
        grid=(B, pairs, S // blk),
        in_specs=[pl.BlockSpec((1, blk, 2 * LANES), lambda b, p, i: (b, i, p)),
                  pl.BlockSpec((1, S, 2 * LANES), lambda b, p, i: (b, 0, p)),
                  pl.BlockSpec((1, S, 2 * LANES), lambda b, p, i: (b, 0, p))],
        out_specs=pl.BlockSpec((1, blk, LANES), lambda b, p, i: (b, i, p)),
        out_shape=jax.ShapeDtypeStruct((B, S, FOX_WIDTH), BF16),
        compiler_params=pltpu.CompilerParams(
            dimension_semantics=("parallel", "parallel", "arbitrary"), vmem_limit_bytes=VMEM_LIMIT),
        name="fox_prompt",
    )(qa, ka, va)


def _fox_sample_kernel(q_ref, kn_ref, vn_ref, lfn_ref, ck_ref, cv_ref, clf_ref, o_ref, *, chunk):
    T = q_ref.shape[1]
    P = ck_ref.shape[1]
    HT = FOX_HEADS * T
    t_shift = T.bit_length() - 1
    assert HT == LANES and 1 << t_shift == T

    q_rep = jnp.concatenate([q_ref[0]] * FOX_HEADS, axis=0)
    row_head = _iota((HT, FOX_WIDTH), 0) >> t_shift
    q_bd = jnp.where(row_head == (_iota((HT, FOX_WIDTH), 1) >> 6), q_rep, 0.0).astype(BF16)
    e_lane = _iota((HT, LANES), 1)
    e_bd = jnp.where((e_lane < BIAS_LANES) & ((e_lane & 7) == (_iota((HT, LANES), 0) >> t_shift)),
                     1.0, 0.0).astype(BF16)

    def scores(keys, c):
        return _dot_nt(keys.astype(BF16), q_bd) + _dot_nt(_bias_pieces(c).astype(BF16), e_bd)

    carry = jnp.zeros((1, LANES), F32)
    s_parts = []
    for ci in range(P // chunk):
        rows = slice(ci * chunk, (ci + 1) * chunk)
        c = _cumsum_rows(clf_ref[0, rows, :], chunk) + carry
        carry = c[chunk - 1:chunk, :]
        s_parts.append(scores(ck_ref[0, rows, :], c))
    pad_rows = lambda a: jnp.concatenate([a, jnp.zeros((LANES - T, a.shape[1]), F32)], axis=0)
    c_new = _cumsum_rows(pad_rows(lfn_ref[0]), LANES) + carry
    s_new = scores(pad_rows(kn_ref[0]), c_new)
    s_new = jnp.where(_iota((LANES, HT), 0) <= (_iota((LANES, HT), 1) & (T - 1)), s_new, -jnp.inf)

    m = jnp.max(s_new, axis=0, keepdims=True)
    for s in s_parts:
        m = jnp.maximum(m, jnp.max(s, axis=0, keepdims=True))
    ones = jnp.ones((chunk, LANES), BF16)
    p_new = jnp.exp2(s_new - m).T.astype(BF16)
    r = _dot(p_new, pad_rows(vn_ref[0]).astype(BF16))
    l = _dot(p_new, ones[:LANES])
    for ci, s in enumerate(s_parts):
        p = jnp.exp2(s - m).T.astype(BF16)
        r = r + _dot(p, cv_ref[0, ci * chunk:(ci + 1) * chunk, :].astype(BF16))
        l = l + _dot(p, ones)
    r = r / jnp.concatenate([l] * (FOX_WIDTH // LANES), axis=1)
    col_head = _iota((T, FOX_WIDTH), 1) >> 6
    o = jnp.zeros((T, FOX_WIDTH), F32)
    for h in range(FOX_HEADS):
        o = jnp.where(col_head == h, r[h * T:(h + 1) * T, :], o)
    o_ref[0] = o.astype(BF16)


def _fox_sample(q, k_new, v_new, lf_new, cache_k, cache_v, cache_lf_rep, *, chunk):
    B, T, _ = q.shape
    P = cache_k.shape[1]
    new = lambda w: pl.BlockSpec((1, T, w), lambda b: (b, 0, 0))
    old = lambda w: pl.BlockSpec((1, P, w), lambda b: (b, 0, 0))
    return pl.pallas_call(
        functools.partial(_fox_sample_kernel, chunk=chunk),
        grid=(B,),
        in_specs=[new(FOX_WIDTH), new(FOX_WIDTH), new(FOX_WIDTH), new(LANES),
                  old(FOX_WIDTH), old(FOX_WIDTH), old(LANES)],
        out_specs=new(FOX_WIDTH),
        out_shape=jax.ShapeDtypeStruct((B, T, FOX_WIDTH), BF16),
        compiler_params=pltpu.CompilerParams(dimension_semantics=("parallel",),
                                             vmem_limit_bytes=VMEM_LIMIT),
        name="fox_sample",
    )(q, k_new, v_new, lf_new, cache_k, cache_v, cache_lf_rep)


def _merge_kernel(x_ref, ofox_ref, opool_ref, omem_ref, g_ref, wg_ref, wbf_ref, wbp_ref, wbm_ref,
                  wout_ref, h_ref):
    D = x_ref.shape[-1]
    x = x_ref[0]
    xn = _rms(x, g_ref[...]).astype(BF16)
    m = None
    for bi, (o_ref, w_ref) in enumerate(((ofox_ref, wbf_ref), (opool_ref, wbp_ref), (omem_ref, wbm_ref))):
        gate = jax.nn.sigmoid(_dot(xn, wg_ref[:, bi * D:(bi + 1) * D]))
        term = gate * _dot(o_ref[0], w_ref[...])
        m = term if m is None else m + term
    h_ref[0] = x + _dot(m.astype(BF16), wout_ref[...])


def _merge(x, ofox, opool, omem, g, wg, wbf, wbp, wbm, wout, *, tm):
    B, S, D = x.shape
    tile = lambda w: pl.BlockSpec((1, tm, w), lambda b, t: (b, t, 0))
    const = lambda a: pl.BlockSpec(a.shape, lambda b, t: (0,) * a.ndim)
    return pl.pallas_call(
        _merge_kernel,
        grid=(B, S // tm),
        in_specs=[tile(D), tile(FOX_WIDTH), tile(POOL_WIDTH), tile(MEM_WIDTH),
                  const(g), const(wg), const(wbf), const(wbp), const(wbm), const(wout)],
        out_specs=tile(D),
        out_shape=jax.ShapeDtypeStruct((B, S, D), F32),
        compiler_params=pltpu.CompilerParams(dimension_semantics=("parallel", "parallel"),
                                             vmem_limit_bytes=VMEM_LIMIT),
        name="merge",
    )(x, ofox, opool, omem, g, wg, wbf, wbp, wbm, wout)


def _ffn_kernel(h_ref, g_ref, wup_ref, wdown_ref, gfin_ref, y_ref, *, ff_chunk, final_norm):
    h = h_ref[0]
    hn = _rms(h, g_ref[...]).astype(BF16)
    acc = h
    for c in range(wup_ref.shape[1] // ff_chunk):
        cols = slice(c * ff_chunk, (c + 1) * ff_chunk)
        a = jnp.maximum(_dot(hn, wup_ref[:, cols]), 0.0)
        acc = acc + _dot((a * a).astype(BF16), wdown_ref[cols, :])
    y_ref[0] = _rms(acc, gfin_ref[...]) if final_norm else acc


def _ffn(h, g, wup, wdown, gfin, *, tm, ff_chunk, final_norm):
    B, S, D = h.shape
    tile = pl.BlockSpec((1, tm, D), lambda b, t: (b, t, 0))
    const = lambda a: pl.BlockSpec(a.shape, lambda b, t: (0,) * a.ndim)
    return pl.pallas_call(
        functools.partial(_ffn_kernel, ff_chunk=ff_chunk, final_norm=final_norm),
        grid=(B, S // tm),
        in_specs=[tile, const(g), const(wup), const(wdown), const(gfin)],
        out_specs=tile,
        out_shape=jax.ShapeDtypeStruct((B, S, D), F32),
        compiler_params=pltpu.CompilerParams(dimension_semantics=("parallel", "parallel"),
                                             vmem_limit_bytes=VMEM_LIMIT),
        name="ffn",
    )(h, g, wup, wdown, gfin)


def _tile_rows(n, want):
    return want if n % want == 0 else n


def _repack_w_in(w_in):
    o_f = 3 * FOX_WIDTH
    o_u = o_f + FOX_HEADS
    o_g = o_u + POOL_WIDTH + MEM_WIDTH
    wf = w_in[:, o_f:o_u]
    wf_rep = jnp.concatenate([wf] * BIAS_PIECES
                             + [jnp.zeros((w_in.shape[0], LANES - BIAS_LANES), w_in.dtype)], axis=1)
    w_rep = jnp.concatenate([w_in[:, :o_f], w_in[:, o_u:o_g], wf_rep], axis=1).astype(BF16)
    return w_rep, w_in[:, o_g:].astype(BF16)


def _rep_lanes(a):
    pad = jnp.zeros(a.shape[:-1] + (LANES - BIAS_LANES,), a.dtype)
    return jnp.concatenate([a] * BIAS_PIECES + [pad], axis=-1)


def kernel(x_prompt, x_sample, mem_prompt, cache_fox_k, cache_fox_v, cache_fox_logf, cache_pool,
           cache_mem_k, cache_mem_v, g_attn, w_in, b_f, w_pool, s_pool, g_mem, w_mkv,
           w_br_fox, w_br_pool, w_br_mem, w_out, g_mlp, w_up, w_down, g_final):
    depth = w_in.shape[0]
    B, S, D = x_prompt.shape
    BS, T, _ = x_sample.shape
    P = cache_fox_k.shape[2]
    M = mem_prompt.shape[1]
    tm_p = _tile_rows(S, 256)
    blk_p = _tile_rows(S, 256)
    tm_s = _tile_rows(BS * T, 256)
    chunk_s = _tile_rows(P, 256)
    gfin = g_final.reshape(1, D)

    xp, xs = x_prompt, x_sample
    outs_p = [[] for _ in range(6)]
    outs_s = [[] for _ in range(4)]
    for l in range(depth):
        w_rep, wg = _repack_w_in(w_in[l])
        bf_rep = _rep_lanes(b_f[l]).reshape(1, LANES)
        g_a, g_m, g_f = g_attn[l].reshape(1, D), g_mem[l].reshape(1, D), g_mlp[l].reshape(1, D)
        wpool, spool = w_pool[l].astype(BF16), s_pool[l].reshape(1, POOL_WIDTH)
        wbf, wbp, wbm = (w.astype(BF16) for w in (w_br_fox[l], w_br_pool[l], w_br_mem[l]))
        wout, wup, wdown = (w.astype(BF16) for w in (w_out[l], w_up[l], w_down[l]))
        final = l == depth - 1

        mk, mv, mkb, mvb = _mem_kv(mem_prompt, g_m, w_mkv[l].astype(BF16))
        hist0 = jnp.zeros((B, HIST_ROWS, POOL_WIDTH), F32)
        k, v, logf, pnew, opool, omem, qa, ka, va = _in_proj(
            xp, g_a, w_rep, bf_rep, hist0, mkb, mvb, wpool, spool, tm=tm_p, pos0=0, augment=True)
        ofox = _fox_prompt(qa, ka, va, blk=blk_p)
        h = _merge(xp, ofox, opool, omem, g_a, wg, wbf, wbp, wbm, wout, tm=tm_p)
        xp = _ffn(h, g_f, wup, wdown, gfin, tm=tm_p, ff_chunk=1024, final_norm=final)
        for acc, val in zip(outs_p, (k.reshape(B, S, FOX_HEADS, FOX_HEAD_DIM),
                                     v.reshape(B, S, FOX_HEADS, FOX_HEAD_DIM), logf, pnew,
                                     mk.reshape(B, M, MEM_HEADS, MEM_HEAD_DIM),
                                     mv.reshape(B, M, MEM_HEADS, MEM_HEAD_DIM))):
            acc.append(val)

        hist = jnp.pad(cache_pool[l], ((0, 0), (HIST_ROWS - POOL_HIST, 0), (0, 0)))
        cmk = cache_mem_k[l].reshape(BS, M, MEM_WIDTH).astype(BF16)
        cmv = cache_mem_v[l].reshape(BS, M, MEM_WIDTH).astype(BF16)
        k, v, logf, pnew, opool, omem, q, lf_rep = _in_proj(
            xs, g_a, w_rep, bf_rep, hist, cmk, cmv, wpool, spool, tm=T, pos0=P, augment=False)
        ofox = _fox_sample(q, k, v, lf_rep, cache_fox_k[l].reshape(BS, P, FOX_WIDTH),
                           cache_fox_v[l].reshape(BS, P, FOX_WIDTH), _rep_lanes(cache_fox_logf[l]),
                           chunk=chunk_s)
        flat = lambda a: a.reshape(1, BS * T, a.shape[-1])
        h = _merge(flat(xs), flat(ofox), flat(opool), flat(omem), g_a, wg, wbf, wbp, wbm, wout, tm=tm_s)
        xs = _ffn(h, g_f, wup, wdown, gfin, tm=tm_s, ff_chunk=1024, final_norm=final).reshape(BS, T, D)
        for acc, val in zip(outs_s, (k.reshape(BS, T, FOX_HEADS, FOX_HEAD_DIM),
                                     v.reshape(BS, T, FOX_HEADS, FOX_HEAD_DIM), logf, pnew)):
            acc.append(val)

    stack = lambda o: o[0][None] if depth == 1 else jnp.stack(o)
    return (xp, xs, *(stack(o) for o in outs_p), *(stack(o) for o in outs_s))
```

```python
import functools
import math

import jax
import jax.numpy as jnp
from jax import lax
from jax.experimental import pallas as pl
from jax.experimental.pallas import tpu as pltpu

F32 = jnp.float32
BF16 = jnp.bfloat16

EPS = 1e-6
LANES = 128
FOX_HEADS = 8
FOX_HEAD_DIM = 64
FOX_WIDTH = FOX_HEADS * FOX_HEAD_DIM
POOL_WINDOWS = (2, 4, 8, 16)
POOL_GROUPS = 4
POOL_GROUP_DIM = 128
POOL_WIDTH = POOL_GROUPS * POOL_GROUP_DIM
POOL_HIST = 15
HIST_ROWS = 16
MEM_HEADS = 4
MEM_HEAD_DIM = 128
MEM_WIDTH = MEM_HEADS * MEM_HEAD_DIM
N_BRANCH = 3
LOG2E = math.log2(math.e)
BIAS_PIECES = 3
BIAS_LANES = BIAS_PIECES * FOX_HEADS
VMEM_LIMIT = 56 * 1024 * 1024


def _dot(a, b):
    return jnp.dot(a, b, preferred_element_type=F32)


def _dot_nt(a, b):
    return lax.dot_general(a, b, (((1,), (1,)), ((), ())), preferred_element_type=F32)


def _rms(x, g):
    return x * lax.rsqrt(jnp.mean(x * x, axis=-1, keepdims=True) + EPS) * g


def _iota(shape, dim):
    return lax.broadcasted_iota(jnp.int32, shape, dim)


def _split3(x):
    p0 = x.astype(BF16)
    r1 = x - p0.astype(F32)
    p1 = r1.astype(BF16)
    p2 = (r1 - p1.astype(F32)).astype(BF16)
    return p0, p1, p2


def _cumsum_rows(x, rows_out):
    n = x.shape[0]
    tril = (_iota((rows_out, n), 1) <= _iota((rows_out, n), 0)).astype(BF16)
    p0, p1, p2 = _split3(x)
    return _dot(tril, p0) + _dot(tril, p1) + _dot(tril, p2)


def _bias_pieces(c):
    b = c * (-LOG2E)
    p0 = b.astype(BF16).astype(F32)
    r1 = b - p0
    p1 = r1.astype(BF16).astype(F32)
    p2 = (r1 - p1).astype(BF16).astype(F32)
    grp = _iota(c.shape, 1) >> 3
    return jnp.where(grp == 0, p0, jnp.where(grp == 1, p1, jnp.where(grp == 2, p2, 0.0)))


def _mem_kv_kernel(mem_ref, g_ref, w_ref, mk_ref, mv_ref, mkb_ref, mvb_ref):
    xn = _rms(mem_ref[0], g_ref[...]).astype(BF16)
    mk = _dot(xn, w_ref[:, :MEM_WIDTH])
    mv = _dot(xn, w_ref[:, MEM_WIDTH:])
    mk_ref[0] = mk
    mv_ref[0] = mv
    mkb_ref[0] = mk.astype(BF16)
    mvb_ref[0] = mv.astype(BF16)


def _mem_kv(mem, g, w_mkv):
    B, M, D = mem.shape
    blk = lambda w: pl.BlockSpec((1, M, w), lambda b: (b, 0, 0))
    return pl.pallas_call(
        _mem_kv_kernel,
        grid=(B,),
        in_specs=[blk(D), pl.BlockSpec((1, D), lambda b: (0, 0)),
                  pl.BlockSpec((D, 2 * MEM_WIDTH), lambda b: (0, 0))],
        out_specs=[blk(MEM_WIDTH)] * 4,
        out_shape=[jax.ShapeDtypeStruct((B, M, MEM_WIDTH), F32)] * 2
        + [jax.ShapeDtypeStruct((B, M, MEM_WIDTH), BF16)] * 2,
        compiler_params=pltpu.CompilerParams(dimension_semantics=("parallel",),
                                             vmem_limit_bytes=VMEM_LIMIT),
        name="mem_kv",
    )(mem, g, w_mkv)


_OFF_Q, _OFF_K, _OFF_V, _OFF_U, _OFF_QM, _OFF_F = 0, 512, 1024, 1536, 2048, 2560
_IN_COLS = _OFF_F + LANES


def _in_proj_kernel(x_ref, g_ref, w_ref, bf_ref, hist_ref, mk_ref, mv_ref, wpool_ref, spool_ref,
                    k_ref, v_ref, logf_ref, pnew_ref, opool_ref, omem_ref, *rest, tm, pos0, augment):
    extra, (uext_ref, carry_ref) = rest[:-2], rest[-2:]
    t = pl.program_id(1)
    last = pl.num_programs(1) - 1

    @pl.when(t == 0)
    def _():
        uext_ref[0:HIST_ROWS, :] = hist_ref[0]
        carry_ref[...] = jnp.zeros_like(carry_ref)

    xn = _rms(x_ref[0], g_ref[...]).astype(BF16)
    lane = _iota((tm, LANES), 1)

    q = _dot(xn, w_ref[:, _OFF_Q:_OFF_K]) * (FOX_HEAD_DIM ** -0.5 * LOG2E)
    k = _dot(xn, w_ref[:, _OFF_K:_OFF_V])
    v = _dot(xn, w_ref[:, _OFF_V:_OFF_U])
    k_ref[0] = k
    v_ref[0] = v
    zf = _dot(xn, w_ref[:, _OFF_F:_IN_COLS]) + bf_ref[...]
    logf = jnp.minimum(zf, 0.0) - jnp.log1p(jnp.exp(-jnp.abs(zf)))
    logf_ref[0] = logf[:, :FOX_HEADS]

    if augment:
        c = _cumsum_rows(logf, tm) + carry_ref[...]
        carry_ref[...] = c[tm - 1:tm, :]
        bias = pltpu.roll(_bias_pieces(c), FOX_HEAD_DIM, axis=1)
        in_bias = (lane >= FOX_HEAD_DIM) & (lane < FOX_HEAD_DIM + BIAS_LANES)
        q_tail = jnp.where(in_bias, 1.0, 0.0)
        v_tail = jnp.where(lane == FOX_HEAD_DIM, 1.0, 0.0)
        for pair in range(FOX_HEADS // 2):
            cols = slice(pair * LANES, (pair + 1) * LANES)
            for (src, kind), dst in zip(((q, 0), (k, 1), (v, 2)), extra):
                even = src[:, cols]
                odd = pltpu.roll(even, FOX_HEAD_DIM, axis=1)
                for par, data in ((0, even), (1, odd)):
                    h = 2 * pair + par
                    if kind == 0:
                        tail = q_tail
                    elif kind == 1:
                        tail = jnp.where(in_bias & ((lane & 7) == h), bias, 0.0)
                    else:
                        tail = v_tail
                    blk = jnp.where(lane < FOX_HEAD_DIM, data, tail)
                    dst[0, :, h * LANES:(h + 1) * LANES] = blk.astype(BF16)
    else:
        q_ref, lfrep_ref = extra
        q_ref[0] = q
        lfrep_ref[0] = logf

    u = _dot(xn, w_ref[:, _OFF_U:_OFF_QM])
    uext_ref[HIST_ROWS:HIST_ROWS + tm, :] = u
    pos = pos0 + t * tm + _iota((tm, 1), 0)
    for gi, w in enumerate(POOL_WINDOWS):
        cols = slice(gi * POOL_GROUP_DIM, (gi + 1) * POOL_GROUP_DIM)
        ug = u[:, cols]
        sw = ug
        for back in range(1, w):
            sw = sw + uext_ref[HIST_ROWS - back:HIST_ROWS - back + tm, cols]
        cnt = jnp.minimum(w, pos + 1).astype(F32)
        pooled = sw / cnt - ug
        mixed = _dot(pooled.astype(BF16), wpool_ref[gi]) * spool_ref[:, cols]
        opool_ref[0, :, cols] = mixed.astype(BF16)

    @pl.when(t == last)
    def _():
        pnew_ref[0] = uext_ref[tm + 1:tm + HIST_ROWS, :]

    uext_ref[0:HIST_ROWS, :] = uext_ref[tm:tm + HIST_ROWS, :]

    for h in range(MEM_HEADS):
        cols = slice(h * MEM_HEAD_DIM, (h + 1) * MEM_HEAD_DIM)
        qm = _dot(xn, w_ref[:, _OFF_QM + h * MEM_HEAD_DIM:_OFF_QM + (h + 1) * MEM_HEAD_DIM])
        s = _dot_nt(qm.astype(BF16), mk_ref[0, :, cols]) * MEM_HEAD_DIM ** -0.5
        e = jnp.exp(s - jnp.max(s, axis=-1, keepdims=True))
        o = _dot(e.astype(BF16), mv_ref[0, :, cols]) / jnp.sum(e, axis=-1, keepdims=True)
        omem_ref[0, :, cols] = o.astype(BF16)


def _in_proj(x, g, w_rep, bf_rep, hist, mkb, mvb, wpool, spool, *, tm, pos0, augment):
    B, S, D = x.shape
    M = mkb.shape[1]
    nt = S // tm
    tile = lambda w: pl.BlockSpec((1, tm, w), lambda b, t: (b, t, 0))
    per_b = lambda r, w: pl.BlockSpec((1, r, w), lambda b, t: (b, 0, 0))
    const = lambda shape: pl.BlockSpec(shape, lambda b, t: (0,) * len(shape))
    if augment:
        abc_specs = [tile(FOX_HEADS * LANES)] * 3
        abc_shapes = [jax.ShapeDtypeStruct((B, S, FOX_HEADS * LANES), BF16)] * 3
    else:
        abc_specs = [tile(FOX_WIDTH), tile(LANES)]
        abc_shapes = [jax.ShapeDtypeStruct((B, S, FOX_WIDTH), F32),
                      jax.ShapeDtypeStruct((B, S, LANES), F32)]
    return pl.pallas_call(
        functools.partial(_in_proj_kernel, tm=tm, pos0=pos0, augment=augment),
        grid=(B, nt),
        in_specs=[tile(D), const((1, D)), const((D, _IN_COLS)), const((1, LANES)),
                  per_b(HIST_ROWS, POOL_WIDTH), per_b(M, MEM_WIDTH), per_b(M, MEM_WIDTH),
                  const((POOL_GROUPS, POOL_GROUP_DIM, POOL_GROUP_DIM)), const((1, POOL_WIDTH))],
        out_specs=[tile(FOX_WIDTH), tile(FOX_WIDTH), tile(FOX_HEADS), per_b(POOL_HIST, POOL_WIDTH),
                   tile(POOL_WIDTH), tile(MEM_WIDTH)] + abc_specs,
        out_shape=[jax.ShapeDtypeStruct((B, S, FOX_WIDTH), F32),
                   jax.ShapeDtypeStruct((B, S, FOX_WIDTH), F32),
                   jax.ShapeDtypeStruct((B, S, FOX_HEADS), F32),
                   jax.ShapeDtypeStruct((B, POOL_HIST, POOL_WIDTH), F32),
                   jax.ShapeDtypeStruct((B, S, POOL_WIDTH), BF16),
                   jax.ShapeDtypeStruct((B, S, MEM_WIDTH), BF16)] + abc_shapes,
        scratch_shapes=[pltpu.VMEM((HIST_ROWS + tm, POOL_WIDTH), F32), pltpu.VMEM((1, LANES), F32)],
        compiler_params=pltpu.CompilerParams(dimension_semantics=("parallel", "arbitrary"),
                                             vmem_limit_bytes=VMEM_LIMIT),
        name="in_proj",
    )(x, g, w_rep, bf_rep, hist, mkb, mvb, wpool, spool)


def _fox_prompt_kernel(q_ref, k_ref, v_ref, o_ref, m_ref, acc_ref, *, blk):
    i = pl.program_id(1)
    m_ref[...] = jnp.full_like(m_ref, -jnp.inf)
    acc_ref[...] = jnp.zeros_like(acc_ref)

    def step(start, masked):
        for h in range(FOX_HEADS):
            cols = slice(h * LANES, (h + 1) * LANES)
            s = _dot_nt(q_ref[0, :, cols], k_ref[0, pl.ds(start, blk), cols])
            if masked:
                s = jnp.where(_iota((blk, blk), 1) <= _iota((blk, blk), 0), s, -jnp.inf)
            m_prev = m_ref[h]
            m_new = jnp.maximum(m_prev, jnp.max(s, axis=-1, keepdims=True))
            p = jnp.exp2(s - jnp.concatenate([m_new] * (blk // LANES), axis=1))
            pv = _dot(p.astype(BF16), v_ref[0, pl.ds(start, blk), cols])
            acc_ref[h] = jnp.exp2(m_prev - m_new) * acc_ref[h] + pv
            m_ref[h] = m_new

    def body(j, carry):
        step(pl.multiple_of(j * blk, blk), masked=False)
        return carry

    lax.fori_loop(0, i, body, 0)
    step(pl.multiple_of(i * blk, blk), masked=True)

    lane = _iota((blk, LANES), 1)
    for pair in range(FOX_HEADS // 2):
        even, odd = (acc_ref[2 * pair + par] for par in (0, 1))
        even = even / even[:, FOX_HEAD_DIM:FOX_HEAD_DIM + 1]
        odd = odd / odd[:, FOX_HEAD_DIM:FOX_HEAD_DIM + 1]
        o = jnp.where(lane < FOX_HEAD_DIM, even, pltpu.roll(odd, FOX_HEAD_DIM, axis=1))
        o_ref[0, :, pair * LANES:(pair + 1) * LANES] = o.astype(BF16)


def _fox_prompt(qa, ka, va, *, blk):
    B, S, W = qa.shape
    return pl.pallas_call(
        functools.partial(_fox_prompt_kernel, blk=blk),
        grid=(B, S // blk),
        in_specs=[pl.BlockSpec((1, blk, W), lambda b, i: (b, i, 0)),
                  pl.BlockSpec((1, S, W), lambda b, i: (b, 0, 0)),
                  pl.BlockSpec((1, S, W), lambda b, i: (b, 0, 0))],
        out_specs=pl.BlockSpec((1, blk, FOX_WIDTH), lambda b, i: (b, i, 0)),
        out_shape=jax.ShapeDtypeStruct((B, S, FOX_WIDTH), BF16),
        scratch_shapes=[pltpu.VMEM((FOX_HEADS, blk, LANES), F32), pltpu.VMEM((FOX_HEADS, blk, LANES), F32)],
        compiler_params=pltpu.CompilerParams(
            dimension_semantics=("parallel", "arbitrary"), vmem_limit_bytes=VMEM_LIMIT),
        name="fox_prompt",
    )(qa, ka, va)


def _fox_sample_kernel(q_ref, kn_ref, vn_ref, lfn_ref, ck_ref, cv_ref, clf_ref, o_ref, *, chunk):
    T = q_ref.shape[1]
    P = ck_ref.shape[1]
    HT = FOX_HEADS * T
    t_shift = T.bit_length() - 1
    assert HT == LANES and 1 << t_shift == T

    q_rep = jnp.concatenate([q_ref[0]] * FOX_HEADS, axis=0)
    row_head = _iota((HT, FOX_WIDTH), 0) >> t_shift
    q_bd = jnp.where(row_head == (_iota((HT, FOX_WIDTH), 1) >> 6), q_rep, 0.0).astype(BF16)
    e_lane = _iota((HT, LANES), 1)
    e_bd = jnp.where((e_lane < BIAS_LANES) & ((e_lane & 7) == (_iota((HT, LANES), 0) >> t_shift)),
                     1.0, 0.0).astype(BF16)

    def scores(keys, c):
        return _dot_nt(keys.astype(BF16), q_bd) + _dot_nt(_bias_pieces(c).astype(BF16), e_bd)

    carry = jnp.zeros((1, LANES), F32)
    s_parts = []
    for ci in range(P // chunk):
        rows = slice(ci * chunk, (ci + 1) * chunk)
        c = _cumsum_rows(clf_ref[0, rows, :], chunk) + carry
        carry = c[chunk - 1:chunk, :]
        s_parts.append(scores(ck_ref[0, rows, :], c))
    pad_rows = lambda a: jnp.concatenate([a, jnp.zeros((LANES - T, a.shape[1]), F32)], axis=0)
    c_new = _cumsum_rows(pad_rows(lfn_ref[0]), LANES) + carry
    s_new = scores(pad_rows(kn_ref[0]), c_new)
    s_new = jnp.where(_iota((LANES, HT), 0) <= (_iota((LANES, HT), 1) & (T - 1)), s_new, -jnp.inf)

    m = jnp.max(s_new, axis=0, keepdims=True)
    for s in s_parts:
        m = jnp.maximum(m, jnp.max(s, axis=0, keepdims=True))
    ones = jnp.ones((chunk, LANES), BF16)
    p_new = jnp.exp2(s_new - m).T.astype(BF16)
    r = _dot(p_new, pad_rows(vn_ref[0]).astype(BF16))
    l = _dot(p_new, ones[:LANES])
    for ci, s in enumerate(s_parts):
        p = jnp.exp2(s - m).T.astype(BF16)
        r = r + _dot(p, cv_ref[0, ci * chunk:(ci + 1) * chunk, :].astype(BF16))
        l = l + _dot(p, ones)
    r = r / jnp.concatenate([l] * (FOX_WIDTH // LANES), axis=1)
    col_head = _iota((T, FOX_WIDTH), 1) >> 6
    o = jnp.zeros((T, FOX_WIDTH), F32)
    for h in range(FOX_HEADS):
        o = jnp.where(col_head == h, r[h * T:(h + 1) * T, :], o)
    o_ref[0] = o.astype(BF16)


def _fox_sample(q, k_new, v_new, lf_new, cache_k, cache_v, cache_lf_rep, *, chunk):
    B, T, _ = q.shape
    P = cache_k.shape[1]
    new = lambda w: pl.BlockSpec((1, T, w), lambda b: (b, 0, 0))
    old = lambda w: pl.BlockSpec((1, P, w), lambda b: (b, 0, 0))
    return pl.pallas_call(
        functools.partial(_fox_sample_kernel, chunk=chunk),
        grid=(B,),
        in_specs=[new(FOX_WIDTH), new(FOX_WIDTH), new(FOX_WIDTH), new(LANES),
                  old(FOX_WIDTH), old(FOX_WIDTH), old(LANES)],
        out_specs=new(FOX_WIDTH),
        out_shape=jax.ShapeDtypeStruct((B, T, FOX_WIDTH), BF16),
        compiler_params=pltpu.CompilerParams(dimension_semantics=("parallel",),
                                             vmem_limit_bytes=VMEM_LIMIT),
        name="fox_sample",
    )(q, k_new, v_new, lf_new, cache_k, cache_v, cache_lf_rep)


def _merge_kernel(x_ref, ofox_ref, opool_ref, omem_ref, g_ref, wg_ref, wbf_ref, wbp_ref, wbm_ref,
                  wout_ref, h_ref):
    D = x_ref.shape[-1]
    x = x_ref[0]
    xn = _rms(x, g_ref[...]).astype(BF16)
    m = None
    for bi, (o_ref, w_ref) in enumerate(((ofox_ref, wbf_ref), (opool_ref, wbp_ref), (omem_ref, wbm_ref))):
        gate = jax.nn.sigmoid(_dot(xn, wg_ref[:, bi * D:(bi + 1) * D]))
        term = gate * _dot(o_ref[0], w_ref[...])
        m = term if m is None else m + term
    h_ref[0] = x + _dot(m.astype(BF16), wout_ref[...])


def _merge(x, ofox, opool, omem, g, wg, wbf, wbp, wbm, wout, *, tm):
    B, S, D = x.shape
    tile = lambda w: pl.BlockSpec((1, tm, w), lambda b, t: (b, t, 0))
    const = lambda a: pl.BlockSpec(a.shape, lambda b, t: (0,) * a.ndim)
    return pl.pallas_call(
        _merge_kernel,
        grid=(B, S // tm),
        in_specs=[tile(D), tile(FOX_WIDTH), tile(POOL_WIDTH), tile(MEM_WIDTH),
                  const(g), const(wg), const(wbf), const(wbp), const(wbm), const(wout)],
        out_specs=tile(D),
        out_shape=jax.ShapeDtypeStruct((B, S, D), F32),
        compiler_params=pltpu.CompilerParams(dimension_semantics=("parallel", "parallel"),
                                             vmem_limit_bytes=VMEM_LIMIT),
        name="merge",
    )(x, ofox, opool, omem, g, wg, wbf, wbp, wbm, wout)


def _ffn_kernel(h_ref, g_ref, wup_ref, wdown_ref, gfin_ref, y_ref, *, ff_chunk, final_norm):
    h = h_ref[0]
    hn = _rms(h, g_ref[...]).astype(BF16)
    acc = h
    for c in range(wup_ref.shape[1] // ff_chunk):
        cols = slice(c * ff_chunk, (c + 1) * ff_chunk)
        a = jnp.maximum(_dot(hn, wup_ref[:, cols]), 0.0)
        acc = acc + _dot((a * a).astype(BF16), wdown_ref[cols, :])
    y_ref[0] = _rms(acc, gfin_ref[...]) if final_norm else acc


def _ffn(h, g, wup, wdown, gfin, *, tm, ff_chunk, final_norm):
    B, S, D = h.shape
    tile = pl.BlockSpec((1, tm, D), lambda b, t: (b, t, 0))
    const = lambda a: pl.BlockSpec(a.shape, lambda b, t: (0,) * a.ndim)
    return pl.pallas_call(
        functools.partial(_ffn_kernel, ff_chunk=ff_chunk, final_norm=final_norm),
        grid=(B, S // tm),
        in_specs=[tile, const(g), const(wup), const(wdown), const(gfin)],
        out_specs=tile,
        out_shape=jax.ShapeDtypeStruct((B, S, D), F32),
        compiler_params=pltpu.CompilerParams(dimension_semantics=("parallel", "parallel"),
                                             vmem_limit_bytes=VMEM_LIMIT),
        name="ffn",
    )(h, g, wup, wdown, gfin)


def _tile_rows(n, want):
    return want if n % want == 0 else n


def _repack_w_in(w_in):
    o_f = 3 * FOX_WIDTH
    o_u = o_f + FOX_HEADS
    o_g = o_u + POOL_WIDTH + MEM_WIDTH
    wf = w_in[:, o_f:o_u]
    wf_rep = jnp.concatenate([wf] * BIAS_PIECES
                             + [jnp.zeros((w_in.shape[0], LANES - BIAS_LANES), w_in.dtype)], axis=1)
    w_rep = jnp.concatenate([w_in[:, :o_f], w_in[:, o_u:o_g], wf_rep], axis=1).astype(BF16)
    return w_rep, w_in[:, o_g:].astype(BF16)


def _rep_lanes(a):
    pad = jnp.zeros(a.shape[:-1] + (LANES - BIAS_LANES,), a.dtype)
    return jnp.concatenate([a] * BIAS_PIECES + [pad], axis=-1)


def kernel(x_prompt, x_sample, mem_prompt, cache_fox_k, cache_fox_v, cache_fox_logf, cache_pool,
           cache_mem_k, cache_mem_v, g_attn, w_in, b_f, w_pool, s_pool, g_mem, w_mkv,
           w_br_fox, w_br_pool, w_br_mem, w_out, g_mlp, w_up, w_down, g_final):
    depth = w_in.shape[0]
    B, S, D = x_prompt.shape
    BS, T, _ = x_sample.shape
    P = cache_fox_k.shape[2]
    M = mem_prompt.shape[1]
    tm_p = _tile_rows(S, 256)
    blk_p = _tile_rows(S, 256)
    tm_s = _tile_rows(BS * T, 256)
    chunk_s = _tile_rows(P, 256)
    gfin = g_final.reshape(1, D)

    xp, xs = x_prompt, x_sample
    outs_p = [[] for _ in range(6)]
    outs_s = [[] for _ in range(4)]
    for l in range(depth):
        w_rep, wg = _repack_w_in(w_in[l])
        bf_rep = _rep_lanes(b_f[l]).reshape(1, LANES)
        g_a, g_m, g_f = g_attn[l].reshape(1, D), g_mem[l].reshape(1, D), g_mlp[l].reshape(1, D)
        wpool, spool = w_pool[l].astype(BF16), s_pool[l].reshape(1, POOL_WIDTH)
        wbf, wbp, wbm = (w.astype(BF16) for w in (w_br_fox[l], w_br_pool[l], w_br_mem[l]))
        wout, wup, wdown = (w.astype(BF16) for w in (w_out[l], w_up[l], w_down[l]))
        final = l == depth - 1

        mk, mv, mkb, mvb = _mem_kv(mem_prompt, g_m, w_mkv[l].astype(BF16))
        hist0 = jnp.zeros((B, HIST_ROWS, POOL_WIDTH), F32)
        k, v, logf, pnew, opool, omem, qa, ka, va = _in_proj(
            xp, g_a, w_rep, bf_rep, hist0, mkb, mvb, wpool, spool, tm=tm_p, pos0=0, augment=True)
        ofox = _fox_prompt(qa, ka, va, blk=blk_p)
        h = _merge(xp, ofox, opool, omem, g_a, wg, wbf, wbp, wbm, wout, tm=tm_p)
        xp = _ffn(h, g_f, wup, wdown, gfin, tm=tm_p, ff_chunk=1024, final_norm=final)
        for acc, val in zip(outs_p, (k.reshape(B, S, FOX_HEADS, FOX_HEAD_DIM),
                                     v.reshape(B, S, FOX_HEADS, FOX_HEAD_DIM), logf, pnew,
                                     mk.reshape(B, M, MEM_HEADS, MEM_HEAD_DIM),
                                     mv.reshape(B, M, MEM_HEADS, MEM_HEAD_DIM))):
            acc.append(val)

        hist = jnp.pad(cache_pool[l], ((0, 0), (HIST_ROWS - POOL_HIST, 0), (0, 0)))
        cmk = cache_mem_k[l].reshape(BS, M, MEM_WIDTH).astype(BF16)
        cmv = cache_mem_v[l].reshape(BS, M, MEM_WIDTH).astype(BF16)
        k, v, logf, pnew, opool, omem, q, lf_rep = _in_proj(
            xs, g_a, w_rep, bf_rep, hist, cmk, cmv, wpool, spool, tm=T, pos0=P, augment=False)
        ofox = _fox_sample(q, k, v, lf_rep, cache_fox_k[l].reshape(BS, P, FOX_WIDTH),
                           cache_fox_v[l].reshape(BS, P, FOX_WIDTH), _rep_lanes(cache_fox_logf[l]),
                           chunk=chunk_s)
        flat = lambda a: a.reshape(1, BS * T, a.shape[-1])
        h = _merge(flat(xs), flat(ofox), flat(opool), flat(omem), g_a, wg, wbf, wbp, wbm, wout, tm=tm_s)
        xs = _ffn(h, g_f, wup, wdown, gfin, tm=tm_s, ff_chunk=1024, final_norm=final).reshape(BS, T, D)
        for acc, val in zip(outs_s, (k.reshape(BS, T, FOX_HEADS, FOX_HEAD_DIM),
                                     v.reshape(BS, T, FOX_HEADS, FOX_HEAD_DIM), logf, pnew)):
            acc.append(val)

    stack = lambda o: o[0][None] if depth == 1 else jnp.stack(o)
    return (xp, xs, *(stack(o) for o in outs_p), *(stack(o) for o in outs_s))
```

```python
import functools
import math

import jax
import jax.numpy as jnp
from jax import lax
from jax.experimental import pallas as pl
from jax.experimental.pallas import tpu as pltpu

F32 = jnp.float32
BF16 = jnp.bfloat16

EPS = 1e-6
LANES = 128
FOX_HEADS = 8
FOX_HEAD_DIM = 64
FOX_WIDTH = FOX_HEADS * FOX_HEAD_DIM
POOL_WINDOWS = (2, 4, 8, 16)
POOL_GROUPS = 4
POOL_GROUP_DIM = 128
POOL_WIDTH = POOL_GROUPS * POOL_GROUP_DIM
POOL_HIST = 15
HIST_ROWS = 16
MEM_HEADS = 4
MEM_HEAD_DIM = 128
MEM_WIDTH = MEM_HEADS * MEM_HEAD_DIM
N_BRANCH = 3
LOG2E = math.log2(math.e)
BIAS_PIECES = 3
BIAS_LANES = BIAS_PIECES * FOX_HEADS
SCORES_AHEAD = 5
CUMSUM_CHUNK = 128
VMEM_LIMIT = 56 * 1024 * 1024


def _dot(a, b):
    return jnp.dot(a, b, preferred_element_type=F32)


def _dot_nt(a, b):
    return lax.dot_general(a, b, (((1,), (1,)), ((), ())), preferred_element_type=F32)


def _rms(x, g):
    return x * lax.rsqrt(jnp.mean(x * x, axis=-1, keepdims=True) + EPS) * g


def _iota(shape, dim):
    return lax.broadcasted_iota(jnp.int32, shape, dim)


def _split3(x):
    p0 = x.astype(BF16)
    r1 = x - p0.astype(F32)
    p1 = r1.astype(BF16)
    p2 = (r1 - p1.astype(F32)).astype(BF16)
    return p0, p1, p2


def _cumsum_rows(x, carry):
    n = x.shape[0]
    ch = min(n, CUMSUM_CHUNK)
    tril = (_iota((ch, ch), 1) <= _iota((ch, ch), 0)).astype(BF16)
    outs = []
    for c0 in range(0, n, ch):
        r = _dot(tril, jnp.concatenate(_split3(x[c0:c0 + ch]), axis=1))
        c = r[:, :LANES] + r[:, LANES:2 * LANES] + r[:, 2 * LANES:] + carry
        carry = c[ch - 1:ch]
        outs.append(c)
    return (outs[0] if len(outs) == 1 else jnp.concatenate(outs, axis=0)), carry


def _bias_pieces(c):
    b = c * (-LOG2E)
    p0 = b.astype(BF16).astype(F32)
    r1 = b - p0
    p1 = r1.astype(BF16).astype(F32)
    p2 = (r1 - p1).astype(BF16).astype(F32)
    grp = _iota(c.shape, 1) >> 3
    return jnp.where(grp == 0, p0, jnp.where(grp == 1, p1, jnp.where(grp == 2, p2, 0.0)))


def _mem_kv_kernel(mem_ref, g_ref, w_ref, mk_ref, mv_ref, mkb_ref, mvb_ref):
    xn = _rms(mem_ref[0], g_ref[...]).astype(BF16)
    mk = _dot(xn, w_ref[:, :MEM_WIDTH])
    mv = _dot(xn, w_ref[:, MEM_WIDTH:])
    mk_ref[0] = mk
    mv_ref[0] = mv
    mkb_ref[0] = mk.astype(BF16)
    mvb_ref[0] = mv.astype(BF16)


def _mem_kv(mem, g, w_mkv):
    B, M, D = mem.shape
    blk = lambda w: pl.BlockSpec((1, M, w), lambda b: (b, 0, 0))
    return pl.pallas_call(
        _mem_kv_kernel,
        grid=(B,),
        in_specs=[blk(D), pl.BlockSpec((1, D), lambda b: (0, 0)),
                  pl.BlockSpec((D, 2 * MEM_WIDTH), lambda b: (0, 0))],
        out_specs=[blk(MEM_WIDTH)] * 4,
        out_shape=[jax.ShapeDtypeStruct((B, M, MEM_WIDTH), F32)] * 2
        + [jax.ShapeDtypeStruct((B, M, MEM_WIDTH), BF16)] * 2,
        compiler_params=pltpu.CompilerParams(dimension_semantics=("parallel",),
                                             vmem_limit_bytes=VMEM_LIMIT),
        name="mem_kv",
    )(mem, g, w_mkv)


_OFF_K, _OFF_V, _OFF_U, _OFF_QM, _OFF_F = 0, 512, 1024, 1536, 2048
_IN_COLS = _OFF_F + LANES


def _in_proj_kernel(x_ref, g_ref, w_ref, wt_ref, bf_ref, hist_ref, mk_ref, mv_ref, wpool_ref, spool_ref,
                    k_ref, v_ref, logf_ref, pnew_ref, opool_ref, omem_ref, *rest, tm, pos0, augment, blk):
    extra, (uext_ref, carry_ref) = rest[:-2], rest[-2:]
    t = pl.program_id(1)
    last = pl.num_programs(1) - 1
    q_scale = FOX_HEAD_DIM ** -0.5 * LOG2E

    @pl.when(t == 0)
    def _():
        uext_ref[0:HIST_ROWS, :] = hist_ref[0]
        carry_ref[...] = jnp.zeros_like(carry_ref)

    xn = _rms(x_ref[0], g_ref[...]).astype(BF16)
    lane = _iota((tm, LANES), 1)
    mem_cols = [slice(h * MEM_HEAD_DIM, (h + 1) * MEM_HEAD_DIM) for h in range(MEM_HEADS)]

    zf = _dot(xn, w_ref[:, _OFF_F:_IN_COLS]) + bf_ref[...]
    k = _dot(xn, w_ref[:, _OFF_K:_OFF_V])
    v = _dot(xn, w_ref[:, _OFF_V:_OFF_U])
    u = _dot(xn, w_ref[:, _OFF_U:_OFF_QM])
    qm = _dot(xn, w_ref[:, _OFF_QM:_OFF_F]).astype(BF16)
    if augment:
        qt_all = (_dot_nt(wt_ref[:FOX_WIDTH, :], xn) * q_scale).astype(BF16)
        vt_all = _dot_nt(wt_ref[FOX_WIDTH:, :], xn).astype(BF16)
    else:
        q = _dot_nt(xn, wt_ref[:FOX_WIDTH, :]) * q_scale
    mem_s = [_dot_nt(qm[:, cols], mk_ref[0, :, cols]) * MEM_HEAD_DIM ** -0.5 for cols in mem_cols]

    k_ref[0] = k
    v_ref[0] = v
    logf = jnp.minimum(zf, 0.0) - jnp.log1p(jnp.exp(-jnp.abs(zf)))
    logf_ref[0] = logf[:, :FOX_HEADS]

    if augment:
        qt_ref, ka_ref, vt_ref = extra
        c, carry_ref[...] = _cumsum_rows(logf, carry_ref[...])
        bias = pltpu.roll(_bias_pieces(c), FOX_HEAD_DIM, axis=1)
        in_bias = (lane >= FOX_HEAD_DIM) & (lane < FOX_HEAD_DIM + BIAS_LANES)
        for pair in range(FOX_HEADS // 2):
            even = k[:, pair * LANES:(pair + 1) * LANES]
            odd = pltpu.roll(even, FOX_HEAD_DIM, axis=1)
            for par, data in ((0, even), (1, odd)):
                h = 2 * pair + par
                tail = jnp.where(in_bias & ((lane & 7) == h), bias, 0.0)
                ka_ref[0, :, h * LANES:(h + 1) * LANES] = jnp.where(lane < FOX_HEAD_DIM, data, tail).astype(BF16)
        tail_row = _iota((LANES - FOX_HEAD_DIM, blk), 0)
        q_tail = jnp.where(tail_row < BIAS_LANES, 1.0, 0.0).astype(BF16)
        v_tail = jnp.where(tail_row == 0, 1.0, 0.0).astype(BF16)
        for h in range(FOX_HEADS):
            qt = qt_all[h * FOX_HEAD_DIM:(h + 1) * FOX_HEAD_DIM]
            vt = vt_all[h * FOX_HEAD_DIM:(h + 1) * FOX_HEAD_DIM]
            for j in range(tm // blk):
                qt_ref[0, j, h * LANES:h * LANES + FOX_HEAD_DIM, :] = qt[:, j * blk:(j + 1) * blk]
                qt_ref[0, j, h * LANES + FOX_HEAD_DIM:(h + 1) * LANES, :] = q_tail
                vt_ref[0, j, h * LANES:h * LANES + FOX_HEAD_DIM, :] = vt[:, j * blk:(j + 1) * blk]
                vt_ref[0, j, h * LANES + FOX_HEAD_DIM:(h + 1) * LANES, :] = v_tail
    else:
        q_ref, lfrep_ref = extra
        q_ref[0] = q
        lfrep_ref[0] = logf

    uext_ref[HIST_ROWS:HIST_ROWS + tm, :] = u
    pos = pos0 + t * tm + _iota((tm, 1), 0)
    for gi, w in enumerate(POOL_WINDOWS):
        cols = slice(gi * POOL_GROUP_DIM, (gi + 1) * POOL_GROUP_DIM)
        sw = uext_ref[:, cols]
        shift = 1
        while shift < w:
            sw = sw + pltpu.roll(sw, shift, axis=0)
            shift *= 2
        cnt = jnp.minimum(w, pos + 1).astype(F32)
        pooled = sw[HIST_ROWS:, :] / cnt - u[:, cols]
        mixed = _dot(pooled.astype(BF16), wpool_ref[gi]) * spool_ref[:, cols]
        opool_ref[0, :, cols] = mixed.astype(BF16)

    @pl.when(t == last)
    def _():
        pnew_ref[0] = uext_ref[tm + 1:tm + HIST_ROWS, :]

    uext_ref[0:HIST_ROWS, :] = uext_ref[tm:tm + HIST_ROWS, :]

    for s, cols in zip(mem_s, mem_cols):
        e = jnp.exp(s - jnp.max(s, axis=-1, keepdims=True))
        o = _dot(e.astype(BF16), mv_ref[0, :, cols]) / jnp.sum(e, axis=-1, keepdims=True)
        omem_ref[0, :, cols] = o.astype(BF16)


def _in_proj(x, g, w_rep, wt, bf_rep, hist, mkb, mvb, wpool, spool, *, tm, pos0, augment, blk=None):
    B, S, D = x.shape
    M = mkb.shape[1]
    nt = S // tm
    tile = lambda w: pl.BlockSpec((1, tm, w), lambda b, t: (b, t, 0))
    per_b = lambda r, w: pl.BlockSpec((1, r, w), lambda b, t: (b, 0, 0))
    const = lambda shape: pl.BlockSpec(shape, lambda b, t: (0,) * len(shape))
    if augment:
        W = FOX_HEADS * LANES
        t_tile = pl.BlockSpec((1, tm // blk, W, blk), lambda b, t: (b, t, 0, 0))
        abc_specs = [t_tile, tile(W), t_tile]
        abc_shapes = [jax.ShapeDtypeStruct((B, S // blk, W, blk), BF16),
                      jax.ShapeDtypeStruct((B, S, W), BF16),
                      jax.ShapeDtypeStruct((B, S // blk, W, blk), BF16)]
    else:
        abc_specs = [tile(FOX_WIDTH), tile(LANES)]
        abc_shapes = [jax.ShapeDtypeStruct((B, S, FOX_WIDTH), F32),
                      jax.ShapeDtypeStruct((B, S, LANES), F32)]
    return pl.pallas_call(
        functools.partial(_in_proj_kernel, tm=tm, pos0=pos0, augment=augment, blk=blk),
        grid=(B, nt),
        in_specs=[tile(D), const((1, D)), const((D, _IN_COLS)), const(wt.shape), const((1, LANES)),
                  per_b(HIST_ROWS, POOL_WIDTH), per_b(M, MEM_WIDTH), per_b(M, MEM_WIDTH),
                  const((POOL_GROUPS, POOL_GROUP_DIM, POOL_GROUP_DIM)), const((1, POOL_WIDTH))],
        out_specs=[tile(FOX_WIDTH), tile(FOX_WIDTH), tile(FOX_HEADS), per_b(POOL_HIST, POOL_WIDTH),
                   tile(POOL_WIDTH), tile(MEM_WIDTH)] + abc_specs,
        out_shape=[jax.ShapeDtypeStruct((B, S, FOX_WIDTH), F32),
                   jax.ShapeDtypeStruct((B, S, FOX_WIDTH), F32),
                   jax.ShapeDtypeStruct((B, S, FOX_HEADS), F32),
                   jax.ShapeDtypeStruct((B, POOL_HIST, POOL_WIDTH), F32),
                   jax.ShapeDtypeStruct((B, S, POOL_WIDTH), BF16),
                   jax.ShapeDtypeStruct((B, S, MEM_WIDTH), BF16)] + abc_shapes,
        scratch_shapes=[pltpu.VMEM((HIST_ROWS + tm, POOL_WIDTH), F32), pltpu.VMEM((1, LANES), F32)],
        compiler_params=pltpu.CompilerParams(dimension_semantics=("parallel", "arbitrary"),
                                             vmem_limit_bytes=VMEM_LIMIT),
        name="in_proj",
    )(x, g, w_rep, wt, bf_rep, hist, mkb, mvb, wpool, spool)


def _fox_prompt_kernel(qt_ref, k_ref, vt_ref, o_ref, m_ref, acc_ref, *, blk):
    i = pl.program_id(1)
    m_ref[...] = jnp.full_like(m_ref, -jnp.inf)
    acc_ref[...] = jnp.zeros_like(acc_ref)
    slabs = [slice(h * LANES, (h + 1) * LANES) for h in range(FOX_HEADS)]

    def scores(j, h):
        start = pl.multiple_of(j * blk, blk)
        return _dot(k_ref[0, pl.ds(start, blk), slabs[h]], qt_ref[0, 0, slabs[h], :])

    def accumulate(j, h, st, masked):
        if masked:
            st = jnp.where(_iota((blk, blk), 0) <= _iota((blk, blk), 1), st, -jnp.inf)
        m_prev = m_ref[h]
        m_new = jnp.maximum(m_prev, jnp.max(st, axis=0, keepdims=True))
        p = jnp.exp2(st - m_new).astype(BF16)
        acc_ref[h] = jnp.exp2(m_prev - m_new) * acc_ref[h] + _dot(vt_ref[0, j, slabs[h], :], p)
        m_ref[h] = m_new

    def run(blocks):
        work = [(j, h, masked) for j, masked in blocks for h in range(FOX_HEADS)]
        pending = [scores(j, h) for j, h, _ in work[:SCORES_AHEAD]]
        for n, (j, h, masked) in enumerate(work):
            if n + SCORES_AHEAD < len(work):
                pending.append(scores(*work[n + SCORES_AHEAD][:2]))
            accumulate(j, h, pending.pop(0), masked)

    def body(jj, carry):
        run([(2 * jj, False), (2 * jj + 1, False)])
        return carry

    lax.fori_loop(0, i >> 1, body, 0)

    @pl.when((i & 1) == 1)
    def _():
        run([(i - 1, False), (i, True)])

    @pl.when((i & 1) == 0)
    def _():
        run([(i, True)])

    for pair in range(FOX_HEADS // 2):
        halves = []
        for par in (0, 1):
            acc = acc_ref[2 * pair + par]
            halves.append(acc[:FOX_HEAD_DIM] / acc[FOX_HEAD_DIM:FOX_HEAD_DIM + 1])
        o_ref[0, :, pair * LANES:(pair + 1) * LANES] = jnp.concatenate(halves, axis=0).T.astype(BF16)


def _fox_prompt(qt, ka, vt, *, blk):
    B, S, W = ka.shape
    nblk = S // blk
    return pl.pallas_call(
        functools.partial(_fox_prompt_kernel, blk=blk),
        grid=(B, nblk),
        in_specs=[pl.BlockSpec((1, 1, W, blk), lambda b, i: (b, i, 0, 0)),
                  pl.BlockSpec((1, S, W), lambda b, i: (b, 0, 0)),
                  pl.BlockSpec((1, nblk, W, blk), lambda b, i: (b, 0, 0, 0))],
        out_specs=pl.BlockSpec((1, blk, FOX_WIDTH), lambda b, i: (b, i, 0)),
        out_shape=jax.ShapeDtypeStruct((B, S, FOX_WIDTH), BF16),
        scratch_shapes=[pltpu.VMEM((FOX_HEADS, 1, blk), F32), pltpu.VMEM((FOX_HEADS, LANES, blk), F32)],
        compiler_params=pltpu.CompilerParams(
            dimension_semantics=("parallel", "arbitrary"), vmem_limit_bytes=VMEM_LIMIT),
        name="fox_prompt",
    )(qt, ka, vt)


def _fox_sample_kernel(q_ref, kn_ref, vn_ref, lfn_ref, ck_ref, cv_ref, clf_ref, o_ref, *, chunk):
    T = q_ref.shape[1]
    P = ck_ref.shape[1]
    HT = FOX_HEADS * T
    t_shift = T.bit_length() - 1
    assert HT == LANES and 1 << t_shift == T

    q_rep = jnp.concatenate([q_ref[0]] * FOX_HEADS, axis=0)
    row_head = _iota((HT, FOX_WIDTH), 0) >> t_shift
    q_bd = jnp.where(row_head == (_iota((HT, FOX_WIDTH), 1) >> 6), q_rep, 0.0).astype(BF16)
    e_lane = _iota((HT, LANES), 1)
    e_bd = jnp.where((e_lane < BIAS_LANES) & ((e_lane & 7) == (_iota((HT, LANES), 0) >> t_shift)),
                     1.0, 0.0).astype(BF16)

    def scores(keys, c):
        return _dot_nt(keys.astype(BF16), q_bd) + _dot_nt(_bias_pieces(c).astype(BF16), e_bd)

    carry = jnp.zeros((1, LANES), F32)
    s_parts = []
    for ci in range(P // chunk):
        rows = slice(ci * chunk, (ci + 1) * chunk)
        c, carry = _cumsum_rows(clf_ref[0, rows, :], carry)
        s_parts.append(scores(ck_ref[0, rows, :], c))
    pad_rows = lambda a: jnp.concatenate([a, jnp.zeros((LANES - T, a.shape[1]), F32)], axis=0)
    c_new, _ = _cumsum_rows(pad_rows(lfn_ref[0]), carry)
    s_new = scores(pad_rows(kn_ref[0]), c_new)
    s_new = jnp.where(_iota((LANES, HT), 0) <= (_iota((LANES, HT), 1) & (T - 1)), s_new, -jnp.inf)

    m = jnp.max(s_new, axis=0, keepdims=True)
    for s in s_parts:
        m = jnp.maximum(m, jnp.max(s, axis=0, keepdims=True))
    ones = jnp.ones((chunk, LANES), BF16)
    p_new = jnp.exp2(s_new - m).T.astype(BF16)
    r = _dot(p_new, pad_rows(vn_ref[0]).astype(BF16))
    l = _dot(p_new, ones[:LANES])
    for ci, s in enumerate(s_parts):
        p = jnp.exp2(s - m).T.astype(BF16)
        r = r + _dot(p, cv_ref[0, ci * chunk:(ci + 1) * chunk, :].astype(BF16))
        l = l + _dot(p, ones)
    r = r / jnp.concatenate([l] * (FOX_WIDTH // LANES), axis=1)
    col_head = _iota((T, FOX_WIDTH), 1) >> 6
    o = jnp.zeros((T, FOX_WIDTH), F32)
    for h in range(FOX_HEADS):
        o = jnp.where(col_head == h, r[h * T:(h + 1) * T, :], o)
    o_ref[0] = o.astype(BF16)


def _fox_sample(q, k_new, v_new, lf_new, cache_k, cache_v, cache_lf_rep, *, chunk):
    B, T, _ = q.shape
    P = cache_k.shape[1]
    new = lambda w: pl.BlockSpec((1, T, w), lambda b: (b, 0, 0))
    old = lambda w: pl.BlockSpec((1, P, w), lambda b: (b, 0, 0))
    return pl.pallas_call(
        functools.partial(_fox_sample_kernel, chunk=chunk),
        grid=(B,),
        in_specs=[new(FOX_WIDTH), new(FOX_WIDTH), new(FOX_WIDTH), new(LANES),
                  old(FOX_WIDTH), old(FOX_WIDTH), old(LANES)],
        out_specs=new(FOX_WIDTH),
        out_shape=jax.ShapeDtypeStruct((B, T, FOX_WIDTH), BF16),
        compiler_params=pltpu.CompilerParams(dimension_semantics=("parallel",),
                                             vmem_limit_bytes=VMEM_LIMIT),
        name="fox_sample",
    )(q, k_new, v_new, lf_new, cache_k, cache_v, cache_lf_rep)


def _merge_kernel(x_ref, ofox_ref, opool_ref, omem_ref, g_ref, wg_ref, wbf_ref, wbp_ref, wbm_ref,
                  wout_ref, h_ref):
    D = x_ref.shape[-1]
    x = x_ref[0]
    xn = _rms(x, g_ref[...]).astype(BF16)
    m = None
    for bi, (o_ref, w_ref) in enumerate(((ofox_ref, wbf_ref), (opool_ref, wbp_ref), (omem_ref, wbm_ref))):
        gate = jax.nn.sigmoid(_dot(xn, wg_ref[:, bi * D:(bi + 1) * D]))
        term = gate * _dot(o_ref[0], w_ref[...])
        m = term if m is None else m + term
    h_ref[0] = x + _dot(m.astype(BF16), wout_ref[...])


def _merge(x, ofox, opool, omem, g, wg, wbf, wbp, wbm, wout, *, tm):
    B, S, D = x.shape
    tile = lambda w: pl.BlockSpec((1, tm, w), lambda b, t: (b, t, 0))
    const = lambda a: pl.BlockSpec(a.shape, lambda b, t: (0,) * a.ndim)
    return pl.pallas_call(
        _merge_kernel,
        grid=(B, S // tm),
        in_specs=[tile(D), tile(FOX_WIDTH), tile(POOL_WIDTH), tile(MEM_WIDTH),
                  const(g), const(wg), const(wbf), const(wbp), const(wbm), const(wout)],
        out_specs=tile(D),
        out_shape=jax.ShapeDtypeStruct((B, S, D), F32),
        compiler_params=pltpu.CompilerParams(dimension_semantics=("parallel", "parallel"),
                                             vmem_limit_bytes=VMEM_LIMIT),
        name="merge",
    )(x, ofox, opool, omem, g, wg, wbf, wbp, wbm, wout)


def _ffn_kernel(h_ref, g_ref, wup_ref, wdown_ref, gfin_ref, y_ref, *, ff_chunk, final_norm):
    h = h_ref[0]
    hn = _rms(h, g_ref[...]).astype(BF16)
    acc = h
    for c in range(wup_ref.shape[1] // ff_chunk):
        cols = slice(c * ff_chunk, (c + 1) * ff_chunk)
        a = jnp.maximum(_dot(hn, wup_ref[:, cols]), 0.0)
        acc = acc + _dot((a * a).astype(BF16), wdown_ref[cols, :])
    y_ref[0] = _rms(acc, gfin_ref[...]) if final_norm else acc


def _ffn(h, g, wup, wdown, gfin, *, tm, ff_chunk, final_norm):
    B, S, D = h.shape
    tile = pl.BlockSpec((1, tm, D), lambda b, t: (b, t, 0))
    const = lambda a: pl.BlockSpec(a.shape, lambda b, t: (0,) * a.ndim)
    return pl.pallas_call(
        functools.partial(_ffn_kernel, ff_chunk=ff_chunk, final_norm=final_norm),
        grid=(B, S // tm),
        in_specs=[tile, const(g), const(wup), const(wdown), const(gfin)],
        out_specs=tile,
        out_shape=jax.ShapeDtypeStruct((B, S, D), F32),
        compiler_params=pltpu.CompilerParams(dimension_semantics=("parallel", "parallel"),
                                             vmem_limit_bytes=VMEM_LIMIT),
        name="ffn",
    )(h, g, wup, wdown, gfin)


def _tile_rows(n, want):
    return want if n % want == 0 else n


def _repack_w_in(w_in):
    o_f = 3 * FOX_WIDTH
    o_u = o_f + FOX_HEADS
    o_g = o_u + POOL_WIDTH + MEM_WIDTH
    wf = w_in[:, o_f:o_u]
    wf_rep = jnp.concatenate([wf] * BIAS_PIECES
                             + [jnp.zeros((w_in.shape[0], LANES - BIAS_LANES), w_in.dtype)], axis=1)
    w_rep = jnp.concatenate([w_in[:, FOX_WIDTH:o_f], w_in[:, o_u:o_g], wf_rep], axis=1).astype(BF16)
    wt = jnp.concatenate([w_in[:, :FOX_WIDTH], w_in[:, 2 * FOX_WIDTH:o_f]], axis=1).astype(BF16).T
    return w_rep, wt, w_in[:, o_g:].astype(BF16)


def _rep_lanes(a):
    pad = jnp.zeros(a.shape[:-1] + (LANES - BIAS_LANES,), a.dtype)
    return jnp.concatenate([a] * BIAS_PIECES + [pad], axis=-1)


def kernel(x_prompt, x_sample, mem_prompt, cache_fox_k, cache_fox_v, cache_fox_logf, cache_pool,
           cache_mem_k, cache_mem_v, g_attn, w_in, b_f, w_pool, s_pool, g_mem, w_mkv,
           w_br_fox, w_br_pool, w_br_mem, w_out, g_mlp, w_up, w_down, g_final):
    depth = w_in.shape[0]
    B, S, D = x_prompt.shape
    BS, T, _ = x_sample.shape
    P = cache_fox_k.shape[2]
    M = mem_prompt.shape[1]
    tm_p = _tile_rows(S, 512)
    blk_p = _tile_rows(tm_p, 256)
    tm_s = _tile_rows(BS * T, 256)
    chunk_s = _tile_rows(P, 256)
    gfin = g_final.reshape(1, D)

    xp, xs = x_prompt, x_sample
    outs_p = [[] for _ in range(6)]
    outs_s = [[] for _ in range(4)]
    for l in range(depth):
        w_rep, wt, wg = _repack_w_in(w_in[l])
        bf_rep = _rep_lanes(b_f[l]).reshape(1, LANES)
        g_a, g_m, g_f = g_attn[l].reshape(1, D), g_mem[l].reshape(1, D), g_mlp[l].reshape(1, D)
        wpool, spool = w_pool[l].astype(BF16), s_pool[l].reshape(1, POOL_WIDTH)
        wbf, wbp, wbm = (w.astype(BF16) for w in (w_br_fox[l], w_br_pool[l], w_br_mem[l]))
        wout, wup, wdown = (w.astype(BF16) for w in (w_out[l], w_up[l], w_down[l]))
        final = l == depth - 1

        mk, mv, mkb, mvb = _mem_kv(mem_prompt, g_m, w_mkv[l].astype(BF16))
        hist0 = jnp.zeros((B, HIST_ROWS, POOL_WIDTH), F32)
        k, v, logf, pnew, opool, omem, qt, ka, vt = _in_proj(
            xp, g_a, w_rep, wt, bf_rep, hist0, mkb, mvb, wpool, spool, tm=tm_p, pos0=0, augment=True,
            blk=blk_p)
        ofox = _fox_prompt(qt, ka, vt, blk=blk_p)
        h = _merge(xp, ofox, opool, omem, g_a, wg, wbf, wbp, wbm, wout, tm=tm_p)
        xp = _ffn(h, g_f, wup, wdown, gfin, tm=tm_p, ff_chunk=1024, final_norm=final)
        for acc, val in zip(outs_p, (k.reshape(B, S, FOX_HEADS, FOX_HEAD_DIM),
                                     v.reshape(B, S, FOX_HEADS, FOX_HEAD_DIM), logf, pnew,
                                     mk.reshape(B, M, MEM_HEADS, MEM_HEAD_DIM),
                                     mv.reshape(B, M, MEM_HEADS, MEM_HEAD_DIM))):
            acc.append(val)

        hist = jnp.pad(cache_pool[l], ((0, 0), (HIST_ROWS - POOL_HIST, 0), (0, 0)))
        cmk = cache_mem_k[l].reshape(BS, M, MEM_WIDTH).astype(BF16)
        cmv = cache_mem_v[l].reshape(BS, M, MEM_WIDTH).astype(BF16)
        k, v, logf, pnew, opool, omem, q, lf_rep = _in_proj(
            xs, g_a, w_rep, wt, bf_rep, hist, cmk, cmv, wpool, spool, tm=T, pos0=P, augment=False)
        ofox = _fox_sample(q, k, v, lf_rep, cache_fox_k[l].reshape(BS, P, FOX_WIDTH),
                           cache_fox_v[l].reshape(BS, P, FOX_WIDTH), _rep_lanes(cache_fox_logf[l]),
                           chunk=chunk_s)
        flat = lambda a: a.reshape(1, BS * T, a.shape[-1])
        h = _merge(flat(xs), flat(ofox), flat(opool), flat(omem), g_a, wg, wbf, wbp, wbm, wout, tm=tm_s)
        xs = _ffn(h, g_f, wup, wdown, gfin, tm=tm_s, ff_chunk=1024, final_norm=final).reshape(BS, T, D)
        for acc, val in zip(outs_s, (k.reshape(BS, T, FOX_HEADS, FOX_HEAD_DIM),
                                     v.reshape(BS, T, FOX_HEADS, FOX_HEAD_DIM), logf, pnew)):
            acc.append(val)

    stack = lambda o: o[0][None] if depth == 1 else jnp.stack(o)
    return (xp, xs, *(stack(o) for o in outs_p), *(stack(o) for o in outs_s))
```

```python
import functools
import math

import jax
import jax.numpy as jnp
from jax import lax
from jax.experimental import pallas as pl
from jax.experimental.pallas import tpu as pltpu

F32 = jnp.float32
BF16 = jnp.bfloat16

EPS = 1e-6
LANES = 128
FOX_HEADS = 8
FOX_HEAD_DIM = 64
FOX_WIDTH = FOX_HEADS * FOX_HEAD_DIM
POOL_WINDOWS = (2, 4, 8, 16)
POOL_GROUPS = 4
POOL_GROUP_DIM = 128
POOL_WIDTH = POOL_GROUPS * POOL_GROUP_DIM
POOL_HIST = 15
HIST_ROWS = 16
MEM_HEADS = 4
MEM_HEAD_DIM = 128
MEM_WIDTH = MEM_HEADS * MEM_HEAD_DIM
N_BRANCH = 3
LOG2E = math.log2(math.e)
BIAS_PIECES = 3
BIAS_LANES = BIAS_PIECES * FOX_HEADS
V_SLAB = 80
SCORES_AHEAD = 5
CUMSUM_CHUNK = 128
VMEM_LIMIT = 56 * 1024 * 1024


def _dot(a, b):
    return jnp.dot(a, b, preferred_element_type=F32)


def _dot_nt(a, b):
    return lax.dot_general(a, b, (((1,), (1,)), ((), ())), preferred_element_type=F32)


def _rms(x, g):
    return x * lax.rsqrt(jnp.mean(x * x, axis=-1, keepdims=True) + EPS) * g


def _iota(shape, dim):
    return lax.broadcasted_iota(jnp.int32, shape, dim)


def _split3(x):
    p0 = x.astype(BF16)
    r1 = x - p0.astype(F32)
    p1 = r1.astype(BF16)
    p2 = (r1 - p1.astype(F32)).astype(BF16)
    return p0, p1, p2


def _cumsum_rows(x, carry):
    n = x.shape[0]
    ch = min(n, CUMSUM_CHUNK)
    tril = (_iota((ch, ch), 1) <= _iota((ch, ch), 0)).astype(BF16)
    outs = []
    for c0 in range(0, n, ch):
        r = _dot(tril, jnp.concatenate(_split3(x[c0:c0 + ch]), axis=1))
        c = r[:, :LANES] + r[:, LANES:2 * LANES] + r[:, 2 * LANES:] + carry
        carry = c[ch - 1:ch]
        outs.append(c)
    return (outs[0] if len(outs) == 1 else jnp.concatenate(outs, axis=0)), carry


def _bias_pieces(c):
    b = c * (-LOG2E)
    p0 = b.astype(BF16).astype(F32)
    r1 = b - p0
    p1 = r1.astype(BF16).astype(F32)
    p2 = (r1 - p1).astype(BF16).astype(F32)
    grp = _iota(c.shape, 1) >> 3
    return jnp.where(grp == 0, p0, jnp.where(grp == 1, p1, jnp.where(grp == 2, p2, 0.0)))


def _mem_kv_kernel(mem_ref, g_ref, w_ref, mk_ref, mv_ref):
    M = mem_ref.shape[1]
    xn = _rms(mem_ref[0], g_ref[...]).astype(BF16)
    for out_ref, off in ((mk_ref, 0), (mv_ref, MEM_WIDTH)):
        kv = _dot(xn, w_ref[:, off:off + MEM_WIDTH])
        for h in range(MEM_HEADS):
            out_ref[0, pl.ds(h, M, stride=MEM_HEADS), :] = kv[:, h * MEM_HEAD_DIM:(h + 1) * MEM_HEAD_DIM]


def _mem_kv(mem, g, w_mkv):
    B, M, D = mem.shape
    return pl.pallas_call(
        _mem_kv_kernel,
        grid=(B,),
        in_specs=[pl.BlockSpec((1, M, D), lambda b: (b, 0, 0)), pl.BlockSpec((1, D), lambda b: (0, 0)),
                  pl.BlockSpec((D, 2 * MEM_WIDTH), lambda b: (0, 0))],
        out_specs=[pl.BlockSpec((1, M * MEM_HEADS, MEM_HEAD_DIM), lambda b: (b, 0, 0))] * 2,
        out_shape=[jax.ShapeDtypeStruct((B, M * MEM_HEADS, MEM_HEAD_DIM), F32)] * 2,
        compiler_params=pltpu.CompilerParams(dimension_semantics=("parallel",),
                                             vmem_limit_bytes=VMEM_LIMIT),
        name="mem_kv",
    )(mem, g, w_mkv)


def _in_proj_kernel(x_ref, g_ref, wk_ref, wv_ref, wu_ref, wqm_ref, wf_ref, wqt_ref, wvt_ref, bf_ref, hist_ref,
                    mk_ref, mv_ref, wpool_ref, spool_ref,
                    k_ref, v_ref, logf_ref, pnew_ref, opool_ref, omem_ref, *rest, tm, pos0, augment, blk):
    extra, (uext_ref, carry_ref) = rest[:-2], rest[-2:]
    t = pl.program_id(1)
    last = pl.num_programs(1) - 1
    q_scale = FOX_HEAD_DIM ** -0.5 * LOG2E

    @pl.when(t == 0)
    def _():
        uext_ref[0:HIST_ROWS, :] = hist_ref[0]
        carry_ref[...] = jnp.zeros_like(carry_ref)

    xn = _rms(x_ref[0], g_ref[...]).astype(BF16)
    lane = _iota((tm, LANES), 1)
    mem_cols = [slice(h * MEM_HEAD_DIM, (h + 1) * MEM_HEAD_DIM) for h in range(MEM_HEADS)]

    zf = _dot(xn, wf_ref[...]) + bf_ref[...]
    k = _dot(xn, wk_ref[...])
    v = _dot(xn, wv_ref[...])
    u = _dot(xn, wu_ref[...])
    qm = _dot(xn, wqm_ref[...]).astype(BF16)
    if augment:
        qt_all = (_dot_nt(wqt_ref[...], xn) * q_scale).astype(BF16)
        vt_all = _dot_nt(wvt_ref[...], xn).astype(BF16)
    else:
        q = _dot_nt(xn, wqt_ref[...]) * q_scale
    n_mem = mk_ref.shape[1] // MEM_HEADS
    mem_head = lambda ref, h: ref[0, pl.ds(h, n_mem, stride=MEM_HEADS), :].astype(BF16)
    mem_s = [_dot_nt(qm[:, cols], mem_head(mk_ref, h)) * MEM_HEAD_DIM ** -0.5
             for h, cols in enumerate(mem_cols)]

    k_ref[0] = k
    v_ref[0] = v
    logf = jnp.minimum(zf, 0.0) - jnp.log1p(jnp.exp(-jnp.abs(zf)))
    logf_ref[0] = logf[:, :FOX_HEADS]

    if augment:
        qt_ref, ka_ref, vt_ref = extra
        c, carry_ref[...] = _cumsum_rows(logf, carry_ref[...])
        bias = pltpu.roll(_bias_pieces(c), FOX_HEAD_DIM, axis=1)
        in_bias = (lane >= FOX_HEAD_DIM) & (lane < FOX_HEAD_DIM + BIAS_LANES)
        for pair in range(FOX_HEADS // 2):
            even = k[:, pair * LANES:(pair + 1) * LANES]
            odd = pltpu.roll(even, FOX_HEAD_DIM, axis=1)
            for par, data in ((0, even), (1, odd)):
                h = 2 * pair + par
                tail = jnp.where(in_bias & ((lane & 7) == h), bias, 0.0)
                ka_ref[0, :, h * LANES:(h + 1) * LANES] = jnp.where(lane < FOX_HEAD_DIM, data, tail).astype(BF16)
        q_tail = jnp.where(_iota((LANES - FOX_HEAD_DIM, blk), 0) < BIAS_LANES, 1.0, 0.0).astype(BF16)
        v_tail = jnp.where(_iota((V_SLAB - FOX_HEAD_DIM, blk), 0) == 0, 1.0, 0.0).astype(BF16)
        for h in range(FOX_HEADS):
            qt = qt_all[h * FOX_HEAD_DIM:(h + 1) * FOX_HEAD_DIM]
            vt = vt_all[h * FOX_HEAD_DIM:(h + 1) * FOX_HEAD_DIM]
            for j in range(tm // blk):
                qt_ref[0, j, h * LANES:h * LANES + FOX_HEAD_DIM, :] = qt[:, j * blk:(j + 1) * blk]
                qt_ref[0, j, h * LANES + FOX_HEAD_DIM:(h + 1) * LANES, :] = q_tail
                vt_ref[0, j, h * V_SLAB:h * V_SLAB + FOX_HEAD_DIM, :] = vt[:, j * blk:(j + 1) * blk]
                vt_ref[0, j, h * V_SLAB + FOX_HEAD_DIM:(h + 1) * V_SLAB, :] = v_tail
    else:
        extra[0][0] = q

    uext_ref[HIST_ROWS:HIST_ROWS + tm, :] = u
    pos = pos0 + t * tm + _iota((tm, 1), 0)
    for gi, w in enumerate(POOL_WINDOWS):
        cols = slice(gi * POOL_GROUP_DIM, (gi + 1) * POOL_GROUP_DIM)
        sw = uext_ref[:, cols]
        shift = 1
        while shift < w:
            sw = sw + pltpu.roll(sw, shift, axis=0)
            shift *= 2
        cnt = jnp.minimum(w, pos + 1).astype(F32)
        pooled = sw[HIST_ROWS:, :] / cnt - u[:, cols]
        mixed = _dot(pooled.astype(BF16), wpool_ref[gi]) * spool_ref[:, cols]
        opool_ref[0, :, cols] = mixed.astype(BF16)

    @pl.when(t == last)
    def _():
        pnew_ref[0] = uext_ref[tm + 1:tm + HIST_ROWS, :]

    uext_ref[0:HIST_ROWS, :] = uext_ref[tm:tm + HIST_ROWS, :]

    for h, (s, cols) in enumerate(zip(mem_s, mem_cols)):
        e = jnp.exp(s - jnp.max(s, axis=-1, keepdims=True))
        o = _dot(e.astype(BF16), mem_head(mv_ref, h)) / jnp.sum(e, axis=-1, keepdims=True)
        omem_ref[0, :, cols] = o.astype(BF16)


def _in_proj(x, g, weights, bf_rep, hist, mk, mv, wpool, spool, *, tm, pos0, augment, blk=None):
    B, S, D = x.shape
    nt = S // tm
    tile = lambda w: pl.BlockSpec((1, tm, w), lambda b, t: (b, t, 0))
    per_b = lambda r, w: pl.BlockSpec((1, r, w), lambda b, t: (b, 0, 0))
    const = lambda shape: pl.BlockSpec(shape, lambda b, t: (0,) * len(shape))
    if augment:
        W = FOX_HEADS * LANES
        t_tile = lambda rows: pl.BlockSpec((1, tm // blk, rows, blk), lambda b, t: (b, t, 0, 0))
        abc_specs = [t_tile(W), tile(W), t_tile(FOX_HEADS * V_SLAB)]
        abc_shapes = [jax.ShapeDtypeStruct((B, S // blk, W, blk), BF16),
                      jax.ShapeDtypeStruct((B, S, W), BF16),
                      jax.ShapeDtypeStruct((B, S // blk, FOX_HEADS * V_SLAB, blk), BF16)]
    else:
        abc_specs = [tile(FOX_WIDTH)]
        abc_shapes = [jax.ShapeDtypeStruct((B, S, FOX_WIDTH), F32)]
    return pl.pallas_call(
        functools.partial(_in_proj_kernel, tm=tm, pos0=pos0, augment=augment, blk=blk),
        grid=(B, nt),
        in_specs=[tile(D), const((1, D))] + [const(w.shape) for w in weights] + [
            const((1, LANES)), per_b(HIST_ROWS, POOL_WIDTH), per_b(*mk.shape[1:]), per_b(*mv.shape[1:]),
            const((POOL_GROUPS, POOL_GROUP_DIM, POOL_GROUP_DIM)), const((1, POOL_WIDTH))],
        out_specs=[tile(FOX_WIDTH), tile(FOX_WIDTH), tile(FOX_HEADS), per_b(POOL_HIST, POOL_WIDTH),
                   tile(POOL_WIDTH), tile(MEM_WIDTH)] + abc_specs,
        out_shape=[jax.ShapeDtypeStruct((B, S, FOX_WIDTH), F32),
                   jax.ShapeDtypeStruct((B, S, FOX_WIDTH), F32),
                   jax.ShapeDtypeStruct((B, S, FOX_HEADS), F32),
                   jax.ShapeDtypeStruct((B, POOL_HIST, POOL_WIDTH), F32),
                   jax.ShapeDtypeStruct((B, S, POOL_WIDTH), BF16),
                   jax.ShapeDtypeStruct((B, S, MEM_WIDTH), BF16)] + abc_shapes,
        scratch_shapes=[pltpu.VMEM((HIST_ROWS + tm, POOL_WIDTH), F32), pltpu.VMEM((1, LANES), F32)],
        compiler_params=pltpu.CompilerParams(dimension_semantics=("parallel", "arbitrary"),
                                             vmem_limit_bytes=VMEM_LIMIT),
        name="in_proj",
    )(x, g, *weights, bf_rep, hist, mk, mv, wpool, spool)


def _fox_prompt_kernel(qt_ref, k_ref, vt_ref, o_ref, m_ref, acc_ref, *, blk):
    i = pl.program_id(1)
    m_ref[...] = jnp.full_like(m_ref, -jnp.inf)
    acc_ref[...] = jnp.zeros_like(acc_ref)
    slabs = [slice(h * LANES, (h + 1) * LANES) for h in range(FOX_HEADS)]

    def scores(j, h):
        start = pl.multiple_of(j * blk, blk)
        return _dot(k_ref[0, pl.ds(start, blk), slabs[h]], qt_ref[0, 0, slabs[h], :])

    def accumulate(j, h, st, masked):
        if masked:
            st = jnp.where(_iota((blk, blk), 0) <= _iota((blk, blk), 1), st, -jnp.inf)
        m_prev = m_ref[h]
        m_new = jnp.maximum(m_prev, jnp.max(st, axis=0, keepdims=True))
        p = jnp.exp2(st - m_new).astype(BF16)
        pv = _dot(vt_ref[0, j, h * V_SLAB:(h + 1) * V_SLAB, :], p)
        acc_ref[h] = jnp.exp2(m_prev - m_new) * acc_ref[h] + pv
        m_ref[h] = m_new

    def run(blocks):
        work = [(j, h, masked) for j, masked in blocks for h in range(FOX_HEADS)]
        pending = [scores(j, h) for j, h, _ in work[:SCORES_AHEAD]]
        for n, (j, h, masked) in enumerate(work):
            if n + SCORES_AHEAD < len(work):
                pending.append(scores(*work[n + SCORES_AHEAD][:2]))
            accumulate(j, h, pending.pop(0), masked)

    def body(jj, carry):
        run([(2 * jj, False), (2 * jj + 1, False)])
        return carry

    lax.fori_loop(0, i >> 1, body, 0)

    @pl.when((i & 1) == 1)
    def _():
        run([(i - 1, False), (i, True)])

    @pl.when((i & 1) == 0)
    def _():
        run([(i, True)])

    for pair in range(FOX_HEADS // 2):
        halves = []
        for par in (0, 1):
            acc = acc_ref[2 * pair + par]
            halves.append(acc[:FOX_HEAD_DIM] / acc[FOX_HEAD_DIM:FOX_HEAD_DIM + 1])
        o_ref[0, :, pair * LANES:(pair + 1) * LANES] = jnp.concatenate(halves, axis=0).T.astype(BF16)


def _fox_prompt(qt, ka, vt, *, blk):
    B, S, W = ka.shape
    nblk = S // blk
    return pl.pallas_call(
        functools.partial(_fox_prompt_kernel, blk=blk),
        grid=(B, nblk),
        in_specs=[pl.BlockSpec((1, 1, W, blk), lambda b, i: (b, i, 0, 0)),
                  pl.BlockSpec((1, S, W), lambda b, i: (b, 0, 0)),
                  pl.BlockSpec((1, nblk, FOX_HEADS * V_SLAB, blk), lambda b, i: (b, 0, 0, 0))],
        out_specs=pl.BlockSpec((1, blk, FOX_WIDTH), lambda b, i: (b, i, 0)),
        out_shape=jax.ShapeDtypeStruct((B, S, FOX_WIDTH), BF16),
        scratch_shapes=[pltpu.VMEM((FOX_HEADS, 1, blk), F32), pltpu.VMEM((FOX_HEADS, V_SLAB, blk), F32)],
        compiler_params=pltpu.CompilerParams(
            dimension_semantics=("parallel", "arbitrary"), vmem_limit_bytes=VMEM_LIMIT),
        name="fox_prompt",
    )(qt, ka, vt)


def _fox_sample_kernel(q_ref, kn_ref, vn_ref, lfn_ref, ck_ref, cv_ref, clf_ref, o_ref, *, chunk):
    T = q_ref.shape[1]
    P = clf_ref.shape[1]
    HT = FOX_HEADS * T
    t_shift = T.bit_length() - 1
    assert HT == LANES and 1 << t_shift == T

    q_rep = jnp.concatenate([q_ref[0]] * FOX_HEADS, axis=0)
    row_head = _iota((HT, FOX_WIDTH), 0) >> t_shift
    q_bd = jnp.where(row_head == (_iota((HT, FOX_WIDTH), 1) >> 6), q_rep, 0.0).astype(BF16)
    e_lane = _iota((HT, LANES), 1)
    e_bd = jnp.where((e_lane < BIAS_LANES) & ((e_lane & 7) == (_iota((HT, LANES), 0) >> t_shift)),
                     1.0, 0.0).astype(BF16)

    def scores(keys, c):
        c = c + pltpu.roll(c, FOX_HEADS, axis=1) + pltpu.roll(c, 2 * FOX_HEADS, axis=1)
        return _dot_nt(keys.astype(BF16), q_bd) + _dot_nt(_bias_pieces(c).astype(BF16), e_bd)

    def cache_rows(ref, ci):
        return jnp.concatenate([ref[0, pl.ds(ci * chunk * FOX_HEADS + h, chunk, stride=FOX_HEADS), :]
                                for h in range(FOX_HEADS)], axis=1)

    pad_lanes = lambda a: jnp.concatenate([a, jnp.zeros((a.shape[0], LANES - a.shape[1]), F32)], axis=1)
    pad_rows = lambda a: jnp.concatenate([a, jnp.zeros((LANES - T, a.shape[1]), F32)], axis=0)
    carry = jnp.zeros((1, LANES), F32)
    s_parts = []
    for ci in range(P // chunk):
        c, carry = _cumsum_rows(pad_lanes(clf_ref[0, ci * chunk:(ci + 1) * chunk, :]), carry)
        s_parts.append(scores(cache_rows(ck_ref, ci), c))
    c_new, _ = _cumsum_rows(pad_rows(pad_lanes(lfn_ref[0])), carry)
    s_new = scores(pad_rows(kn_ref[0]), c_new)
    s_new = jnp.where(_iota((LANES, HT), 0) <= (_iota((LANES, HT), 1) & (T - 1)), s_new, -jnp.inf)

    m = jnp.max(s_new, axis=0, keepdims=True)
    for s in s_parts:
        m = jnp.maximum(m, jnp.max(s, axis=0, keepdims=True))
    ones = jnp.ones((chunk, LANES), BF16)
    p_new = jnp.exp2(s_new - m).T.astype(BF16)
    r = _dot(p_new, pad_rows(vn_ref[0]).astype(BF16))
    l = _dot(p_new, ones[:LANES])
    for ci, s in enumerate(s_parts):
        p = jnp.exp2(s - m).T.astype(BF16)
        r = r + _dot(p, cache_rows(cv_ref, ci).astype(BF16))
        l = l + _dot(p, ones)
    r = r / jnp.concatenate([l] * (FOX_WIDTH // LANES), axis=1)
    col_head = _iota((T, FOX_WIDTH), 1) >> 6
    o = jnp.zeros((T, FOX_WIDTH), F32)
    for h in range(FOX_HEADS):
        o = jnp.where(col_head == h, r[h * T:(h + 1) * T, :], o)
    o_ref[0] = o.astype(BF16)


def _fox_sample(q, k_new, v_new, lf_new, cache_k, cache_v, cache_lf, *, chunk):
    B, T, _ = q.shape
    P = cache_lf.shape[1]
    new = lambda w: pl.BlockSpec((1, T, w), lambda b: (b, 0, 0))
    old = lambda r, w: pl.BlockSpec((1, r, w), lambda b: (b, 0, 0))
    return pl.pallas_call(
        functools.partial(_fox_sample_kernel, chunk=chunk),
        grid=(B,),
        in_specs=[new(FOX_WIDTH), new(FOX_WIDTH), new(FOX_WIDTH), new(FOX_HEADS),
                  old(P * FOX_HEADS, FOX_HEAD_DIM), old(P * FOX_HEADS, FOX_HEAD_DIM), old(P, FOX_HEADS)],
        out_specs=new(FOX_WIDTH),
        out_shape=jax.ShapeDtypeStruct((B, T, FOX_WIDTH), BF16),
        compiler_params=pltpu.CompilerParams(dimension_semantics=("parallel",),
                                             vmem_limit_bytes=VMEM_LIMIT),
        name="fox_sample",
    )(q, k_new, v_new, lf_new, cache_k, cache_v, cache_lf)


def _merge_kernel(x_ref, ofox_ref, opool_ref, omem_ref, g_ref, wg_ref, wbf_ref, wbp_ref, wbm_ref,
                  wout_ref, h_ref):
    D = x_ref.shape[-1]
    x = x_ref[0]
    xn = _rms(x, g_ref[...]).astype(BF16)
    m = None
    for bi, (o_ref, w_ref) in enumerate(((ofox_ref, wbf_ref), (opool_ref, wbp_ref), (omem_ref, wbm_ref))):
        gate = jax.nn.sigmoid(_dot(xn, wg_ref[:, bi * D:(bi + 1) * D]))
        term = gate * _dot(o_ref[0], w_ref[...])
        m = term if m is None else m + term
    h_ref[0] = x + _dot(m.astype(BF16), wout_ref[...])


def _merge(x, ofox, opool, omem, g, wg, wbf, wbp, wbm, wout, *, tm):
    B, S, D = x.shape
    tile = lambda w: pl.BlockSpec((1, tm, w), lambda b, t: (b, t, 0))
    const = lambda a: pl.BlockSpec(a.shape, lambda b, t: (0,) * a.ndim)
    return pl.pallas_call(
        _merge_kernel,
        grid=(B, S // tm),
        in_specs=[tile(D), tile(FOX_WIDTH), tile(POOL_WIDTH), tile(MEM_WIDTH),
                  const(g), const(wg), const(wbf), const(wbp), const(wbm), const(wout)],
        out_specs=tile(D),
        out_shape=jax.ShapeDtypeStruct((B, S, D), F32),
        compiler_params=pltpu.CompilerParams(dimension_semantics=("parallel", "parallel"),
                                             vmem_limit_bytes=VMEM_LIMIT),
        name="merge",
    )(x, ofox, opool, omem, g, wg, wbf, wbp, wbm, wout)


def _ffn_kernel(h_ref, g_ref, wup_ref, wdown_ref, gfin_ref, y_ref, *, ff_chunk, final_norm):
    h = h_ref[0]
    hn = _rms(h, g_ref[...]).astype(BF16)
    acc = h
    for c in range(wup_ref.shape[1] // ff_chunk):
        cols = slice(c * ff_chunk, (c + 1) * ff_chunk)
        a = jnp.maximum(_dot(hn, wup_ref[:, cols]), 0.0)
        acc = acc + _dot((a * a).astype(BF16), wdown_ref[cols, :])
    y_ref[0] = _rms(acc, gfin_ref[...]) if final_norm else acc


def _ffn(h, g, wup, wdown, gfin, *, tm, ff_chunk, final_norm):
    B, S, D = h.shape
    tile = pl.BlockSpec((1, tm, D), lambda b, t: (b, t, 0))
    const = lambda a: pl.BlockSpec(a.shape, lambda b, t: (0,) * a.ndim)
    return pl.pallas_call(
        functools.partial(_ffn_kernel, ff_chunk=ff_chunk, final_norm=final_norm),
        grid=(B, S // tm),
        in_specs=[tile, const(g), const(wup), const(wdown), const(gfin)],
        out_specs=tile,
        out_shape=jax.ShapeDtypeStruct((B, S, D), F32),
        compiler_params=pltpu.CompilerParams(dimension_semantics=("parallel", "parallel"),
                                             vmem_limit_bytes=VMEM_LIMIT),
        name="ffn",
    )(h, g, wup, wdown, gfin)


def _tile_rows(n, want):
    return want if n % want == 0 else n


def _repack_w_in(w_in):
    o_f = 3 * FOX_WIDTH
    o_u = o_f + FOX_HEADS
    o_qm = o_u + POOL_WIDTH
    o_g = o_qm + MEM_WIDTH
    b16 = lambda a: a.astype(BF16)
    wq, wk, wv = (w_in[:, i * FOX_WIDTH:(i + 1) * FOX_WIDTH] for i in range(3))
    wf_rep = _rep_lanes(w_in[:, o_f:o_u])
    weights = (b16(wk), b16(wv), b16(w_in[:, o_u:o_qm]), b16(w_in[:, o_qm:o_g]), b16(wf_rep),
               b16(wq).T, b16(wv).T)
    return weights, b16(w_in[:, o_g:])


def _rep_lanes(a):
    pad = jnp.zeros(a.shape[:-1] + (LANES - BIAS_LANES,), a.dtype)
    return jnp.concatenate([a] * BIAS_PIECES + [pad], axis=-1)


def kernel(x_prompt, x_sample, mem_prompt, cache_fox_k, cache_fox_v, cache_fox_logf, cache_pool,
           cache_mem_k, cache_mem_v, g_attn, w_in, b_f, w_pool, s_pool, g_mem, w_mkv,
           w_br_fox, w_br_pool, w_br_mem, w_out, g_mlp, w_up, w_down, g_final):
    depth = w_in.shape[0]
    B, S, D = x_prompt.shape
    BS, T, _ = x_sample.shape
    P = cache_fox_k.shape[2]
    M = mem_prompt.shape[1]
    tm_p = _tile_rows(S, 512)
    blk_p = _tile_rows(tm_p, 256)
    tm_s = _tile_rows(BS * T, 256)
    chunk_s = _tile_rows(P, 256)
    gfin = g_final.reshape(1, D)

    xp, xs = x_prompt, x_sample
    outs_p = [[] for _ in range(6)]
    outs_s = [[] for _ in range(4)]
    for l in range(depth):
        weights, wg = _repack_w_in(w_in[l])
        bf_rep = _rep_lanes(b_f[l]).reshape(1, LANES)
        g_a, g_m, g_f = g_attn[l].reshape(1, D), g_mem[l].reshape(1, D), g_mlp[l].reshape(1, D)
        wpool, spool = w_pool[l].astype(BF16), s_pool[l].reshape(1, POOL_WIDTH)
        wbf, wbp, wbm = (w.astype(BF16) for w in (w_br_fox[l], w_br_pool[l], w_br_mem[l]))
        wout, wup, wdown = (w.astype(BF16) for w in (w_out[l], w_up[l], w_down[l]))
        final = l == depth - 1

        mk, mv = _mem_kv(mem_prompt, g_m, w_mkv[l].astype(BF16))
        hist0 = jnp.zeros((B, HIST_ROWS, POOL_WIDTH), F32)
        k, v, logf, pnew, opool, omem, qt, ka, vt = _in_proj(
            xp, g_a, weights, bf_rep, hist0, mk, mv, wpool, spool, tm=tm_p, pos0=0, augment=True,
            blk=blk_p)
        ofox = _fox_prompt(qt, ka, vt, blk=blk_p)
        h = _merge(xp, ofox, opool, omem, g_a, wg, wbf, wbp, wbm, wout, tm=tm_p)
        xp = _ffn(h, g_f, wup, wdown, gfin, tm=tm_p, ff_chunk=1024, final_norm=final)
        for acc, val in zip(outs_p, (k.reshape(B, S, FOX_HEADS, FOX_HEAD_DIM),
                                     v.reshape(B, S, FOX_HEADS, FOX_HEAD_DIM), logf, pnew,
                                     mk.reshape(B, M, MEM_HEADS, MEM_HEAD_DIM),
                                     mv.reshape(B, M, MEM_HEADS, MEM_HEAD_DIM))):
            acc.append(val)

        hist = jnp.pad(cache_pool[l], ((0, 0), (HIST_ROWS - POOL_HIST, 0), (0, 0)))
        cmk = cache_mem_k[l].reshape(BS, M * MEM_HEADS, MEM_HEAD_DIM)
        cmv = cache_mem_v[l].reshape(BS, M * MEM_HEADS, MEM_HEAD_DIM)
        k, v, logf, pnew, opool, omem, q = _in_proj(
            xs, g_a, weights, bf_rep, hist, cmk, cmv, wpool, spool, tm=T, pos0=P, augment=False)
        ofox = _fox_sample(q, k, v, logf, cache_fox_k[l].reshape(BS, P * FOX_HEADS, FOX_HEAD_DIM),
                           cache_fox_v[l].reshape(BS, P * FOX_HEADS, FOX_HEAD_DIM), cache_fox_logf[l],
                           chunk=chunk_s)
        flat = lambda a: a.reshape(1, BS * T, a.shape[-1])
        h = _merge(flat(xs), flat(ofox), flat(opool), flat(omem), g_a, wg, wbf, wbp, wbm, wout, tm=tm_s)
        xs = _ffn(h, g_f, wup, wdown, gfin, tm=tm_s, ff_chunk=1024, final_norm=final).reshape(BS, T, D)
        for acc, val in zip(outs_s, (k.reshape(BS, T, FOX_HEADS, FOX_HEAD_DIM),
                                     v.reshape(BS, T, FOX_HEADS, FOX_HEAD_DIM), logf, pnew)):
            acc.append(val)

    stack = lambda o: o[0][None] if depth == 1 else jnp.stack(o)
    return (xp, xs, *(stack(o) for o in outs_p), *(stack(o) for o in outs_s))
```

```python
import functools
import math

import jax
import jax.numpy as jnp
from jax import lax
from jax.experimental import pallas as pl
from jax.experimental.pallas import tpu as pltpu

F32 = jnp.float32
BF16 = jnp.bfloat16

EPS = 1e-6
LANES = 128
FOX_HEADS = 8
FOX_HEAD_DIM = 64
FOX_WIDTH = FOX_HEADS * FOX_HEAD_DIM
POOL_WINDOWS = (2, 4, 8, 16)
POOL_GROUPS = 4
POOL_GROUP_DIM = 128
POOL_WIDTH = POOL_GROUPS * POOL_GROUP_DIM
POOL_HIST = 15
HIST_ROWS = 16
MEM_HEADS = 4
MEM_HEAD_DIM = 128
MEM_WIDTH = MEM_HEADS * MEM_HEAD_DIM
N_BRANCH = 3
LOG2E = math.log2(math.e)
BIAS_PIECES = 3
BIAS_LANES = BIAS_PIECES * FOX_HEADS
V_SLAB = 80
KEY_BLOCKS_PER_ITER = 4
SCORES_AHEAD = 5
CUMSUM_CHUNK = 128
VMEM_LIMIT = 56 * 1024 * 1024


def _dot(a, b):
    return jnp.dot(a, b, preferred_element_type=F32)


def _dot_nt(a, b):
    return lax.dot_general(a, b, (((1,), (1,)), ((), ())), preferred_element_type=F32)


def _rms(x, g):
    return x * lax.rsqrt(jnp.mean(x * x, axis=-1, keepdims=True) + EPS) * g


def _iota(shape, dim):
    return lax.broadcasted_iota(jnp.int32, shape, dim)


def _split3(x):
    p0 = x.astype(BF16)
    r1 = x - p0.astype(F32)
    p1 = r1.astype(BF16)
    p2 = (r1 - p1.astype(F32)).astype(BF16)
    return p0, p1, p2


def _cumsum_rows(x, carry):
    n = x.shape[0]
    ch = min(n, CUMSUM_CHUNK)
    tril = (_iota((ch, ch), 1) <= _iota((ch, ch), 0)).astype(BF16)
    outs = []
    for c0 in range(0, n, ch):
        r = _dot(tril, jnp.concatenate(_split3(x[c0:c0 + ch]), axis=1))
        c = r[:, :LANES] + r[:, LANES:2 * LANES] + r[:, 2 * LANES:] + carry
        carry = c[ch - 1:ch]
        outs.append(c)
    return (outs[0] if len(outs) == 1 else jnp.concatenate(outs, axis=0)), carry


def _bias_pieces(c):
    b = c * (-LOG2E)
    p0 = b.astype(BF16).astype(F32)
    r1 = b - p0
    p1 = r1.astype(BF16).astype(F32)
    p2 = (r1 - p1).astype(BF16).astype(F32)
    grp = _iota(c.shape, 1) >> 3
    return jnp.where(grp == 0, p0, jnp.where(grp == 1, p1, jnp.where(grp == 2, p2, 0.0)))


def _mem_kv_kernel(mem_ref, g_ref, w_ref, mk_ref, mv_ref):
    M = mem_ref.shape[1]
    xn = _rms(mem_ref[0], g_ref[...]).astype(BF16)
    for out_ref, off in ((mk_ref, 0), (mv_ref, MEM_WIDTH)):
        kv = _dot(xn, w_ref[:, off:off + MEM_WIDTH])
        for h in range(MEM_HEADS):
            out_ref[0, pl.ds(h, M, stride=MEM_HEADS), :] = kv[:, h * MEM_HEAD_DIM:(h + 1) * MEM_HEAD_DIM]


def _mem_kv(mem, g, w_mkv):
    B, M, D = mem.shape
    return pl.pallas_call(
        _mem_kv_kernel,
        grid=(B,),
        in_specs=[pl.BlockSpec((1, M, D), lambda b: (b, 0, 0)), pl.BlockSpec((1, D), lambda b: (0, 0)),
                  pl.BlockSpec((D, 2 * MEM_WIDTH), lambda b: (0, 0))],
        out_specs=[pl.BlockSpec((1, M * MEM_HEADS, MEM_HEAD_DIM), lambda b: (b, 0, 0))] * 2,
        out_shape=[jax.ShapeDtypeStruct((B, M * MEM_HEADS, MEM_HEAD_DIM), F32)] * 2,
        compiler_params=pltpu.CompilerParams(dimension_semantics=("parallel",),
                                             vmem_limit_bytes=VMEM_LIMIT),
        name="mem_kv",
    )(mem, g, w_mkv)


def _in_proj_kernel(x_ref, g_ref, wk_ref, wv_ref, wu_ref, wqm_ref, wf_ref, wqt_ref, wvt_ref, bf_ref, hist_ref,
                    mk_ref, mv_ref, wpool_ref, spool_ref,
                    k_ref, v_ref, logf_ref, pnew_ref, opool_ref, omem_ref, *rest, tm, pos0, augment, blk):
    extra, (uext_ref, carry_ref) = rest[:-2], rest[-2:]
    t = pl.program_id(1)
    last = pl.num_programs(1) - 1
    q_scale = FOX_HEAD_DIM ** -0.5 * LOG2E

    @pl.when(t == 0)
    def _():
        uext_ref[0:HIST_ROWS, :] = hist_ref[0]
        carry_ref[...] = jnp.zeros_like(carry_ref)

    xn = _rms(x_ref[0], g_ref[...]).astype(BF16)
    lane = _iota((tm, LANES), 1)
    mem_cols = [slice(h * MEM_HEAD_DIM, (h + 1) * MEM_HEAD_DIM) for h in range(MEM_HEADS)]

    zf = _dot(xn, wf_ref[...]) + bf_ref[...]
    k = _dot(xn, wk_ref[...])
    v = _dot(xn, wv_ref[...])
    u = _dot(xn, wu_ref[...])
    qm = _dot(xn, wqm_ref[...]).astype(BF16)
    if augment:
        qt_all = (_dot_nt(wqt_ref[...], xn) * q_scale).astype(BF16)
        vt_all = _dot_nt(wvt_ref[...], xn).astype(BF16)
    else:
        q = _dot_nt(xn, wqt_ref[...]) * q_scale
    n_mem = mk_ref.shape[1] // MEM_HEADS
    mem_head = lambda ref, h: ref[0, pl.ds(h, n_mem, stride=MEM_HEADS), :].astype(BF16)
    mem_s = [_dot_nt(qm[:, cols], mem_head(mk_ref, h)) * MEM_HEAD_DIM ** -0.5
             for h, cols in enumerate(mem_cols)]

    k_ref[0] = k
    v_ref[0] = v
    logf = jnp.minimum(zf, 0.0) - jnp.log1p(jnp.exp(-jnp.abs(zf)))
    logf_ref[0] = logf[:, :FOX_HEADS]

    if augment:
        qt_ref, ka_ref, vt_ref = extra
        c, carry_ref[...] = _cumsum_rows(logf, carry_ref[...])
        bias = pltpu.roll(_bias_pieces(c), FOX_HEAD_DIM, axis=1)
        in_bias = (lane >= FOX_HEAD_DIM) & (lane < FOX_HEAD_DIM + BIAS_LANES)
        for pair in range(FOX_HEADS // 2):
            even = k[:, pair * LANES:(pair + 1) * LANES]
            odd = pltpu.roll(even, FOX_HEAD_DIM, axis=1)
            for par, data in ((0, even), (1, odd)):
                h = 2 * pair + par
                tail = jnp.where(in_bias & ((lane & 7) == h), bias, 0.0)
                ka_ref[0, :, h * LANES:(h + 1) * LANES] = jnp.where(lane < FOX_HEAD_DIM, data, tail).astype(BF16)
        q_tail = jnp.where(_iota((LANES - FOX_HEAD_DIM, blk), 0) < BIAS_LANES, 1.0, 0.0).astype(BF16)
        v_tail = jnp.where(_iota((V_SLAB - FOX_HEAD_DIM, blk), 0) == 0, 1.0, 0.0).astype(BF16)
        for h in range(FOX_HEADS):
            qt = qt_all[h * FOX_HEAD_DIM:(h + 1) * FOX_HEAD_DIM]
            vt = vt_all[h * FOX_HEAD_DIM:(h + 1) * FOX_HEAD_DIM]
            for j in range(tm // blk):
                qt_ref[0, j, h * LANES:h * LANES + FOX_HEAD_DIM, :] = qt[:, j * blk:(j + 1) * blk]
                qt_ref[0, j, h * LANES + FOX_HEAD_DIM:(h + 1) * LANES, :] = q_tail
                vt_ref[0, j, h * V_SLAB:h * V_SLAB + FOX_HEAD_DIM, :] = vt[:, j * blk:(j + 1) * blk]
                vt_ref[0, j, h * V_SLAB + FOX_HEAD_DIM:(h + 1) * V_SLAB, :] = v_tail
    else:
        extra[0][0] = q

    uext_ref[HIST_ROWS:HIST_ROWS + tm, :] = u
    pos = pos0 + t * tm + _iota((tm, 1), 0)
    for gi, w in enumerate(POOL_WINDOWS):
        cols = slice(gi * POOL_GROUP_DIM, (gi + 1) * POOL_GROUP_DIM)
        sw = uext_ref[:, cols]
        shift = 1
        while shift < w:
            sw = sw + pltpu.roll(sw, shift, axis=0)
            shift *= 2
        cnt = jnp.minimum(w, pos + 1).astype(F32)
        pooled = sw[HIST_ROWS:, :] / cnt - u[:, cols]
        mixed = _dot(pooled.astype(BF16), wpool_ref[gi]) * spool_ref[:, cols]
        opool_ref[0, :, cols] = mixed.astype(BF16)

    @pl.when(t == last)
    def _():
        pnew_ref[0] = uext_ref[tm + 1:tm + HIST_ROWS, :]

    uext_ref[0:HIST_ROWS, :] = uext_ref[tm:tm + HIST_ROWS, :]

    for h, (s, cols) in enumerate(zip(mem_s, mem_cols)):
        e = jnp.exp(s - jnp.max(s, axis=-1, keepdims=True))
        o = _dot(e.astype(BF16), mem_head(mv_ref, h)) / jnp.sum(e, axis=-1, keepdims=True)
        omem_ref[0, :, cols] = o.astype(BF16)


def _in_proj(x, g, weights, bf_rep, hist, mk, mv, wpool, spool, *, tm, pos0, augment, blk=None):
    B, S, D = x.shape
    nt = S // tm
    tile = lambda w: pl.BlockSpec((1, tm, w), lambda b, t: (b, t, 0))
    per_b = lambda r, w: pl.BlockSpec((1, r, w), lambda b, t: (b, 0, 0))
    const = lambda shape: pl.BlockSpec(shape, lambda b, t: (0,) * len(shape))
    if augment:
        W = FOX_HEADS * LANES
        t_tile = lambda rows: pl.BlockSpec((1, tm // blk, rows, blk), lambda b, t: (b, t, 0, 0))
        abc_specs = [t_tile(W), tile(W), t_tile(FOX_HEADS * V_SLAB)]
        abc_shapes = [jax.ShapeDtypeStruct((B, S // blk, W, blk), BF16),
                      jax.ShapeDtypeStruct((B, S, W), BF16),
                      jax.ShapeDtypeStruct((B, S // blk, FOX_HEADS * V_SLAB, blk), BF16)]
    else:
        abc_specs = [tile(FOX_WIDTH)]
        abc_shapes = [jax.ShapeDtypeStruct((B, S, FOX_WIDTH), F32)]
    return pl.pallas_call(
        functools.partial(_in_proj_kernel, tm=tm, pos0=pos0, augment=augment, blk=blk),
        grid=(B, nt),
        in_specs=[tile(D), const((1, D))] + [const(w.shape) for w in weights] + [
            const((1, LANES)), per_b(HIST_ROWS, POOL_WIDTH), per_b(*mk.shape[1:]), per_b(*mv.shape[1:]),
            const((POOL_GROUPS, POOL_GROUP_DIM, POOL_GROUP_DIM)), const((1, POOL_WIDTH))],
        out_specs=[tile(FOX_WIDTH), tile(FOX_WIDTH), tile(FOX_HEADS), per_b(POOL_HIST, POOL_WIDTH),
                   tile(POOL_WIDTH), tile(MEM_WIDTH)] + abc_specs,
        out_shape=[jax.ShapeDtypeStruct((B, S, FOX_WIDTH), F32),
                   jax.ShapeDtypeStruct((B, S, FOX_WIDTH), F32),
                   jax.ShapeDtypeStruct((B, S, FOX_HEADS), F32),
                   jax.ShapeDtypeStruct((B, POOL_HIST, POOL_WIDTH), F32),
                   jax.ShapeDtypeStruct((B, S, POOL_WIDTH), BF16),
                   jax.ShapeDtypeStruct((B, S, MEM_WIDTH), BF16)] + abc_shapes,
        scratch_shapes=[pltpu.VMEM((HIST_ROWS + tm, POOL_WIDTH), F32), pltpu.VMEM((1, LANES), F32)],
        compiler_params=pltpu.CompilerParams(dimension_semantics=("parallel", "arbitrary"),
                                             vmem_limit_bytes=VMEM_LIMIT),
        name="in_proj",
    )(x, g, *weights, bf_rep, hist, mk, mv, wpool, spool)


def _fox_prompt_kernel(qt_ref, k_ref, vt_ref, o_ref, m_ref, acc_ref, *, blk):
    i = pl.program_id(1)
    m_ref[...] = jnp.full_like(m_ref, -jnp.inf)
    acc_ref[...] = jnp.zeros_like(acc_ref)
    slabs = [slice(h * LANES, (h + 1) * LANES) for h in range(FOX_HEADS)]

    def scores(j, h):
        start = pl.multiple_of(j * blk, blk)
        return _dot(k_ref[0, pl.ds(start, blk), slabs[h]], qt_ref[0, 0, slabs[h], :])

    def accumulate(j, h, st, masked):
        if masked:
            st = jnp.where(_iota((blk, blk), 0) <= _iota((blk, blk), 1), st, -jnp.inf)
        m_prev = m_ref[h]
        m_new = jnp.maximum(m_prev, jnp.max(st, axis=0, keepdims=True))
        p = jnp.exp2(st - m_new).astype(BF16)
        pv = _dot(vt_ref[0, j, h * V_SLAB:(h + 1) * V_SLAB, :], p)
        acc_ref[h] = jnp.exp2(m_prev - m_new) * acc_ref[h] + pv
        m_ref[h] = m_new

    def run(blocks):
        work = [(j, h, masked) for j, masked in blocks for h in range(FOX_HEADS)]
        pending = [scores(j, h) for j, h, _ in work[:SCORES_AHEAD]]
        for n, (j, h, masked) in enumerate(work):
            if n + SCORES_AHEAD < len(work):
                pending.append(scores(*work[n + SCORES_AHEAD][:2]))
            accumulate(j, h, pending.pop(0), masked)

    shift = KEY_BLOCKS_PER_ITER.bit_length() - 1
    n_iter = i >> shift

    def body(jj, carry):
        run([(KEY_BLOCKS_PER_ITER * jj + r, False) for r in range(KEY_BLOCKS_PER_ITER)])
        return carry

    lax.fori_loop(0, n_iter, body, 0)
    for rem in range(KEY_BLOCKS_PER_ITER):
        @pl.when(i - (n_iter << shift) == rem)
        def _(rem=rem):
            run([(i - rem + r, False) for r in range(rem)] + [(i, True)])

    for pair in range(FOX_HEADS // 2):
        halves = []
        for par in (0, 1):
            acc = acc_ref[2 * pair + par]
            halves.append(acc[:FOX_HEAD_DIM] / acc[FOX_HEAD_DIM:FOX_HEAD_DIM + 1])
        o_ref[0, :, pair * LANES:(pair + 1) * LANES] = jnp.concatenate(halves, axis=0).T.astype(BF16)


def _fox_prompt(qt, ka, vt, *, blk):
    B, S, W = ka.shape
    nblk = S // blk
    return pl.pallas_call(
        functools.partial(_fox_prompt_kernel, blk=blk),
        grid=(B, nblk),
        in_specs=[pl.BlockSpec((1, 1, W, blk), lambda b, i: (b, i, 0, 0)),
                  pl.BlockSpec((1, S, W), lambda b, i: (b, 0, 0)),
                  pl.BlockSpec((1, nblk, FOX_HEADS * V_SLAB, blk), lambda b, i: (b, 0, 0, 0))],
        out_specs=pl.BlockSpec((1, blk, FOX_WIDTH), lambda b, i: (b, i, 0)),
        out_shape=jax.ShapeDtypeStruct((B, S, FOX_WIDTH), BF16),
        scratch_shapes=[pltpu.VMEM((FOX_HEADS, 1, blk), F32), pltpu.VMEM((FOX_HEADS, V_SLAB, blk), F32)],
        compiler_params=pltpu.CompilerParams(
            dimension_semantics=("parallel", "arbitrary"), vmem_limit_bytes=VMEM_LIMIT),
        name="fox_prompt",
    )(qt, ka, vt)


def _fox_sample_kernel(q_ref, kn_ref, vn_ref, lfn_ref, ck_ref, cv_ref, clf_ref, o_ref, *, chunk):
    T = q_ref.shape[1]
    P = clf_ref.shape[1]
    HT = FOX_HEADS * T
    t_shift = T.bit_length() - 1
    assert HT == LANES and 1 << t_shift == T

    q_rep = jnp.concatenate([q_ref[0]] * FOX_HEADS, axis=0)
    row_head = _iota((HT, FOX_WIDTH), 0) >> t_shift
    q_bd = jnp.where(row_head == (_iota((HT, FOX_WIDTH), 1) >> 6), q_rep, 0.0).astype(BF16)
    e_lane = _iota((HT, LANES), 1)
    e_bd = jnp.where((e_lane < BIAS_LANES) & ((e_lane & 7) == (_iota((HT, LANES), 0) >> t_shift)),
                     1.0, 0.0).astype(BF16)

    def scores(keys, c):
        c = c + pltpu.roll(c, FOX_HEADS, axis=1) + pltpu.roll(c, 2 * FOX_HEADS, axis=1)
        return _dot_nt(keys.astype(BF16), q_bd) + _dot_nt(_bias_pieces(c).astype(BF16), e_bd)

    cache_rows = lambda ref, ci: ref[0, ci * chunk:(ci + 1) * chunk, :]
    pad_lanes = lambda a: jnp.concatenate([a, jnp.zeros((a.shape[0], LANES - a.shape[1]), F32)], axis=1)
    pad_rows = lambda a: jnp.concatenate([a, jnp.zeros((LANES - T, a.shape[1]), F32)], axis=0)
    carry = jnp.zeros((1, LANES), F32)
    s_parts = []
    for ci in range(P // chunk):
        c, carry = _cumsum_rows(pad_lanes(clf_ref[0, ci * chunk:(ci + 1) * chunk, :]), carry)
        s_parts.append(scores(cache_rows(ck_ref, ci), c))
    c_new, _ = _cumsum_rows(pad_rows(pad_lanes(lfn_ref[0])), carry)
    s_new = scores(pad_rows(kn_ref[0]), c_new)
    s_new = jnp.where(_iota((LANES, HT), 0) <= (_iota((LANES, HT), 1) & (T - 1)), s_new, -jnp.inf)

    m = jnp.max(s_new, axis=0, keepdims=True)
    for s in s_parts:
        m = jnp.maximum(m, jnp.max(s, axis=0, keepdims=True))
    ones = jnp.ones((chunk, LANES), BF16)
    p_new = jnp.exp2(s_new - m).T.astype(BF16)
    r = _dot(p_new, pad_rows(vn_ref[0]).astype(BF16))
    l = _dot(p_new, ones[:LANES])
    for ci, s in enumerate(s_parts):
        p = jnp.exp2(s - m).T.astype(BF16)
        r = r + _dot(p, cache_rows(cv_ref, ci).astype(BF16))
        l = l + _dot(p, ones)
    r = r / jnp.concatenate([l] * (FOX_WIDTH // LANES), axis=1)
    col_head = _iota((T, FOX_WIDTH), 1) >> 6
    o = jnp.zeros((T, FOX_WIDTH), F32)
    for h in range(FOX_HEADS):
        o = jnp.where(col_head == h, r[h * T:(h + 1) * T, :], o)
    o_ref[0] = o.astype(BF16)


def _fox_sample(q, k_new, v_new, lf_new, cache_k, cache_v, cache_lf, *, chunk):
    B, T, _ = q.shape
    P = cache_lf.shape[1]
    new = lambda w: pl.BlockSpec((1, T, w), lambda b: (b, 0, 0))
    old = lambda w: pl.BlockSpec((1, P, w), lambda b: (b, 0, 0))
    return pl.pallas_call(
        functools.partial(_fox_sample_kernel, chunk=chunk),
        grid=(B,),
        in_specs=[new(FOX_WIDTH), new(FOX_WIDTH), new(FOX_WIDTH), new(FOX_HEADS),
                  old(FOX_WIDTH), old(FOX_WIDTH), old(FOX_HEADS)],
        out_specs=new(FOX_WIDTH),
        out_shape=jax.ShapeDtypeStruct((B, T, FOX_WIDTH), BF16),
        compiler_params=pltpu.CompilerParams(dimension_semantics=("parallel",),
                                             vmem_limit_bytes=VMEM_LIMIT),
        name="fox_sample",
    )(q, k_new, v_new, lf_new, cache_k, cache_v, cache_lf)


def _merge_ffn_kernel(x_ref, ofox_ref, opool_ref, omem_ref, ga_ref, wg_ref, wbf_ref, wbp_ref, wbm_ref,
                      wout_ref, gm_ref, wup_ref, wdown_ref, gfin_ref, y_ref, *, streams, ff_chunk, final_norm):
    D = x_ref.shape[-1]
    rows = x_ref.shape[1] // streams
    groups = [slice(i * rows, (i + 1) * rows) for i in range(streams)]
    branches = ((ofox_ref, wbf_ref), (opool_ref, wbp_ref), (omem_ref, wbm_ref))

    xs = [x_ref[0, r, :] for r in groups]
    pre = []
    for x, r in zip(xs, groups):
        xn = _rms(x, ga_ref[...]).astype(BF16)
        pre.append([(_dot(xn, wg_ref[:, bi * D:(bi + 1) * D]), _dot(o_ref[0, r, :], w_ref[...]))
                    for bi, (o_ref, w_ref) in enumerate(branches)])
    hs = []
    for x, terms in zip(xs, pre):
        (g0, v0), (g1, v1), (g2, v2) = terms
        m = jax.nn.sigmoid(g0) * v0 + jax.nn.sigmoid(g1) * v1 + jax.nn.sigmoid(g2) * v2
        hs.append(x + _dot(m.astype(BF16), wout_ref[...]))
    hns = [_rms(h, gm_ref[...]).astype(BF16) for h in hs]
    accs = list(hs)
    for c in range(wup_ref.shape[1] // ff_chunk):
        cols = slice(c * ff_chunk, (c + 1) * ff_chunk)
        ups = [jnp.maximum(_dot(hn, wup_ref[:, cols]), 0.0) for hn in hns]
        accs = [acc + _dot((a * a).astype(BF16), wdown_ref[cols, :]) for acc, a in zip(accs, ups)]
    for acc, r in zip(accs, groups):
        y_ref[0, r, :] = _rms(acc, gfin_ref[...]) if final_norm else acc


def _merge_ffn(x, ofox, opool, omem, ga, wg, wbf, wbp, wbm, wout, gm, wup, wdown, gfin, *, tm, streams,
               ff_chunk, final_norm):
    B, S, D = x.shape
    tile = lambda w: pl.BlockSpec((1, tm, w), lambda b, t: (b, t, 0))
    const = lambda a: pl.BlockSpec(a.shape, lambda b, t: (0,) * a.ndim, pipeline_mode=pl.Buffered(1))
    consts = (ga, wg, wbf, wbp, wbm, wout, gm, wup, wdown, gfin)
    return pl.pallas_call(
        functools.partial(_merge_ffn_kernel, streams=streams, ff_chunk=ff_chunk, final_norm=final_norm),
        grid=(B, S // tm),
        in_specs=[tile(D), tile(FOX_WIDTH), tile(POOL_WIDTH), tile(MEM_WIDTH)] + [const(a) for a in consts],
        out_specs=tile(D),
        out_shape=jax.ShapeDtypeStruct((B, S, D), F32),
        compiler_params=pltpu.CompilerParams(dimension_semantics=("parallel", "parallel"),
                                             vmem_limit_bytes=VMEM_LIMIT),
        name="merge_ffn",
    )(x, ofox, opool, omem, *consts)


def _tile_rows(n, want):
    return want if n % want == 0 else n


def _repack_w_in(w_in):
    o_f = 3 * FOX_WIDTH
    o_u = o_f + FOX_HEADS
    o_qm = o_u + POOL_WIDTH
    o_g = o_qm + MEM_WIDTH
    b16 = lambda a: a.astype(BF16)
    wq, wk, wv = (w_in[:, i * FOX_WIDTH:(i + 1) * FOX_WIDTH] for i in range(3))
    wf_rep = _rep_lanes(w_in[:, o_f:o_u])
    weights = (b16(wk), b16(wv), b16(w_in[:, o_u:o_qm]), b16(w_in[:, o_qm:o_g]), b16(wf_rep),
               b16(wq).T, b16(wv).T)
    return weights, b16(w_in[:, o_g:])


def _rep_lanes(a):
    pad = jnp.zeros(a.shape[:-1] + (LANES - BIAS_LANES,), a.dtype)
    return jnp.concatenate([a] * BIAS_PIECES + [pad], axis=-1)


def kernel(x_prompt, x_sample, mem_prompt, cache_fox_k, cache_fox_v, cache_fox_logf, cache_pool,
           cache_mem_k, cache_mem_v, g_attn, w_in, b_f, w_pool, s_pool, g_mem, w_mkv,
           w_br_fox, w_br_pool, w_br_mem, w_out, g_mlp, w_up, w_down, g_final):
    depth = w_in.shape[0]
    B, S, D = x_prompt.shape
    BS, T, _ = x_sample.shape
    P = cache_fox_k.shape[2]
    M = mem_prompt.shape[1]
    tm_p = _tile_rows(S, 512)
    blk_p = _tile_rows(tm_p, 256)
    tm_s = _tile_rows(BS * T, 256)
    chunk_s = _tile_rows(P, 256)
    gfin = g_final.reshape(1, D)

    xp, xs = x_prompt, x_sample
    outs_p = [[] for _ in range(6)]
    outs_s = [[] for _ in range(4)]
    for l in range(depth):
        weights, wg = _repack_w_in(w_in[l])
        bf_rep = _rep_lanes(b_f[l]).reshape(1, LANES)
        g_a, g_m, g_f = g_attn[l].reshape(1, D), g_mem[l].reshape(1, D), g_mlp[l].reshape(1, D)
        wpool, spool = w_pool[l].astype(BF16), s_pool[l].reshape(1, POOL_WIDTH)
        wbf, wbp, wbm = (w.astype(BF16) for w in (w_br_fox[l], w_br_pool[l], w_br_mem[l]))
        wout, wup, wdown = (w.astype(BF16) for w in (w_out[l], w_up[l], w_down[l]))
        final = l == depth - 1

        mk, mv = _mem_kv(mem_prompt, g_m, w_mkv[l].astype(BF16))
        hist0 = jnp.zeros((B, HIST_ROWS, POOL_WIDTH), F32)
        k, v, logf, pnew, opool, omem, qt, ka, vt = _in_proj(
            xp, g_a, weights, bf_rep, hist0, mk, mv, wpool, spool, tm=tm_p, pos0=0, augment=True,
            blk=blk_p)
        ofox = _fox_prompt(qt, ka, vt, blk=blk_p)
        post = functools.partial(_merge_ffn, ga=g_a, wg=wg, wbf=wbf, wbp=wbp, wbm=wbm, wout=wout, gm=g_f,
                                 wup=wup, wdown=wdown, gfin=gfin, ff_chunk=1024, final_norm=final)
        xp = post(xp, ofox, opool, omem, tm=tm_p, streams=2 if tm_p % 512 == 0 else 1)
        for acc, val in zip(outs_p, (k.reshape(B, S, FOX_HEADS, FOX_HEAD_DIM),
                                     v.reshape(B, S, FOX_HEADS, FOX_HEAD_DIM), logf, pnew,
                                     mk.reshape(B, M, MEM_HEADS, MEM_HEAD_DIM),
                                     mv.reshape(B, M, MEM_HEADS, MEM_HEAD_DIM))):
            acc.append(val)

        hist = jnp.pad(cache_pool[l], ((0, 0), (HIST_ROWS - POOL_HIST, 0), (0, 0)))
        cmk = cache_mem_k[l].reshape(BS, M * MEM_HEADS, MEM_HEAD_DIM)
        cmv = cache_mem_v[l].reshape(BS, M * MEM_HEADS, MEM_HEAD_DIM)
        k, v, logf, pnew, opool, omem, q = _in_proj(
            xs, g_a, weights, bf_rep, hist, cmk, cmv, wpool, spool, tm=T, pos0=P, augment=False)
        ofox = _fox_sample(q, k, v, logf, cache_fox_k[l].astype(BF16).reshape(BS, P, FOX_WIDTH),
                           cache_fox_v[l].astype(BF16).reshape(BS, P, FOX_WIDTH), cache_fox_logf[l],
                           chunk=chunk_s)
        flat = lambda a: a.reshape(1, BS * T, a.shape[-1])
        xs = post(flat(xs), flat(ofox), flat(opool), flat(omem), tm=tm_s, streams=1).reshape(BS, T, D)
        for acc, val in zip(outs_s, (k.reshape(BS, T, FOX_HEADS, FOX_HEAD_DIM),
                                     v.reshape(BS, T, FOX_HEADS, FOX_HEAD_DIM), logf, pnew)):
            acc.append(val)

    stack = lambda o: o[0][None] if depth == 1 else jnp.stack(o)
    return (xp, xs, *(stack(o) for o in outs_p), *(stack(o) for o in outs_s))
```

```python
import functools
import math

import jax
import jax.numpy as jnp
from jax import lax
from jax.experimental import pallas as pl
from jax.experimental.pallas import tpu as pltpu

F32 = jnp.float32
BF16 = jnp.bfloat16

EPS = 1e-6
LANES = 128
FOX_HEADS = 8
FOX_HEAD_DIM = 64
FOX_WIDTH = FOX_HEADS * FOX_HEAD_DIM
POOL_WINDOWS = (2, 4, 8, 16)
POOL_GROUPS = 4
POOL_GROUP_DIM = 128
POOL_WIDTH = POOL_GROUPS * POOL_GROUP_DIM
POOL_HIST = 15
HIST_ROWS = 16
MEM_HEADS = 4
MEM_HEAD_DIM = 128
MEM_WIDTH = MEM_HEADS * MEM_HEAD_DIM
N_BRANCH = 3
LOG2E = math.log2(math.e)
BIAS_PIECES = 3
BIAS_LANES = BIAS_PIECES * FOX_HEADS
V_SLAB = 80
KEY_BLOCKS_PER_ITER = 4
SCORES_AHEAD = 5
CUMSUM_CHUNK = 128
VMEM_LIMIT = 56 * 1024 * 1024


def _dot(a, b):
    return jnp.dot(a, b, preferred_element_type=F32)


def _dot_nt(a, b):
    return lax.dot_general(a, b, (((1,), (1,)), ((), ())), preferred_element_type=F32)


def _rms(x, g):
    return x * lax.rsqrt(jnp.mean(x * x, axis=-1, keepdims=True) + EPS) * g


def _iota(shape, dim):
    return lax.broadcasted_iota(jnp.int32, shape, dim)


def _split3(x):
    p0 = x.astype(BF16)
    r1 = x - p0.astype(F32)
    p1 = r1.astype(BF16)
    p2 = (r1 - p1.astype(F32)).astype(BF16)
    return p0, p1, p2


def _cumsum_rows(x, carry):
    n = x.shape[0]
    ch = min(n, CUMSUM_CHUNK)
    tril = (_iota((ch, ch), 1) <= _iota((ch, ch), 0)).astype(BF16)
    outs = []
    for c0 in range(0, n, ch):
        r = _dot(tril, jnp.concatenate(_split3(x[c0:c0 + ch]), axis=1))
        c = r[:, :LANES] + r[:, LANES:2 * LANES] + r[:, 2 * LANES:] + carry
        carry = c[ch - 1:ch]
        outs.append(c)
    return (outs[0] if len(outs) == 1 else jnp.concatenate(outs, axis=0)), carry


def _bias_pieces(c):
    b = c * (-LOG2E)
    p0 = b.astype(BF16).astype(F32)
    r1 = b - p0
    p1 = r1.astype(BF16).astype(F32)
    p2 = (r1 - p1).astype(BF16).astype(F32)
    grp = _iota(c.shape, 1) >> 3
    return jnp.where(grp == 0, p0, jnp.where(grp == 1, p1, jnp.where(grp == 2, p2, 0.0)))


def _mem_kv_kernel(mem_ref, g_ref, w_ref, mk_ref, mv_ref):
    M = mem_ref.shape[1]
    xn = _rms(mem_ref[0], g_ref[...]).astype(BF16)
    for out_ref, off in ((mk_ref, 0), (mv_ref, MEM_WIDTH)):
        kv = _dot(xn, w_ref[:, off:off + MEM_WIDTH])
        for h in range(MEM_HEADS):
            out_ref[0, pl.ds(h, M, stride=MEM_HEADS), :] = kv[:, h * MEM_HEAD_DIM:(h + 1) * MEM_HEAD_DIM]


def _mem_kv(mem, g, w_mkv):
    B, M, D = mem.shape
    return pl.pallas_call(
        _mem_kv_kernel,
        grid=(B,),
        in_specs=[pl.BlockSpec((1, M, D), lambda b: (b, 0, 0)), pl.BlockSpec((1, D), lambda b: (0, 0)),
                  pl.BlockSpec((D, 2 * MEM_WIDTH), lambda b: (0, 0))],
        out_specs=[pl.BlockSpec((1, M * MEM_HEADS, MEM_HEAD_DIM), lambda b: (b, 0, 0))] * 2,
        out_shape=[jax.ShapeDtypeStruct((B, M * MEM_HEADS, MEM_HEAD_DIM), F32)] * 2,
        compiler_params=pltpu.CompilerParams(dimension_semantics=("parallel",),
                                             vmem_limit_bytes=VMEM_LIMIT),
        name="mem_kv",
    )(mem, g, w_mkv)


def _in_proj_kernel(x_ref, g_ref, wk_ref, wv_ref, wu_ref, wqm_ref, wf_ref, wqt_ref, wvt_ref, bf_ref, hist_ref,
                    mk_ref, mv_ref, wpool_ref, spool_ref,
                    k_ref, v_ref, logf_ref, pnew_ref, opool_ref, omem_ref, *rest, tm, pos0, augment, blk):
    extra, (uext_ref, carry_ref) = rest[:-2], rest[-2:]
    t = pl.program_id(1)
    last = pl.num_programs(1) - 1
    q_scale = FOX_HEAD_DIM ** -0.5 * LOG2E

    @pl.when(t == 0)
    def _():
        uext_ref[0:HIST_ROWS, :] = hist_ref[0]
        carry_ref[...] = jnp.zeros_like(carry_ref)

    xn = _rms(x_ref[0], g_ref[...]).astype(BF16)
    lane = _iota((tm, LANES), 1)
    mem_cols = [slice(h * MEM_HEAD_DIM, (h + 1) * MEM_HEAD_DIM) for h in range(MEM_HEADS)]

    zf = _dot(xn, wf_ref[...]) + bf_ref[...]
    k = _dot(xn, wk_ref[...])
    v = _dot(xn, wv_ref[...])
    u = _dot(xn, wu_ref[...])
    qm = _dot(xn, wqm_ref[...]).astype(BF16)
    if augment:
        qt_all = (_dot_nt(wqt_ref[...], xn) * q_scale).astype(BF16)
        vt_all = _dot_nt(wvt_ref[...], xn).astype(BF16)
    else:
        q = _dot_nt(xn, wqt_ref[...]) * q_scale
    n_mem = mk_ref.shape[1] // MEM_HEADS
    mem_head = lambda ref, h: ref[0, pl.ds(h, n_mem, stride=MEM_HEADS), :].astype(BF16)
    mem_s = [_dot_nt(qm[:, cols], mem_head(mk_ref, h)) * MEM_HEAD_DIM ** -0.5
             for h, cols in enumerate(mem_cols)]

    k_ref[0] = k
    v_ref[0] = v
    logf = jnp.minimum(zf, 0.0) - jnp.log1p(jnp.exp(-jnp.abs(zf)))
    logf_ref[0] = logf[:, :FOX_HEADS]

    if augment:
        qt_ref, ka_ref, vt_ref = extra
        c, carry_ref[...] = _cumsum_rows(logf, carry_ref[...])
        bias = pltpu.roll(_bias_pieces(c), FOX_HEAD_DIM, axis=1)
        in_bias = (lane >= FOX_HEAD_DIM) & (lane < FOX_HEAD_DIM + BIAS_LANES)
        for pair in range(FOX_HEADS // 2):
            even = k[:, pair * LANES:(pair + 1) * LANES]
            odd = pltpu.roll(even, FOX_HEAD_DIM, axis=1)
            for par, data in ((0, even), (1, odd)):
                h = 2 * pair + par
                tail = jnp.where(in_bias & ((lane & 7) == h), bias, 0.0)
                ka_ref[0, :, h * LANES:(h + 1) * LANES] = jnp.where(lane < FOX_HEAD_DIM, data, tail).astype(BF16)
        q_tail = jnp.where(_iota((LANES - FOX_HEAD_DIM, blk), 0) < BIAS_LANES, 1.0, 0.0).astype(BF16)
        v_tail = jnp.where(_iota((V_SLAB - FOX_HEAD_DIM, blk), 0) == 0, 1.0, 0.0).astype(BF16)
        for h in range(FOX_HEADS):
            qt = qt_all[h * FOX_HEAD_DIM:(h + 1) * FOX_HEAD_DIM]
            vt = vt_all[h * FOX_HEAD_DIM:(h + 1) * FOX_HEAD_DIM]
            for j in range(tm // blk):
                qt_ref[0, j, h * LANES:h * LANES + FOX_HEAD_DIM, :] = qt[:, j * blk:(j + 1) * blk]
                qt_ref[0, j, h * LANES + FOX_HEAD_DIM:(h + 1) * LANES, :] = q_tail
                vt_ref[0, j, h * V_SLAB:h * V_SLAB + FOX_HEAD_DIM, :] = vt[:, j * blk:(j + 1) * blk]
                vt_ref[0, j, h * V_SLAB + FOX_HEAD_DIM:(h + 1) * V_SLAB, :] = v_tail
    else:
        extra[0][0] = q

    uext_ref[HIST_ROWS:HIST_ROWS + tm, :] = u
    pos = pos0 + t * tm + _iota((tm, 1), 0)
    for gi, w in enumerate(POOL_WINDOWS):
        cols = slice(gi * POOL_GROUP_DIM, (gi + 1) * POOL_GROUP_DIM)
        sw = uext_ref[:, cols]
        shift = 1
        while shift < w:
            sw = sw + pltpu.roll(sw, shift, axis=0)
            shift *= 2
        cnt = jnp.minimum(w, pos + 1).astype(F32)
        pooled = sw[HIST_ROWS:, :] / cnt - u[:, cols]
        mixed = _dot(pooled.astype(BF16), wpool_ref[gi]) * spool_ref[:, cols]
        opool_ref[0, :, cols] = mixed.astype(BF16)

    @pl.when(t == last)
    def _():
        pnew_ref[0] = uext_ref[tm + 1:tm + HIST_ROWS, :]

    uext_ref[0:HIST_ROWS, :] = uext_ref[tm:tm + HIST_ROWS, :]

    for h, (s, cols) in enumerate(zip(mem_s, mem_cols)):
        e = jnp.exp(s - jnp.max(s, axis=-1, keepdims=True))
        o = _dot(e.astype(BF16), mem_head(mv_ref, h)) / jnp.sum(e, axis=-1, keepdims=True)
        omem_ref[0, :, cols] = o.astype(BF16)


def _in_proj(x, g, weights, bf_rep, hist, mk, mv, wpool, spool, *, tm, pos0, augment, blk=None):
    B, S, D = x.shape
    nt = S // tm
    tile = lambda w: pl.BlockSpec((1, tm, w), lambda b, t: (b, t, 0))
    per_b = lambda r, w: pl.BlockSpec((1, r, w), lambda b, t: (b, 0, 0))
    const = lambda shape: pl.BlockSpec(shape, lambda b, t: (0,) * len(shape))
    if augment:
        W = FOX_HEADS * LANES
        t_tile = lambda rows: pl.BlockSpec((1, tm // blk, rows, blk), lambda b, t: (b, t, 0, 0))
        abc_specs = [t_tile(W), tile(W), t_tile(FOX_HEADS * V_SLAB)]
        abc_shapes = [jax.ShapeDtypeStruct((B, S // blk, W, blk), BF16),
                      jax.ShapeDtypeStruct((B, S, W), BF16),
                      jax.ShapeDtypeStruct((B, S // blk, FOX_HEADS * V_SLAB, blk), BF16)]
    else:
        abc_specs = [tile(FOX_WIDTH)]
        abc_shapes = [jax.ShapeDtypeStruct((B, S, FOX_WIDTH), F32)]
    return pl.pallas_call(
        functools.partial(_in_proj_kernel, tm=tm, pos0=pos0, augment=augment, blk=blk),
        grid=(B, nt),
        in_specs=[tile(D), const((1, D))] + [const(w.shape) for w in weights] + [
            const((1, LANES)), per_b(HIST_ROWS, POOL_WIDTH), per_b(*mk.shape[1:]), per_b(*mv.shape[1:]),
            const((POOL_GROUPS, POOL_GROUP_DIM, POOL_GROUP_DIM)), const((1, POOL_WIDTH))],
        out_specs=[tile(FOX_WIDTH), tile(FOX_WIDTH), tile(FOX_HEADS), per_b(POOL_HIST, POOL_WIDTH),
                   tile(POOL_WIDTH), tile(MEM_WIDTH)] + abc_specs,
        out_shape=[jax.ShapeDtypeStruct((B, S, FOX_WIDTH), F32),
                   jax.ShapeDtypeStruct((B, S, FOX_WIDTH), F32),
                   jax.ShapeDtypeStruct((B, S, FOX_HEADS), F32),
                   jax.ShapeDtypeStruct((B, POOL_HIST, POOL_WIDTH), F32),
                   jax.ShapeDtypeStruct((B, S, POOL_WIDTH), BF16),
                   jax.ShapeDtypeStruct((B, S, MEM_WIDTH), BF16)] + abc_shapes,
        scratch_shapes=[pltpu.VMEM((HIST_ROWS + tm, POOL_WIDTH), F32), pltpu.VMEM((1, LANES), F32)],
        compiler_params=pltpu.CompilerParams(dimension_semantics=("parallel", "arbitrary"),
                                             vmem_limit_bytes=VMEM_LIMIT),
        name="in_proj",
    )(x, g, *weights, bf_rep, hist, mk, mv, wpool, spool)


def _fox_prompt_kernel(qt_ref, k_ref, vt_ref, o_ref, m_ref, acc_ref, *, blk):
    i = pl.program_id(1)
    m_ref[...] = jnp.full_like(m_ref, -jnp.inf)
    acc_ref[...] = jnp.zeros_like(acc_ref)
    slabs = [slice(h * LANES, (h + 1) * LANES) for h in range(FOX_HEADS)]

    def scores(j, h):
        start = pl.multiple_of(j * blk, blk)
        return _dot(k_ref[0, pl.ds(start, blk), slabs[h]], qt_ref[0, 0, slabs[h], :])

    def accumulate(j, h, st, masked):
        if masked:
            st = jnp.where(_iota((blk, blk), 0) <= _iota((blk, blk), 1), st, -jnp.inf)
        m_prev = m_ref[h]
        m_new = jnp.maximum(m_prev, jnp.max(st, axis=0, keepdims=True))
        p = jnp.exp2(st - m_new).astype(BF16)
        pv = _dot(vt_ref[0, j, h * V_SLAB:(h + 1) * V_SLAB, :], p)
        acc_ref[h] = jnp.exp2(m_prev - m_new) * acc_ref[h] + pv
        m_ref[h] = m_new

    def run(blocks):
        work = [(j, h, masked) for j, masked in blocks for h in range(FOX_HEADS)]
        pending = [scores(j, h) for j, h, _ in work[:SCORES_AHEAD]]
        for n, (j, h, masked) in enumerate(work):
            if n + SCORES_AHEAD < len(work):
                pending.append(scores(*work[n + SCORES_AHEAD][:2]))
            accumulate(j, h, pending.pop(0), masked)

    shift = KEY_BLOCKS_PER_ITER.bit_length() - 1
    n_iter = i >> shift

    def body(jj, carry):
        run([(KEY_BLOCKS_PER_ITER * jj + r, False) for r in range(KEY_BLOCKS_PER_ITER)])
        return carry

    lax.fori_loop(0, n_iter, body, 0)
    for rem in range(KEY_BLOCKS_PER_ITER):
        @pl.when(i - (n_iter << shift) == rem)
        def _(rem=rem):
            run([(i - rem + r, False) for r in range(rem)] + [(i, True)])

    for pair in range(FOX_HEADS // 2):
        halves = []
        for par in (0, 1):
            acc = acc_ref[2 * pair + par]
            halves.append(acc[:FOX_HEAD_DIM] / acc[FOX_HEAD_DIM:FOX_HEAD_DIM + 1])
        o_ref[0, :, pair * LANES:(pair + 1) * LANES] = jnp.concatenate(halves, axis=0).T.astype(BF16)


def _fox_prompt(qt, ka, vt, *, blk):
    B, S, W = ka.shape
    nblk = S // blk
    return pl.pallas_call(
        functools.partial(_fox_prompt_kernel, blk=blk),
        grid=(B, nblk),
        in_specs=[pl.BlockSpec((1, 1, W, blk), lambda b, i: (b, i, 0, 0)),
                  pl.BlockSpec((1, S, W), lambda b, i: (b, 0, 0)),
                  pl.BlockSpec((1, nblk, FOX_HEADS * V_SLAB, blk), lambda b, i: (b, 0, 0, 0))],
        out_specs=pl.BlockSpec((1, blk, FOX_WIDTH), lambda b, i: (b, i, 0)),
        out_shape=jax.ShapeDtypeStruct((B, S, FOX_WIDTH), BF16),
        scratch_shapes=[pltpu.VMEM((FOX_HEADS, 1, blk), F32), pltpu.VMEM((FOX_HEADS, V_SLAB, blk), F32)],
        compiler_params=pltpu.CompilerParams(
            dimension_semantics=("parallel", "arbitrary"), vmem_limit_bytes=VMEM_LIMIT),
        name="fox_prompt",
    )(qt, ka, vt)


def _fox_sample_kernel(q_ref, kn_ref, vn_ref, lfn_ref, ck_ref, cv_ref, clf_ref, o_ref, *, chunk):
    T = q_ref.shape[1]
    P = clf_ref.shape[1]
    HT = FOX_HEADS * T
    t_shift = T.bit_length() - 1
    assert HT == LANES and 1 << t_shift == T

    R = chunk * FOX_HEADS
    n_chunks = P // chunk

    q_rep = jnp.concatenate([q_ref[0]] * FOX_HEADS, axis=0)
    row_head = _iota((HT, FOX_WIDTH), 0) >> t_shift
    q_bd = jnp.where(row_head == (_iota((HT, FOX_WIDTH), 1) >> 6), q_rep, 0.0)
    q_all = q_bd[:, :FOX_HEAD_DIM]
    for h in range(1, FOX_HEADS):
        q_all = q_all + q_bd[:, h * FOX_HEAD_DIM:(h + 1) * FOX_HEAD_DIM]
    q_all = q_all.astype(BF16)
    q_bd = q_bd.astype(BF16)

    pad_lanes = lambda a: jnp.concatenate([a, jnp.zeros((a.shape[0], LANES - a.shape[1]), F32)], axis=1)
    pad_rows = lambda a: jnp.concatenate([a, jnp.zeros((LANES - T, a.shape[1]), F32)], axis=0)
    replicate = lambda c: c + pltpu.roll(c, FOX_HEADS, axis=1) + pltpu.roll(c, 2 * FOX_HEADS, axis=1)
    carry = jnp.zeros((1, LANES), F32)
    piece_rows = []
    for ci in range(n_chunks):
        c, carry = _cumsum_rows(pad_lanes(clf_ref[0, ci * chunk:(ci + 1) * chunk, :]), carry)
        piece_rows.append(_bias_pieces(replicate(c)).T[:BIAS_LANES].astype(BF16))
    spread = (_iota((chunk, R), 1) >> 3 == _iota((chunk, R), 0)).astype(BF16)
    spread_out = _dot(jnp.concatenate(piece_rows, axis=0), spread)
    own_head = (_iota((BIAS_LANES, R), 0) & 7) == (_iota((BIAS_LANES, R), 1) & 7)
    bias = [jnp.sum(jnp.where(own_head, spread_out[ci * BIAS_LANES:(ci + 1) * BIAS_LANES], 0.0),
                    axis=0, keepdims=True) for ci in range(n_chunks)]

    own_key = (_iota((HT, R), 1) & 7) == (_iota((HT, R), 0) >> t_shift)
    tiles = lambda ref, ci: ref[ci * chunk:(ci + 1) * chunk].reshape(R, FOX_HEAD_DIM)
    s_parts = [jnp.where(own_key, _dot_nt(q_all, tiles(ck_ref, ci).astype(BF16)) + bias[ci], -jnp.inf)
               for ci in range(n_chunks)]
    c_new, _ = _cumsum_rows(pad_rows(pad_lanes(lfn_ref[0])), carry)
    e_lane = _iota((HT, LANES), 1)
    e_bd = jnp.where((e_lane < BIAS_LANES) & ((e_lane & 7) == (_iota((HT, LANES), 0) >> t_shift)),
                     1.0, 0.0).astype(BF16)
    s_new = (_dot_nt(q_bd, pad_rows(kn_ref[0]).astype(BF16))
             + _dot_nt(e_bd, _bias_pieces(replicate(c_new)).astype(BF16)))
    s_new = jnp.where(_iota((HT, LANES), 1) <= (_iota((HT, LANES), 0) & (T - 1)), s_new, -jnp.inf)

    m = jnp.max(s_new, axis=1, keepdims=True)
    for s in s_parts:
        m = jnp.maximum(m, jnp.max(s, axis=1, keepdims=True))
    p_new = jnp.exp2(s_new - m).astype(BF16)
    r_new = _dot(p_new, pad_rows(vn_ref[0]).astype(BF16))
    l_new = _dot(p_new, jnp.ones((LANES, LANES), BF16))
    acc = jnp.zeros((HT, LANES), F32)
    ones_half = jnp.ones((R, LANES - FOX_HEAD_DIM), BF16)
    for ci, s in enumerate(s_parts):
        vals = jnp.concatenate([tiles(cv_ref, ci).astype(BF16), ones_half], axis=1)
        acc = acc + _dot(jnp.exp2(s - m).astype(BF16), vals)
    lane = _iota((HT, LANES), 1)
    swapped = pltpu.roll(acc, FOX_HEAD_DIM, axis=1)
    l = l_new + jnp.where(lane < FOX_HEAD_DIM, swapped, acc)
    head = _iota((HT, LANES), 0) >> t_shift
    placed = jnp.where((head & 1) == 0, jnp.where(lane < FOX_HEAD_DIM, acc, 0.0),
                       jnp.where(lane < FOX_HEAD_DIM, 0.0, swapped))
    r_old = jnp.concatenate([jnp.where((head >> 1) == cb, placed, 0.0) for cb in range(FOX_WIDTH // LANES)],
                            axis=1)
    r = (r_new + r_old) / jnp.concatenate([l] * (FOX_WIDTH // LANES), axis=1)
    col_head = _iota((T, FOX_WIDTH), 1) >> 6
    o = jnp.zeros((T, FOX_WIDTH), F32)
    for h in range(FOX_HEADS):
        o = jnp.where(col_head == h, r[h * T:(h + 1) * T, :], o)
    o_ref[0] = o.astype(BF16)


def _fox_sample(q, k_new, v_new, lf_new, cache_k, cache_v, cache_lf, layer, *, chunk):
    B, T, _ = q.shape
    P = cache_lf.shape[2]
    new = lambda w: pl.BlockSpec((1, T, w), lambda b: (b, 0, 0))
    kv_spec = pl.BlockSpec((None, None, P, FOX_HEADS, FOX_HEAD_DIM), lambda b: (layer, b, 0, 0, 0))
    lf_spec = pl.BlockSpec((None, 1, P, FOX_HEADS), lambda b: (layer, b, 0, 0))
    return pl.pallas_call(
        functools.partial(_fox_sample_kernel, chunk=chunk),
        grid=(B,),
        in_specs=[new(FOX_WIDTH), new(FOX_WIDTH), new(FOX_WIDTH), new(FOX_HEADS), kv_spec, kv_spec, lf_spec],
        out_specs=new(FOX_WIDTH),
        out_shape=jax.ShapeDtypeStruct((B, T, FOX_WIDTH), BF16),
        compiler_params=pltpu.CompilerParams(dimension_semantics=("parallel",),
                                             vmem_limit_bytes=VMEM_LIMIT),
        name="fox_sample",
    )(q, k_new, v_new, lf_new, cache_k, cache_v, cache_lf)


def _merge_ffn_kernel(x_ref, ofox_ref, opool_ref, omem_ref, ga_ref, wg_ref, wbf_ref, wbp_ref, wbm_ref,
                      wout_ref, gm_ref, wup_ref, wdown_ref, gfin_ref, y_ref, *, streams, ff_chunk, final_norm):
    D = x_ref.shape[-1]
    rows = x_ref.shape[1] // streams
    groups = [slice(i * rows, (i + 1) * rows) for i in range(streams)]
    branches = ((ofox_ref, wbf_ref), (opool_ref, wbp_ref), (omem_ref, wbm_ref))

    xs = [x_ref[0, r, :] for r in groups]
    pre = []
    for x, r in zip(xs, groups):
        xn = _rms(x, ga_ref[...]).astype(BF16)
        pre.append([(_dot(xn, wg_ref[:, bi * D:(bi + 1) * D]), _dot(o_ref[0, r, :], w_ref[...]))
                    for bi, (o_ref, w_ref) in enumerate(branches)])
    hs = []
    for x, terms in zip(xs, pre):
        (g0, v0), (g1, v1), (g2, v2) = terms
        m = jax.nn.sigmoid(g0) * v0 + jax.nn.sigmoid(g1) * v1 + jax.nn.sigmoid(g2) * v2
        hs.append(x + _dot(m.astype(BF16), wout_ref[...]))
    hns = [_rms(h, gm_ref[...]).astype(BF16) for h in hs]
    accs = list(hs)
    for c in range(wup_ref.shape[1] // ff_chunk):
        cols = slice(c * ff_chunk, (c + 1) * ff_chunk)
        ups = [jnp.maximum(_dot(hn, wup_ref[:, cols]), 0.0) for hn in hns]
        accs = [acc + _dot((a * a).astype(BF16), wdown_ref[cols, :]) for acc, a in zip(accs, ups)]
    for acc, r in zip(accs, groups):
        y_ref[0, r, :] = _rms(acc, gfin_ref[...]) if final_norm else acc


def _merge_ffn(x, ofox, opool, omem, ga, wg, wbf, wbp, wbm, wout, gm, wup, wdown, gfin, *, tm, streams,
               ff_chunk, final_norm):
    B, S, D = x.shape
    tile = lambda w: pl.BlockSpec((1, tm, w), lambda b, t: (b, t, 0))
    const = lambda a: pl.BlockSpec(a.shape, lambda b, t: (0,) * a.ndim, pipeline_mode=pl.Buffered(1))
    consts = (ga, wg, wbf, wbp, wbm, wout, gm, wup, wdown, gfin)
    return pl.pallas_call(
        functools.partial(_merge_ffn_kernel, streams=streams, ff_chunk=ff_chunk, final_norm=final_norm),
        grid=(B, S // tm),
        in_specs=[tile(D), tile(FOX_WIDTH), tile(POOL_WIDTH), tile(MEM_WIDTH)] + [const(a) for a in consts],
        out_specs=tile(D),
        out_shape=jax.ShapeDtypeStruct((B, S, D), F32),
        compiler_params=pltpu.CompilerParams(dimension_semantics=("parallel", "parallel"),
                                             vmem_limit_bytes=VMEM_LIMIT),
        name="merge_ffn",
    )(x, ofox, opool, omem, *consts)


def _tile_rows(n, want):
    return want if n % want == 0 else n


def _repack_w_in(w_in):
    o_f = 3 * FOX_WIDTH
    o_u = o_f + FOX_HEADS
    o_qm = o_u + POOL_WIDTH
    o_g = o_qm + MEM_WIDTH
    b16 = lambda a: a.astype(BF16)
    wq, wk, wv = (w_in[:, i * FOX_WIDTH:(i + 1) * FOX_WIDTH] for i in range(3))
    wf_rep = _rep_lanes(w_in[:, o_f:o_u])
    weights = (b16(wk), b16(wv), b16(w_in[:, o_u:o_qm]), b16(w_in[:, o_qm:o_g]), b16(wf_rep),
               b16(wq).T, b16(wv).T)
    return weights, b16(w_in[:, o_g:])


def _rep_lanes(a):
    pad = jnp.zeros(a.shape[:-1] + (LANES - BIAS_LANES,), a.dtype)
    return jnp.concatenate([a] * BIAS_PIECES + [pad], axis=-1)


def kernel(x_prompt, x_sample, mem_prompt, cache_fox_k, cache_fox_v, cache_fox_logf, cache_pool,
           cache_mem_k, cache_mem_v, g_attn, w_in, b_f, w_pool, s_pool, g_mem, w_mkv,
           w_br_fox, w_br_pool, w_br_mem, w_out, g_mlp, w_up, w_down, g_final):
    depth = w_in.shape[0]
    B, S, D = x_prompt.shape
    BS, T, _ = x_sample.shape
    P = cache_fox_k.shape[2]
    M = mem_prompt.shape[1]
    tm_p = _tile_rows(S, 512)
    blk_p = _tile_rows(tm_p, 256)
    tm_s = _tile_rows(BS * T, 256)
    chunk_s = _tile_rows(P, 256)
    gfin = g_final.reshape(1, D)

    xp, xs = x_prompt, x_sample
    outs_p = [[] for _ in range(6)]
    outs_s = [[] for _ in range(4)]
    for l in range(depth):
        weights, wg = _repack_w_in(w_in[l])
        bf_rep = _rep_lanes(b_f[l]).reshape(1, LANES)
        g_a, g_m, g_f = g_attn[l].reshape(1, D), g_mem[l].reshape(1, D), g_mlp[l].reshape(1, D)
        wpool, spool = w_pool[l].astype(BF16), s_pool[l].reshape(1, POOL_WIDTH)
        wbf, wbp, wbm = (w.astype(BF16) for w in (w_br_fox[l], w_br_pool[l], w_br_mem[l]))
        wout, wup, wdown = (w.astype(BF16) for w in (w_out[l], w_up[l], w_down[l]))
        final = l == depth - 1

        mk, mv = _mem_kv(mem_prompt, g_m, w_mkv[l].astype(BF16))
        hist0 = jnp.zeros((B, HIST_ROWS, POOL_WIDTH), F32)
        k, v, logf, pnew, opool, omem, qt, ka, vt = _in_proj(
            xp, g_a, weights, bf_rep, hist0, mk, mv, wpool, spool, tm=tm_p, pos0=0, augment=True,
            blk=blk_p)
        ofox = _fox_prompt(qt, ka, vt, blk=blk_p)
        post = functools.partial(_merge_ffn, ga=g_a, wg=wg, wbf=wbf, wbp=wbp, wbm=wbm, wout=wout, gm=g_f,
                                 wup=wup, wdown=wdown, gfin=gfin, ff_chunk=1024, final_norm=final)
        xp = post(xp, ofox, opool, omem, tm=tm_p, streams=2 if tm_p % 512 == 0 else 1)
        for acc, val in zip(outs_p, (k.reshape(B, S, FOX_HEADS, FOX_HEAD_DIM),
                                     v.reshape(B, S, FOX_HEADS, FOX_HEAD_DIM), logf, pnew,
                                     mk.reshape(B, M, MEM_HEADS, MEM_HEAD_DIM),
                                     mv.reshape(B, M, MEM_HEADS, MEM_HEAD_DIM))):
            acc.append(val)

        hist = jnp.pad(cache_pool[l], ((0, 0), (HIST_ROWS - POOL_HIST, 0), (0, 0)))
        cmk = cache_mem_k[l].reshape(BS, M * MEM_HEADS, MEM_HEAD_DIM)
        cmv = cache_mem_v[l].reshape(BS, M * MEM_HEADS, MEM_HEAD_DIM)
        k, v, logf, pnew, opool, omem, q = _in_proj(
            xs, g_a, weights, bf_rep, hist, cmk, cmv, wpool, spool, tm=T, pos0=P, augment=False)
        ofox = _fox_sample(q, k, v, logf, cache_fox_k, cache_fox_v, cache_fox_logf, l, chunk=chunk_s)
        flat = lambda a: a.reshape(1, BS * T, a.shape[-1])
        xs = post(flat(xs), flat(ofox), flat(opool), flat(omem), tm=tm_s, streams=1).reshape(BS, T, D)
        for acc, val in zip(outs_s, (k.reshape(BS, T, FOX_HEADS, FOX_HEAD_DIM),
                                     v.reshape(BS, T, FOX_HEADS, FOX_HEAD_DIM), logf, pnew)):
            acc.append(val)

    stack = lambda o: o[0][None] if depth == 1 else jnp.stack(o)
    return (xp, xs, *(stack(o) for o in outs_p), *(stack(o) for o in outs_s))
```

```python
import functools
import math

import jax
import jax.numpy as jnp
from jax import lax
from jax.experimental import pallas as pl
from jax.experimental.pallas import tpu as pltpu

F32 = jnp.float32
BF16 = jnp.bfloat16

EPS = 1e-6
LANES = 128
FOX_HEADS = 8
FOX_HEAD_DIM = 64
FOX_WIDTH = FOX_HEADS * FOX_HEAD_DIM
POOL_WINDOWS = (2, 4, 8, 16)
POOL_GROUPS = 4
POOL_GROUP_DIM = 128
POOL_WIDTH = POOL_GROUPS * POOL_GROUP_DIM
POOL_HIST = 15
HIST_ROWS = 16
MEM_HEADS = 4
MEM_HEAD_DIM = 128
MEM_WIDTH = MEM_HEADS * MEM_HEAD_DIM
N_BRANCH = 3
LOG2E = math.log2(math.e)
BIAS_PIECES = 3
BIAS_LANES = BIAS_PIECES * FOX_HEADS
V_SLAB = 80
KEY_BLOCKS_PER_ITER = 4
SCORES_AHEAD = 5
CUMSUM_CHUNK = 128
VMEM_LIMIT = 56 * 1024 * 1024


def _dot(a, b):
    return jnp.dot(a, b, preferred_element_type=F32)


def _dot_nt(a, b):
    return lax.dot_general(a, b, (((1,), (1,)), ((), ())), preferred_element_type=F32)


def _rms(x, g):
    return x * lax.rsqrt(jnp.mean(x * x, axis=-1, keepdims=True) + EPS) * g


def _iota(shape, dim):
    return lax.broadcasted_iota(jnp.int32, shape, dim)


def _split3(x):
    p0 = x.astype(BF16)
    r1 = x - p0.astype(F32)
    p1 = r1.astype(BF16)
    p2 = (r1 - p1.astype(F32)).astype(BF16)
    return p0, p1, p2


def _cumsum_rows(x, carry):
    n = x.shape[0]
    ch = min(n, CUMSUM_CHUNK)
    tril = (_iota((ch, ch), 1) <= _iota((ch, ch), 0)).astype(BF16)
    outs = []
    for c0 in range(0, n, ch):
        r = _dot(tril, jnp.concatenate(_split3(x[c0:c0 + ch]), axis=1))
        c = r[:, :LANES] + r[:, LANES:2 * LANES] + r[:, 2 * LANES:] + carry
        carry = c[ch - 1:ch]
        outs.append(c)
    return (outs[0] if len(outs) == 1 else jnp.concatenate(outs, axis=0)), carry


def _bias_pieces(c):
    b = c * (-LOG2E)
    p0 = b.astype(BF16).astype(F32)
    r1 = b - p0
    p1 = r1.astype(BF16).astype(F32)
    p2 = (r1 - p1).astype(BF16).astype(F32)
    grp = _iota(c.shape, 1) >> 3
    return jnp.where(grp == 0, p0, jnp.where(grp == 1, p1, jnp.where(grp == 2, p2, 0.0)))


def _mem_kv_kernel(mem_ref, g_ref, w_ref, mk_ref, mv_ref):
    M = mem_ref.shape[1]
    xn = _rms(mem_ref[0], g_ref[...]).astype(BF16)
    for out_ref, off in ((mk_ref, 0), (mv_ref, MEM_WIDTH)):
        kv = _dot(xn, w_ref[:, off:off + MEM_WIDTH])
        for h in range(MEM_HEADS):
            out_ref[0, pl.ds(h, M, stride=MEM_HEADS), :] = kv[:, h * MEM_HEAD_DIM:(h + 1) * MEM_HEAD_DIM]


def _mem_kv(mem, g, w_mkv):
    B, M, D = mem.shape
    return pl.pallas_call(
        _mem_kv_kernel,
        grid=(B,),
        in_specs=[pl.BlockSpec((1, M, D), lambda b: (b, 0, 0)), pl.BlockSpec((1, D), lambda b: (0, 0)),
                  pl.BlockSpec((D, 2 * MEM_WIDTH), lambda b: (0, 0))],
        out_specs=[pl.BlockSpec((1, M * MEM_HEADS, MEM_HEAD_DIM), lambda b: (b, 0, 0))] * 2,
        out_shape=[jax.ShapeDtypeStruct((B, M * MEM_HEADS, MEM_HEAD_DIM), F32)] * 2,
        compiler_params=pltpu.CompilerParams(dimension_semantics=("parallel",),
                                             vmem_limit_bytes=VMEM_LIMIT),
        name="mem_kv",
    )(mem, g, w_mkv)


def _in_proj_kernel(x_ref, g_ref, wk_ref, wv_ref, wu_ref, wqm_ref, wf_ref, wqt_ref, wvt_ref, bf_ref, hist_ref,
                    mk_ref, mv_ref, wpool_ref, spool_ref,
                    k_ref, v_ref, logf_ref, pnew_ref, opool_ref, omem_ref, *rest, tm, pos0, augment, blk):
    extra, (uext_ref, carry_ref) = rest[:-2], rest[-2:]
    t = pl.program_id(1)
    last = pl.num_programs(1) - 1
    q_scale = FOX_HEAD_DIM ** -0.5 * LOG2E

    @pl.when(t == 0)
    def _():
        uext_ref[0:HIST_ROWS, :] = hist_ref[0]
        carry_ref[...] = jnp.zeros_like(carry_ref)

    xn = _rms(x_ref[0], g_ref[...]).astype(BF16)
    lane = _iota((tm, LANES), 1)
    mem_cols = [slice(h * MEM_HEAD_DIM, (h + 1) * MEM_HEAD_DIM) for h in range(MEM_HEADS)]

    zf = _dot(xn, wf_ref[...]) + bf_ref[...]
    k = _dot(xn, wk_ref[...])
    v = _dot(xn, wv_ref[...])
    u = _dot(xn, wu_ref[...])
    qm = _dot(xn, wqm_ref[...]).astype(BF16)
    if augment:
        qt_all = (_dot_nt(wqt_ref[...], xn) * q_scale).astype(BF16)
        vt_all = _dot_nt(wvt_ref[...], xn).astype(BF16)
    else:
        q = _dot_nt(xn, wqt_ref[...]) * q_scale
    n_mem = mk_ref.shape[1] // MEM_HEADS
    mem_head = lambda ref, h: ref[0, pl.ds(h, n_mem, stride=MEM_HEADS), :].astype(BF16)
    mem_s = [_dot_nt(qm[:, cols], mem_head(mk_ref, h)) * MEM_HEAD_DIM ** -0.5
             for h, cols in enumerate(mem_cols)]

    k_ref[0] = k
    v_ref[0] = v
    logf = jnp.minimum(zf, 0.0) - jnp.log1p(jnp.exp(-jnp.abs(zf)))
    logf_ref[0] = logf[:, :FOX_HEADS]

    if augment:
        qt_ref, ka_ref, vt_ref = extra
        c, carry_ref[...] = _cumsum_rows(logf, carry_ref[...])
        bias = pltpu.roll(_bias_pieces(c), FOX_HEAD_DIM, axis=1)
        in_bias = (lane >= FOX_HEAD_DIM) & (lane < FOX_HEAD_DIM + BIAS_LANES)
        for pair in range(FOX_HEADS // 2):
            even = k[:, pair * LANES:(pair + 1) * LANES]
            odd = pltpu.roll(even, FOX_HEAD_DIM, axis=1)
            for par, data in ((0, even), (1, odd)):
                h = 2 * pair + par
                tail = jnp.where(in_bias & ((lane & 7) == h), bias, 0.0)
                ka_ref[0, :, h * LANES:(h + 1) * LANES] = jnp.where(lane < FOX_HEAD_DIM, data, tail).astype(BF16)
        q_tail = jnp.where(_iota((LANES - FOX_HEAD_DIM, blk), 0) < BIAS_LANES, 1.0, 0.0).astype(BF16)
        v_tail = jnp.where(_iota((V_SLAB - FOX_HEAD_DIM, blk), 0) == 0, 1.0, 0.0).astype(BF16)
        for h in range(FOX_HEADS):
            qt = qt_all[h * FOX_HEAD_DIM:(h + 1) * FOX_HEAD_DIM]
            vt = vt_all[h * FOX_HEAD_DIM:(h + 1) * FOX_HEAD_DIM]
            for j in range(tm // blk):
                qt_ref[0, j, h * LANES:h * LANES + FOX_HEAD_DIM, :] = qt[:, j * blk:(j + 1) * blk]
                qt_ref[0, j, h * LANES + FOX_HEAD_DIM:(h + 1) * LANES, :] = q_tail
                vt_ref[0, j, h * V_SLAB:h * V_SLAB + FOX_HEAD_DIM, :] = vt[:, j * blk:(j + 1) * blk]
                vt_ref[0, j, h * V_SLAB + FOX_HEAD_DIM:(h + 1) * V_SLAB, :] = v_tail
    else:
        extra[0][0] = q

    uext_ref[HIST_ROWS:HIST_ROWS + tm, :] = u
    pos = pos0 + t * tm + _iota((tm, 1), 0)
    for gi, w in enumerate(POOL_WINDOWS):
        cols = slice(gi * POOL_GROUP_DIM, (gi + 1) * POOL_GROUP_DIM)
        sw = uext_ref[:, cols]
        shift = 1
        while shift < w:
            sw = sw + pltpu.roll(sw, shift, axis=0)
            shift *= 2
        cnt = jnp.minimum(w, pos + 1).astype(F32)
        pooled = sw[HIST_ROWS:, :] / cnt - u[:, cols]
        mixed = _dot(pooled.astype(BF16), wpool_ref[gi]) * spool_ref[:, cols]
        opool_ref[0, :, cols] = mixed.astype(BF16)

    @pl.when(t == last)
    def _():
        pnew_ref[0] = uext_ref[tm + 1:tm + HIST_ROWS, :]

    uext_ref[0:HIST_ROWS, :] = uext_ref[tm:tm + HIST_ROWS, :]

    for h, (s, cols) in enumerate(zip(mem_s, mem_cols)):
        e = jnp.exp(s - jnp.max(s, axis=-1, keepdims=True))
        o = _dot(e.astype(BF16), mem_head(mv_ref, h)) / jnp.sum(e, axis=-1, keepdims=True)
        omem_ref[0, :, cols] = o.astype(BF16)


def _in_proj(x, g, weights, bf_rep, hist, mk, mv, wpool, spool, *, tm, pos0, augment, blk=None):
    B, S, D = x.shape
    nt = S // tm
    tile = lambda w: pl.BlockSpec((1, tm, w), lambda b, t: (b, t, 0))
    per_b = lambda r, w: pl.BlockSpec((1, r, w), lambda b, t: (b, 0, 0))
    const = lambda shape: pl.BlockSpec(shape, lambda b, t: (0,) * len(shape))
    if augment:
        W = FOX_HEADS * LANES
        t_tile = lambda rows: pl.BlockSpec((1, tm // blk, rows, blk), lambda b, t: (b, t, 0, 0))
        abc_specs = [t_tile(W), tile(W), t_tile(FOX_HEADS * V_SLAB)]
        abc_shapes = [jax.ShapeDtypeStruct((B, S // blk, W, blk), BF16),
                      jax.ShapeDtypeStruct((B, S, W), BF16),
                      jax.ShapeDtypeStruct((B, S // blk, FOX_HEADS * V_SLAB, blk), BF16)]
    else:
        abc_specs = [tile(FOX_WIDTH)]
        abc_shapes = [jax.ShapeDtypeStruct((B, S, FOX_WIDTH), F32)]
    return pl.pallas_call(
        functools.partial(_in_proj_kernel, tm=tm, pos0=pos0, augment=augment, blk=blk),
        grid=(B, nt),
        in_specs=[tile(D), const((1, D))] + [const(w.shape) for w in weights] + [
            const((1, LANES)), per_b(HIST_ROWS, POOL_WIDTH), per_b(*mk.shape[1:]), per_b(*mv.shape[1:]),
            const((POOL_GROUPS, POOL_GROUP_DIM, POOL_GROUP_DIM)), const((1, POOL_WIDTH))],
        out_specs=[tile(FOX_WIDTH), tile(FOX_WIDTH), tile(FOX_HEADS), per_b(POOL_HIST, POOL_WIDTH),
                   tile(POOL_WIDTH), tile(MEM_WIDTH)] + abc_specs,
        out_shape=[jax.ShapeDtypeStruct((B, S, FOX_WIDTH), F32),
                   jax.ShapeDtypeStruct((B, S, FOX_WIDTH), F32),
                   jax.ShapeDtypeStruct((B, S, FOX_HEADS), F32),
                   jax.ShapeDtypeStruct((B, POOL_HIST, POOL_WIDTH), F32),
                   jax.ShapeDtypeStruct((B, S, POOL_WIDTH), BF16),
                   jax.ShapeDtypeStruct((B, S, MEM_WIDTH), BF16)] + abc_shapes,
        scratch_shapes=[pltpu.VMEM((HIST_ROWS + tm, POOL_WIDTH), F32), pltpu.VMEM((1, LANES), F32)],
        compiler_params=pltpu.CompilerParams(dimension_semantics=("parallel", "arbitrary"),
                                             vmem_limit_bytes=VMEM_LIMIT),
        name="in_proj",
    )(x, g, *weights, bf_rep, hist, mk, mv, wpool, spool)


def _fox_prompt_kernel(qt_ref, k_ref, vt_ref, o_ref, m_ref, acc_ref, *, blk):
    i = pl.program_id(1)
    m_ref[...] = jnp.full_like(m_ref, -jnp.inf)
    acc_ref[...] = jnp.zeros_like(acc_ref)
    slabs = [slice(h * LANES, (h + 1) * LANES) for h in range(FOX_HEADS)]

    def scores(j, h):
        start = pl.multiple_of(j * blk, blk)
        return _dot(k_ref[0, pl.ds(start, blk), slabs[h]], qt_ref[0, 0, slabs[h], :])

    def accumulate(j, h, st, masked):
        if masked:
            st = jnp.where(_iota((blk, blk), 0) <= _iota((blk, blk), 1), st, -jnp.inf)
        m_prev = m_ref[h]
        m_new = jnp.maximum(m_prev, jnp.max(st, axis=0, keepdims=True))
        p = jnp.exp2(st - m_new).astype(BF16)
        pv = _dot(vt_ref[0, j, h * V_SLAB:(h + 1) * V_SLAB, :], p)
        acc_ref[h] = jnp.exp2(m_prev - m_new) * acc_ref[h] + pv
        m_ref[h] = m_new

    def run(blocks):
        work = [(j, h, masked) for j, masked in blocks for h in range(FOX_HEADS)]
        pending = [scores(j, h) for j, h, _ in work[:SCORES_AHEAD]]
        for n, (j, h, masked) in enumerate(work):
            if n + SCORES_AHEAD < len(work):
                pending.append(scores(*work[n + SCORES_AHEAD][:2]))
            accumulate(j, h, pending.pop(0), masked)

    shift = KEY_BLOCKS_PER_ITER.bit_length() - 1
    n_iter = i >> shift

    def body(jj, carry):
        run([(KEY_BLOCKS_PER_ITER * jj + r, False) for r in range(KEY_BLOCKS_PER_ITER)])
        return carry

    lax.fori_loop(0, n_iter, body, 0)
    for rem in range(KEY_BLOCKS_PER_ITER):
        @pl.when(i - (n_iter << shift) == rem)
        def _(rem=rem):
            run([(i - rem + r, False) for r in range(rem)] + [(i, True)])

    for pair in range(FOX_HEADS // 2):
        halves = []
        for par in (0, 1):
            acc = acc_ref[2 * pair + par]
            halves.append(acc[:FOX_HEAD_DIM] / acc[FOX_HEAD_DIM:FOX_HEAD_DIM + 1])
        o_ref[0, :, pair * LANES:(pair + 1) * LANES] = jnp.concatenate(halves, axis=0).T.astype(BF16)


def _fox_prompt(qt, ka, vt, *, blk):
    B, S, W = ka.shape
    nblk = S // blk
    return pl.pallas_call(
        functools.partial(_fox_prompt_kernel, blk=blk),
        grid=(B, nblk),
        in_specs=[pl.BlockSpec((1, 1, W, blk), lambda b, i: (b, i, 0, 0)),
                  pl.BlockSpec((1, S, W), lambda b, i: (b, 0, 0)),
                  pl.BlockSpec((1, nblk, FOX_HEADS * V_SLAB, blk), lambda b, i: (b, 0, 0, 0))],
        out_specs=pl.BlockSpec((1, blk, FOX_WIDTH), lambda b, i: (b, i, 0)),
        out_shape=jax.ShapeDtypeStruct((B, S, FOX_WIDTH), BF16),
        scratch_shapes=[pltpu.VMEM((FOX_HEADS, 1, blk), F32), pltpu.VMEM((FOX_HEADS, V_SLAB, blk), F32)],
        compiler_params=pltpu.CompilerParams(
            dimension_semantics=("parallel", "arbitrary"), vmem_limit_bytes=VMEM_LIMIT),
        name="fox_prompt",
    )(qt, ka, vt)


def _bias_piece_rows(c):
    b = c * (-LOG2E)
    p0 = b.astype(BF16).astype(F32)
    r1 = b - p0
    p1 = r1.astype(BF16).astype(F32)
    p2 = (r1 - p1).astype(BF16).astype(F32)
    return jnp.concatenate([p0, p1, p2], axis=0)


def _fox_sample_kernel(q_ref, kn_ref, vn_ref, lfn_ref, kt_ref, vt_ref, lft_ref, o_ref, *, chunk):
    T = q_ref.shape[1]
    P = lft_ref.shape[2]
    HT = FOX_HEADS * T
    t_shift = T.bit_length() - 1
    assert HT == LANES and 1 << t_shift == T

    q_rep = jnp.concatenate([q_ref[0]] * FOX_HEADS, axis=0)
    row_head = _iota((HT, FOX_WIDTH), 0) >> t_shift
    q_bd = jnp.where(row_head == (_iota((HT, FOX_WIDTH), 1) >> 6), q_rep, 0.0).astype(BF16)
    e_lane = _iota((HT, LANES), 1)
    e_bd = jnp.where((e_lane < BIAS_LANES) & ((e_lane & 7) == (_iota((HT, LANES), 0) >> t_shift)),
                     1.0, 0.0).astype(BF16)

    def bias_operand(c):
        rows = _bias_piece_rows(c)
        return jnp.concatenate([rows, jnp.zeros((LANES - BIAS_LANES, c.shape[1]), F32)], axis=0).astype(BF16)

    def running_sum(lf, carry, upper):
        pieces = jnp.concatenate(list(_split3(lf)) + [jnp.zeros(lf.shape, BF16)], axis=0)
        loc = _dot(pieces, upper)
        return loc[0:8] + loc[8:16] + loc[16:24] + carry

    upper = (_iota((chunk, chunk), 0) <= _iota((chunk, chunk), 1)).astype(BF16)
    carry = jnp.zeros((FOX_HEADS, 1), F32)
    c_parts = []
    for ci in range(P // chunk):
        c = running_sum(lft_ref[0, :, ci * chunk:(ci + 1) * chunk], carry, upper)
        carry = c[:, chunk - 1:chunk]
        c_parts.append(c)
    kt = kt_ref[0].reshape(FOX_WIDTH, P).astype(BF16)
    s_old = _dot(q_bd, kt) + _dot(e_bd, bias_operand(jnp.concatenate(c_parts, axis=1)))

    pad_lanes = lambda a: jnp.concatenate([a, jnp.zeros((a.shape[0], LANES - a.shape[1]), F32)], axis=1)
    pad_rows = lambda a: jnp.concatenate([a, jnp.zeros((LANES - T, a.shape[1]), F32)], axis=0)
    upper_new = (_iota((LANES, LANES), 0) <= _iota((LANES, LANES), 1)).astype(BF16)
    c_new = running_sum(pad_lanes(lfn_ref[0]), carry, upper_new)
    s_new = _dot_nt(q_bd, pad_rows(kn_ref[0]).astype(BF16)) + _dot(e_bd, bias_operand(c_new))
    s_new = jnp.where(_iota((HT, LANES), 1) <= (_iota((HT, LANES), 0) & (T - 1)), s_new, -jnp.inf)

    m = jnp.maximum(jnp.max(s_old, axis=1, keepdims=True), jnp.max(s_new, axis=1, keepdims=True))
    p_old = jnp.exp2(s_old - m).astype(BF16)
    p_new = jnp.exp2(s_new - m).astype(BF16)
    l = (jnp.sum(p_old.astype(F32), axis=1, keepdims=True)
         + jnp.sum(p_new.astype(F32), axis=1, keepdims=True))
    r = (_dot_nt(p_old, vt_ref[0].reshape(FOX_WIDTH, P).astype(BF16))
         + _dot(p_new, pad_rows(vn_ref[0]).astype(BF16))) / l
    col_head = _iota((T, FOX_WIDTH), 1) >> 6
    o = jnp.zeros((T, FOX_WIDTH), F32)
    for h in range(FOX_HEADS):
        o = jnp.where(col_head == h, r[h * T:(h + 1) * T, :], o)
    o_ref[0] = o.astype(BF16)


def _fox_sample(q, k_new, v_new, lf_new_t, cache_kt, cache_vt, cache_lft, *, chunk):
    B, T, _ = q.shape
    P = cache_lft.shape[2]
    new = lambda w: pl.BlockSpec((1, T, w), lambda b: (b, 0, 0))
    kv_spec = pl.BlockSpec((1, FOX_HEADS, FOX_HEAD_DIM, P), lambda b: (b, 0, 0, 0))
    return pl.pallas_call(
        functools.partial(_fox_sample_kernel, chunk=chunk),
        grid=(B,),
        in_specs=[new(FOX_WIDTH), new(FOX_WIDTH), new(FOX_WIDTH),
                  pl.BlockSpec((1, FOX_HEADS, T), lambda b: (b, 0, 0)), kv_spec, kv_spec,
                  pl.BlockSpec((1, FOX_HEADS, P), lambda b: (b, 0, 0))],
        out_specs=new(FOX_WIDTH),
        out_shape=jax.ShapeDtypeStruct((B, T, FOX_WIDTH), BF16),
        compiler_params=pltpu.CompilerParams(dimension_semantics=("parallel",),
                                             vmem_limit_bytes=VMEM_LIMIT),
        name="fox_sample",
    )(q, k_new, v_new, lf_new_t, cache_kt, cache_vt, cache_lft)


def _merge_ffn_kernel(x_ref, ofox_ref, opool_ref, omem_ref, ga_ref, wg_ref, wbf_ref, wbp_ref, wbm_ref,
                      wout_ref, gm_ref, wup_ref, wdown_ref, gfin_ref, y_ref, *, streams, ff_chunk, final_norm):
    D = x_ref.shape[-1]
    rows = x_ref.shape[1] // streams
    groups = [slice(i * rows, (i + 1) * rows) for i in range(streams)]
    branches = ((ofox_ref, wbf_ref), (opool_ref, wbp_ref), (omem_ref, wbm_ref))

    xs = [x_ref[0, r, :] for r in groups]
    pre = []
    for x, r in zip(xs, groups):
        xn = _rms(x, ga_ref[...]).astype(BF16)
        pre.append([(_dot(xn, wg_ref[:, bi * D:(bi + 1) * D]), _dot(o_ref[0, r, :], w_ref[...]))
                    for bi, (o_ref, w_ref) in enumerate(branches)])
    hs = []
    for x, terms in zip(xs, pre):
        (g0, v0), (g1, v1), (g2, v2) = terms
        m = jax.nn.sigmoid(g0) * v0 + jax.nn.sigmoid(g1) * v1 + jax.nn.sigmoid(g2) * v2
        hs.append(x + _dot(m.astype(BF16), wout_ref[...]))
    hns = [_rms(h, gm_ref[...]).astype(BF16) for h in hs]
    accs = list(hs)
    for c in range(wup_ref.shape[1] // ff_chunk):
        cols = slice(c * ff_chunk, (c + 1) * ff_chunk)
        ups = [jnp.maximum(_dot(hn, wup_ref[:, cols]), 0.0) for hn in hns]
        accs = [acc + _dot((a * a).astype(BF16), wdown_ref[cols, :]) for acc, a in zip(accs, ups)]
    for acc, r in zip(accs, groups):
        y_ref[0, r, :] = _rms(acc, gfin_ref[...]) if final_norm else acc


def _merge_ffn(x, ofox, opool, omem, ga, wg, wbf, wbp, wbm, wout, gm, wup, wdown, gfin, *, tm, streams,
               ff_chunk, final_norm):
    B, S, D = x.shape
    tile = lambda w: pl.BlockSpec((1, tm, w), lambda b, t: (b, t, 0))
    const = lambda a: pl.BlockSpec(a.shape, lambda b, t: (0,) * a.ndim, pipeline_mode=pl.Buffered(1))
    consts = (ga, wg, wbf, wbp, wbm, wout, gm, wup, wdown, gfin)
    return pl.pallas_call(
        functools.partial(_merge_ffn_kernel, streams=streams, ff_chunk=ff_chunk, final_norm=final_norm),
        grid=(B, S // tm),
        in_specs=[tile(D), tile(FOX_WIDTH), tile(POOL_WIDTH), tile(MEM_WIDTH)] + [const(a) for a in consts],
        out_specs=tile(D),
        out_shape=jax.ShapeDtypeStruct((B, S, D), F32),
        compiler_params=pltpu.CompilerParams(dimension_semantics=("parallel", "parallel"),
                                             vmem_limit_bytes=VMEM_LIMIT),
        name="merge_ffn",
    )(x, ofox, opool, omem, *consts)


def _tile_rows(n, want):
    return want if n % want == 0 else n


def _repack_w_in(w_in):
    o_f = 3 * FOX_WIDTH
    o_u = o_f + FOX_HEADS
    o_qm = o_u + POOL_WIDTH
    o_g = o_qm + MEM_WIDTH
    b16 = lambda a: a.astype(BF16)
    wq, wk, wv = (w_in[:, i * FOX_WIDTH:(i + 1) * FOX_WIDTH] for i in range(3))
    wf_rep = _rep_lanes(w_in[:, o_f:o_u])
    weights = (b16(wk), b16(wv), b16(w_in[:, o_u:o_qm]), b16(w_in[:, o_qm:o_g]), b16(wf_rep),
               b16(wq).T, b16(wv).T)
    return weights, b16(w_in[:, o_g:])


def _rep_lanes(a):
    pad = jnp.zeros(a.shape[:-1] + (LANES - BIAS_LANES,), a.dtype)
    return jnp.concatenate([a] * BIAS_PIECES + [pad], axis=-1)


def kernel(x_prompt, x_sample, mem_prompt, cache_fox_k, cache_fox_v, cache_fox_logf, cache_pool,
           cache_mem_k, cache_mem_v, g_attn, w_in, b_f, w_pool, s_pool, g_mem, w_mkv,
           w_br_fox, w_br_pool, w_br_mem, w_out, g_mlp, w_up, w_down, g_final):
    depth = w_in.shape[0]
    B, S, D = x_prompt.shape
    BS, T, _ = x_sample.shape
    P = cache_fox_k.shape[2]
    M = mem_prompt.shape[1]
    tm_p = _tile_rows(S, 512)
    blk_p = _tile_rows(tm_p, 256)
    tm_s = _tile_rows(BS * T, 256)
    chunk_s = _tile_rows(P, 256)
    gfin = g_final.reshape(1, D)

    xp, xs = x_prompt, x_sample
    outs_p = [[] for _ in range(6)]
    outs_s = [[] for _ in range(4)]
    for l in range(depth):
        weights, wg = _repack_w_in(w_in[l])
        bf_rep = _rep_lanes(b_f[l]).reshape(1, LANES)
        g_a, g_m, g_f = g_attn[l].reshape(1, D), g_mem[l].reshape(1, D), g_mlp[l].reshape(1, D)
        wpool, spool = w_pool[l].astype(BF16), s_pool[l].reshape(1, POOL_WIDTH)
        wbf, wbp, wbm = (w.astype(BF16) for w in (w_br_fox[l], w_br_pool[l], w_br_mem[l]))
        wout, wup, wdown = (w.astype(BF16) for w in (w_out[l], w_up[l], w_down[l]))
        final = l == depth - 1

        mk, mv = _mem_kv(mem_prompt, g_m, w_mkv[l].astype(BF16))
        hist0 = jnp.zeros((B, HIST_ROWS, POOL_WIDTH), F32)
        k, v, logf, pnew, opool, omem, qt, ka, vt = _in_proj(
            xp, g_a, weights, bf_rep, hist0, mk, mv, wpool, spool, tm=tm_p, pos0=0, augment=True,
            blk=blk_p)
        ofox = _fox_prompt(qt, ka, vt, blk=blk_p)
        post = functools.partial(_merge_ffn, ga=g_a, wg=wg, wbf=wbf, wbp=wbp, wbm=wbm, wout=wout, gm=g_f,
                                 wup=wup, wdown=wdown, gfin=gfin, ff_chunk=1024, final_norm=final)
        xp = post(xp, ofox, opool, omem, tm=tm_p, streams=2 if tm_p % 512 == 0 else 1)
        for acc, val in zip(outs_p, (k.reshape(B, S, FOX_HEADS, FOX_HEAD_DIM),
                                     v.reshape(B, S, FOX_HEADS, FOX_HEAD_DIM), logf, pnew,
                                     mk.reshape(B, M, MEM_HEADS, MEM_HEAD_DIM),
                                     mv.reshape(B, M, MEM_HEADS, MEM_HEAD_DIM))):
            acc.append(val)

        hist = jnp.pad(cache_pool[l], ((0, 0), (HIST_ROWS - POOL_HIST, 0), (0, 0)))
        cmk = cache_mem_k[l].reshape(BS, M * MEM_HEADS, MEM_HEAD_DIM)
        cmv = cache_mem_v[l].reshape(BS, M * MEM_HEADS, MEM_HEAD_DIM)
        k, v, logf, pnew, opool, omem, q = _in_proj(
            xs, g_a, weights, bf_rep, hist, cmk, cmv, wpool, spool, tm=T, pos0=P, augment=False)
        ofox = _fox_sample(q, k, v, jnp.transpose(logf, (0, 2, 1)),
                           jnp.transpose(cache_fox_k[l], (0, 2, 3, 1)), jnp.transpose(cache_fox_v[l], (0, 2, 3, 1)),
                           jnp.transpose(cache_fox_logf[l], (0, 2, 1)), chunk=chunk_s)
        flat = lambda a: a.reshape(1, BS * T, a.shape[-1])
        xs = post(flat(xs), flat(ofox), flat(opool), flat(omem), tm=tm_s, streams=1).reshape(BS, T, D)
        for acc, val in zip(outs_s, (k.reshape(BS, T, FOX_HEADS, FOX_HEAD_DIM),
                                     v.reshape(BS, T, FOX_HEADS, FOX_HEAD_DIM), logf, pnew)):
            acc.append(val)

    stack = lambda o: o[0][None] if depth == 1 else jnp.stack(o)
    return (xp, xs, *(stack(o) for o in outs_p), *(stack(o) for o in outs_s))
```

```python
import functools
import math

import jax
import jax.numpy as jnp
from jax import lax
from jax.experimental import pallas as pl
from jax.experimental.pallas import tpu as pltpu

F32 = jnp.float32
BF16 = jnp.bfloat16

EPS = 1e-6
LANES = 128
FOX_HEADS = 8
FOX_HEAD_DIM = 64
FOX_WIDTH = FOX_HEADS * FOX_HEAD_DIM
POOL_WINDOWS = (2, 4, 8, 16)
POOL_GROUPS = 4
POOL_GROUP_DIM = 128
POOL_WIDTH = POOL_GROUPS * POOL_GROUP_DIM
POOL_HIST = 15
HIST_ROWS = 16
MEM_HEADS = 4
MEM_HEAD_DIM = 128
MEM_WIDTH = MEM_HEADS * MEM_HEAD_DIM
N_BRANCH = 3
LOG2E = math.log2(math.e)
BIAS_PIECES = 3
BIAS_LANES = BIAS_PIECES * FOX_HEADS
V_SLAB = 80
KEY_BLOCKS_PER_ITER = 4
KEY_BLOCKS_PER_ITEM = 1
SCORES_AHEAD = 6
CUMSUM_CHUNK = 128
VMEM_LIMIT = 56 * 1024 * 1024


def _dot(a, b):
    return jnp.dot(a, b, preferred_element_type=F32)


def _dot_nt(a, b):
    return lax.dot_general(a, b, (((1,), (1,)), ((), ())), preferred_element_type=F32)


def _rms(x, g):
    return x * lax.rsqrt(jnp.mean(x * x, axis=-1, keepdims=True) + EPS) * g


def _iota(shape, dim):
    return lax.broadcasted_iota(jnp.int32, shape, dim)


def _split3(x):
    p0 = x.astype(BF16)
    r1 = x - p0.astype(F32)
    p1 = r1.astype(BF16)
    p2 = (r1 - p1.astype(F32)).astype(BF16)
    return p0, p1, p2


def _cumsum_rows(x, carry):
    n = x.shape[0]
    ch = min(n, CUMSUM_CHUNK)
    tril = (_iota((ch, ch), 1) <= _iota((ch, ch), 0)).astype(BF16)
    outs = []
    for c0 in range(0, n, ch):
        r = _dot(tril, jnp.concatenate(_split3(x[c0:c0 + ch]), axis=1))
        c = r[:, :LANES] + r[:, LANES:2 * LANES] + r[:, 2 * LANES:] + carry
        carry = c[ch - 1:ch]
        outs.append(c)
    return (outs[0] if len(outs) == 1 else jnp.concatenate(outs, axis=0)), carry


def _bias_pieces(c):
    b = c * (-LOG2E)
    p0 = b.astype(BF16).astype(F32)
    r1 = b - p0
    p1 = r1.astype(BF16).astype(F32)
    p2 = (r1 - p1).astype(BF16).astype(F32)
    grp = _iota(c.shape, 1) >> 3
    return jnp.where(grp == 0, p0, jnp.where(grp == 1, p1, jnp.where(grp == 2, p2, 0.0)))


def _mem_kv_kernel(mem_ref, g_ref, w_ref, mk_ref, mv_ref):
    M = mem_ref.shape[1]
    xn = _rms(mem_ref[0], g_ref[...]).astype(BF16)
    for out_ref, off in ((mk_ref, 0), (mv_ref, MEM_WIDTH)):
        kv = _dot(xn, w_ref[:, off:off + MEM_WIDTH])
        for h in range(MEM_HEADS):
            out_ref[0, pl.ds(h, M, stride=MEM_HEADS), :] = kv[:, h * MEM_HEAD_DIM:(h + 1) * MEM_HEAD_DIM]


def _mem_kv(mem, g, w_mkv):
    B, M, D = mem.shape
    return pl.pallas_call(
        _mem_kv_kernel,
        grid=(B,),
        in_specs=[pl.BlockSpec((1, M, D), lambda b: (b, 0, 0)), pl.BlockSpec((1, D), lambda b: (0, 0)),
                  pl.BlockSpec((D, 2 * MEM_WIDTH), lambda b: (0, 0))],
        out_specs=[pl.BlockSpec((1, M * MEM_HEADS, MEM_HEAD_DIM), lambda b: (b, 0, 0))] * 2,
        out_shape=[jax.ShapeDtypeStruct((B, M * MEM_HEADS, MEM_HEAD_DIM), F32)] * 2,
        compiler_params=pltpu.CompilerParams(dimension_semantics=("parallel",),
                                             vmem_limit_bytes=VMEM_LIMIT),
        name="mem_kv",
    )(mem, g, w_mkv)


def _in_proj_kernel(x_ref, g_ref, wk_ref, wv_ref, wu_ref, wqm_ref, wf_ref, wqt_ref, wvt_ref, bf_ref, hist_ref,
                    mk_ref, mv_ref, wpool_ref, spool_ref,
                    k_ref, v_ref, logf_ref, pnew_ref, opool_ref, omem_ref, *rest, tm, pos0, augment, blk):
    extra, (uext_ref, carry_ref) = rest[:-2], rest[-2:]
    t = pl.program_id(1)
    last = pl.num_programs(1) - 1
    q_scale = FOX_HEAD_DIM ** -0.5 * LOG2E

    @pl.when(t == 0)
    def _():
        uext_ref[0:HIST_ROWS, :] = hist_ref[0]
        carry_ref[...] = jnp.zeros_like(carry_ref)

    xn = _rms(x_ref[0], g_ref[...]).astype(BF16)
    lane = _iota((tm, LANES), 1)
    mem_cols = [slice(h * MEM_HEAD_DIM, (h + 1) * MEM_HEAD_DIM) for h in range(MEM_HEADS)]

    zf = _dot(xn, wf_ref[...]) + bf_ref[...]
    qm = _dot(xn, wqm_ref[...]).astype(BF16)
    k = _dot(xn, wk_ref[...])
    logf = jnp.minimum(zf, 0.0) - jnp.log1p(jnp.exp(-jnp.abs(zf)))
    n_mem = mk_ref.shape[1] // MEM_HEADS
    mem_head = lambda ref, h: ref[0, pl.ds(h, n_mem, stride=MEM_HEADS), :].astype(BF16)
    mem_s = [_dot_nt(qm[:, cols], mem_head(mk_ref, h)) * MEM_HEAD_DIM ** -0.5
             for h, cols in enumerate(mem_cols)]
    v = _dot(xn, wv_ref[...])
    u = _dot(xn, wu_ref[...])
    if augment:
        c, carry_ref[...] = _cumsum_rows(logf, carry_ref[...])
        qt_all = (_dot_nt(wqt_ref[...], xn) * q_scale).astype(BF16)
    else:
        q = _dot_nt(xn, wqt_ref[...]) * q_scale
    for h, (s, cols) in enumerate(zip(mem_s, mem_cols)):
        e = jnp.exp(s - jnp.max(s, axis=-1, keepdims=True))
        o = _dot(e.astype(BF16), mem_head(mv_ref, h)) / jnp.sum(e, axis=-1, keepdims=True)
        omem_ref[0, :, cols] = o.astype(BF16)
    if augment:
        vt_all = _dot_nt(wvt_ref[...], xn).astype(BF16)

    k_ref[0] = k
    v_ref[0] = v
    logf_rows = logf if tm >= LANES else jnp.concatenate([logf, jnp.zeros((LANES - tm, LANES), F32)], axis=0)
    logf_ref[0] = logf_rows.T[:FOX_HEADS, :tm]

    if augment:
        qt_ref, ka_ref, vt_ref = extra
        bias = pltpu.roll(_bias_pieces(c), FOX_HEAD_DIM, axis=1)
        in_bias = (lane >= FOX_HEAD_DIM) & (lane < FOX_HEAD_DIM + BIAS_LANES)
        for pair in range(FOX_HEADS // 2):
            even = k[:, pair * LANES:(pair + 1) * LANES]
            odd = pltpu.roll(even, FOX_HEAD_DIM, axis=1)
            for par, data in ((0, even), (1, odd)):
                h = 2 * pair + par
                tail = jnp.where(in_bias & ((lane & 7) == h), bias, 0.0)
                ka_ref[0, :, h * LANES:(h + 1) * LANES] = jnp.where(lane < FOX_HEAD_DIM, data, tail).astype(BF16)
        q_tail = jnp.where(_iota((LANES - FOX_HEAD_DIM, blk), 0) < BIAS_LANES, 1.0, 0.0).astype(BF16)
        v_tail = jnp.where(_iota((V_SLAB - FOX_HEAD_DIM, blk), 0) == 0, 1.0, 0.0).astype(BF16)
        for h in range(FOX_HEADS):
            qt = qt_all[h * FOX_HEAD_DIM:(h + 1) * FOX_HEAD_DIM]
            vt = vt_all[h * FOX_HEAD_DIM:(h + 1) * FOX_HEAD_DIM]
            for j in range(tm // blk):
                qt_ref[0, j, h * LANES:h * LANES + FOX_HEAD_DIM, :] = qt[:, j * blk:(j + 1) * blk]
                qt_ref[0, j, h * LANES + FOX_HEAD_DIM:(h + 1) * LANES, :] = q_tail
                vt_ref[0, j, h * V_SLAB:h * V_SLAB + FOX_HEAD_DIM, :] = vt[:, j * blk:(j + 1) * blk]
                vt_ref[0, j, h * V_SLAB + FOX_HEAD_DIM:(h + 1) * V_SLAB, :] = v_tail
    else:
        extra[0][0] = q

    uext_ref[HIST_ROWS:HIST_ROWS + tm, :] = u
    pos = pos0 + t * tm + _iota((tm, 1), 0)
    for gi, w in enumerate(POOL_WINDOWS):
        cols = slice(gi * POOL_GROUP_DIM, (gi + 1) * POOL_GROUP_DIM)
        sw = uext_ref[:, cols]
        shift = 1
        while shift < w:
            sw = sw + pltpu.roll(sw, shift, axis=0)
            shift *= 2
        cnt = jnp.minimum(w, pos + 1).astype(F32)
        pooled = sw[HIST_ROWS:, :] / cnt - u[:, cols]
        mixed = _dot(pooled.astype(BF16), wpool_ref[gi]) * spool_ref[:, cols]
        opool_ref[0, :, cols] = mixed.astype(BF16)

    @pl.when(t == last)
    def _():
        pnew_ref[0] = uext_ref[tm + 1:tm + HIST_ROWS, :]

    uext_ref[0:HIST_ROWS, :] = uext_ref[tm:tm + HIST_ROWS, :]


def _in_proj(x, g, weights, bf_rep, hist, mk, mv, wpool, spool, *, tm, pos0, augment, blk=None):
    B, S, D = x.shape
    nt = S // tm
    tile = lambda w: pl.BlockSpec((1, tm, w), lambda b, t: (b, t, 0))
    per_b = lambda r, w: pl.BlockSpec((1, r, w), lambda b, t: (b, 0, 0))
    const = lambda shape: pl.BlockSpec(shape, lambda b, t: (0,) * len(shape))
    if augment:
        W = FOX_HEADS * LANES
        t_tile = lambda rows: pl.BlockSpec((1, tm // blk, rows, blk), lambda b, t: (b, t, 0, 0))
        abc_specs = [t_tile(W), tile(W), t_tile(FOX_HEADS * V_SLAB)]
        abc_shapes = [jax.ShapeDtypeStruct((B, S // blk, W, blk), BF16),
                      jax.ShapeDtypeStruct((B, S, W), BF16),
                      jax.ShapeDtypeStruct((B, S // blk, FOX_HEADS * V_SLAB, blk), BF16)]
    else:
        abc_specs = [tile(FOX_WIDTH)]
        abc_shapes = [jax.ShapeDtypeStruct((B, S, FOX_WIDTH), F32)]
    return pl.pallas_call(
        functools.partial(_in_proj_kernel, tm=tm, pos0=pos0, augment=augment, blk=blk),
        grid=(B, nt),
        in_specs=[tile(D), const((1, D))] + [const(w.shape) for w in weights] + [
            const((1, LANES)), per_b(HIST_ROWS, POOL_WIDTH), per_b(*mk.shape[1:]), per_b(*mv.shape[1:]),
            const((POOL_GROUPS, POOL_GROUP_DIM, POOL_GROUP_DIM)), const((1, POOL_WIDTH))],
        out_specs=[tile(FOX_WIDTH), tile(FOX_WIDTH), pl.BlockSpec((1, FOX_HEADS, tm), lambda b, t: (b, 0, t)),
                   per_b(POOL_HIST, POOL_WIDTH), tile(POOL_WIDTH), tile(MEM_WIDTH)] + abc_specs,
        out_shape=[jax.ShapeDtypeStruct((B, S, FOX_WIDTH), F32),
                   jax.ShapeDtypeStruct((B, S, FOX_WIDTH), F32),
                   jax.ShapeDtypeStruct((B, FOX_HEADS, S), F32),
                   jax.ShapeDtypeStruct((B, POOL_HIST, POOL_WIDTH), F32),
                   jax.ShapeDtypeStruct((B, S, POOL_WIDTH), BF16),
                   jax.ShapeDtypeStruct((B, S, MEM_WIDTH), BF16)] + abc_shapes,
        scratch_shapes=[pltpu.VMEM((HIST_ROWS + tm, POOL_WIDTH), F32), pltpu.VMEM((1, LANES), F32)],
        compiler_params=pltpu.CompilerParams(dimension_semantics=("parallel", "arbitrary"),
                                             vmem_limit_bytes=VMEM_LIMIT),
        name="in_proj",
    )(x, g, *weights, bf_rep, hist, mk, mv, wpool, spool)


def _fox_prompt_kernel(qt_ref, k_ref, vt_ref, o_ref, m_ref, acc_ref, *, blk):
    i = pl.program_id(1)
    m_ref[...] = jnp.full_like(m_ref, -jnp.inf)
    acc_ref[...] = jnp.zeros_like(acc_ref)
    slabs = [slice(h * LANES, (h + 1) * LANES) for h in range(FOX_HEADS)]

    def scores(j, nb, h):
        start = pl.multiple_of(j * blk, blk)
        return _dot(k_ref[0, pl.ds(start, nb * blk), slabs[h]], qt_ref[0, 0, slabs[h], :])

    def accumulate(j, nb, h, st, masked):
        if masked:
            st = jnp.where(_iota(st.shape, 0) <= _iota(st.shape, 1), st, -jnp.inf)
        m_prev = m_ref[h]
        m_new = jnp.maximum(m_prev, jnp.max(st, axis=0, keepdims=True))
        p = jnp.exp2(st - m_new).astype(BF16)
        pv = _dot(vt_ref[0, j, h * V_SLAB:(h + 1) * V_SLAB, :], p[:blk])
        for r in range(1, nb):
            pv = pv + _dot(vt_ref[0, j + r, h * V_SLAB:(h + 1) * V_SLAB, :], p[r * blk:(r + 1) * blk])
        acc_ref[h] = jnp.exp2(m_prev - m_new) * acc_ref[h] + pv
        m_ref[h] = m_new

    def run(first, n_visible):
        spans = [(r, min(KEY_BLOCKS_PER_ITEM, n_visible - r), False)
                 for r in range(0, n_visible, KEY_BLOCKS_PER_ITEM)]
        if n_visible < KEY_BLOCKS_PER_ITER:
            spans.append((n_visible, 1, True))
        work = [(first + r, nb, h, masked) for r, nb, masked in spans for h in range(FOX_HEADS)]
        pending = [scores(*w[:3]) for w in work[:SCORES_AHEAD]]
        for n, (j, nb, h, masked) in enumerate(work):
            if n + SCORES_AHEAD < len(work):
                pending.append(scores(*work[n + SCORES_AHEAD][:3]))
            accumulate(j, nb, h, pending.pop(0), masked)

    shift = KEY_BLOCKS_PER_ITER.bit_length() - 1
    n_iter = i >> shift

    def body(jj, carry):
        run(KEY_BLOCKS_PER_ITER * jj, KEY_BLOCKS_PER_ITER)
        return carry

    lax.fori_loop(0, n_iter, body, 0)
    for rem in range(KEY_BLOCKS_PER_ITER):
        @pl.when(i - (n_iter << shift) == rem)
        def _(rem=rem):
            run(i - rem, rem)

    for pair in range(FOX_HEADS // 2):
        halves = []
        for par in (0, 1):
            acc = acc_ref[2 * pair + par]
            halves.append(acc[:FOX_HEAD_DIM] / acc[FOX_HEAD_DIM:FOX_HEAD_DIM + 1])
        o_ref[0, :, pair * LANES:(pair + 1) * LANES] = jnp.concatenate(halves, axis=0).T.astype(BF16)


def _fox_prompt(qt, ka, vt, *, blk):
    B, S, W = ka.shape
    nblk = S // blk
    return pl.pallas_call(
        functools.partial(_fox_prompt_kernel, blk=blk),
        grid=(B, nblk),
        in_specs=[pl.BlockSpec((1, 1, W, blk), lambda b, i: (b, i, 0, 0)),
                  pl.BlockSpec((1, S, W), lambda b, i: (b, 0, 0)),
                  pl.BlockSpec((1, nblk, FOX_HEADS * V_SLAB, blk), lambda b, i: (b, 0, 0, 0))],
        out_specs=pl.BlockSpec((1, blk, FOX_WIDTH), lambda b, i: (b, i, 0)),
        out_shape=jax.ShapeDtypeStruct((B, S, FOX_WIDTH), BF16),
        scratch_shapes=[pltpu.VMEM((FOX_HEADS, 1, blk), F32), pltpu.VMEM((FOX_HEADS, V_SLAB, blk), F32)],
        compiler_params=pltpu.CompilerParams(
            dimension_semantics=("parallel", "arbitrary"), vmem_limit_bytes=VMEM_LIMIT),
        name="fox_prompt",
    )(qt, ka, vt)


def _bias_piece_rows(c):
    b = c * (-LOG2E)
    p0 = b.astype(BF16).astype(F32)
    r1 = b - p0
    p1 = r1.astype(BF16).astype(F32)
    p2 = (r1 - p1).astype(BF16).astype(F32)
    return jnp.concatenate([p0, p1, p2], axis=0)


def _fox_sample_kernel(q_ref, kn_ref, vn_ref, lfn_ref, kt_ref, vt_ref, lft_ref, o_ref, *, chunk):
    T = q_ref.shape[1]
    P = lft_ref.shape[2]
    HT = FOX_HEADS * T
    t_shift = T.bit_length() - 1
    assert HT == LANES and 1 << t_shift == T

    q_rep = jnp.concatenate([q_ref[0]] * FOX_HEADS, axis=0)
    row_head = _iota((HT, FOX_WIDTH), 0) >> t_shift
    q_bd = jnp.where(row_head == (_iota((HT, FOX_WIDTH), 1) >> 6), q_rep, 0.0).astype(BF16)
    e_lane = _iota((HT, LANES), 1)
    e_bd = jnp.where((e_lane < BIAS_LANES) & ((e_lane & 7) == (_iota((HT, LANES), 0) >> t_shift)),
                     1.0, 0.0).astype(BF16)

    def bias_operand(c):
        rows = _bias_piece_rows(c)
        return jnp.concatenate([rows, jnp.zeros((LANES - BIAS_LANES, c.shape[1]), F32)], axis=0).astype(BF16)

    def running_sum(lf, carry, upper):
        pieces = jnp.concatenate(list(_split3(lf)) + [jnp.zeros(lf.shape, BF16)], axis=0)
        loc = _dot(pieces, upper)
        return loc[0:8] + loc[8:16] + loc[16:24] + carry

    upper = (_iota((chunk, chunk), 0) <= _iota((chunk, chunk), 1)).astype(BF16)
    carry = jnp.zeros((FOX_HEADS, 1), F32)
    c_parts = []
    for ci in range(P // chunk):
        c = running_sum(lft_ref[0, :, ci * chunk:(ci + 1) * chunk], carry, upper)
        carry = c[:, chunk - 1:chunk]
        c_parts.append(c)
    kt = kt_ref[0].reshape(FOX_WIDTH, P).astype(BF16)
    s_old = _dot(q_bd, kt) + _dot(e_bd, bias_operand(jnp.concatenate(c_parts, axis=1)))

    pad_lanes = lambda a: jnp.concatenate([a, jnp.zeros((a.shape[0], LANES - a.shape[1]), F32)], axis=1)
    pad_rows = lambda a: jnp.concatenate([a, jnp.zeros((LANES - T, a.shape[1]), F32)], axis=0)
    upper_new = (_iota((LANES, LANES), 0) <= _iota((LANES, LANES), 1)).astype(BF16)
    c_new = running_sum(pad_lanes(lfn_ref[0]), carry, upper_new)
    s_new = _dot_nt(q_bd, pad_rows(kn_ref[0]).astype(BF16)) + _dot(e_bd, bias_operand(c_new))
    s_new = jnp.where(_iota((HT, LANES), 1) <= (_iota((HT, LANES), 0) & (T - 1)), s_new, -jnp.inf)

    m = jnp.maximum(jnp.max(s_old, axis=1, keepdims=True), jnp.max(s_new, axis=1, keepdims=True))
    p_old = jnp.exp2(s_old - m).astype(BF16)
    p_new = jnp.exp2(s_new - m).astype(BF16)
    l = (jnp.sum(p_old.astype(F32), axis=1, keepdims=True)
         + jnp.sum(p_new.astype(F32), axis=1, keepdims=True))
    r = (_dot_nt(p_old, vt_ref[0].reshape(FOX_WIDTH, P).astype(BF16))
         + _dot(p_new, pad_rows(vn_ref[0]).astype(BF16))) / l
    col_head = _iota((T, FOX_WIDTH), 1) >> 6
    o = jnp.zeros((T, FOX_WIDTH), F32)
    for h in range(FOX_HEADS):
        o = jnp.where(col_head == h, r[h * T:(h + 1) * T, :], o)
    o_ref[0] = o.astype(BF16)


def _fox_sample(q, k_new, v_new, lf_new_t, cache_kt, cache_vt, cache_lft, *, chunk):
    B, T, _ = q.shape
    P = cache_lft.shape[2]
    new = lambda w: pl.BlockSpec((1, T, w), lambda b: (b, 0, 0))
    kv_spec = pl.BlockSpec((1, FOX_HEADS, FOX_HEAD_DIM, P), lambda b: (b, 0, 0, 0))
    return pl.pallas_call(
        functools.partial(_fox_sample_kernel, chunk=chunk),
        grid=(B,),
        in_specs=[new(FOX_WIDTH), new(FOX_WIDTH), new(FOX_WIDTH),
                  pl.BlockSpec((1, FOX_HEADS, T), lambda b: (b, 0, 0)), kv_spec, kv_spec,
                  pl.BlockSpec((1, FOX_HEADS, P), lambda b: (b, 0, 0))],
        out_specs=new(FOX_WIDTH),
        out_shape=jax.ShapeDtypeStruct((B, T, FOX_WIDTH), BF16),
        compiler_params=pltpu.CompilerParams(dimension_semantics=("parallel",),
                                             vmem_limit_bytes=VMEM_LIMIT),
        name="fox_sample",
    )(q, k_new, v_new, lf_new_t, cache_kt, cache_vt, cache_lft)


def _merge_ffn_kernel(x_ref, ofox_ref, opool_ref, omem_ref, ga_ref, wg_ref, wbf_ref, wbp_ref, wbm_ref,
                      wout_ref, gm_ref, wup_ref, wdown_ref, gfin_ref, y_ref, *, streams, ff_chunk, final_norm):
    D = x_ref.shape[-1]
    rows = x_ref.shape[1] // streams
    groups = [slice(i * rows, (i + 1) * rows) for i in range(streams)]
    branches = ((ofox_ref, wbf_ref), (opool_ref, wbp_ref), (omem_ref, wbm_ref))

    xs = [x_ref[0, r, :] for r in groups]
    pre = []
    for x, r in zip(xs, groups):
        xn = _rms(x, ga_ref[...]).astype(BF16)
        pre.append([(_dot(xn, wg_ref[:, bi * D:(bi + 1) * D]), _dot(o_ref[0, r, :], w_ref[...]))
                    for bi, (o_ref, w_ref) in enumerate(branches)])
    hs = []
    for x, terms in zip(xs, pre):
        (g0, v0), (g1, v1), (g2, v2) = terms
        m = jax.nn.sigmoid(g0) * v0 + jax.nn.sigmoid(g1) * v1 + jax.nn.sigmoid(g2) * v2
        hs.append(x + _dot(m.astype(BF16), wout_ref[...]))
    hns = [_rms(h, gm_ref[...]).astype(BF16) for h in hs]
    accs = list(hs)
    for c in range(wup_ref.shape[1] // ff_chunk):
        cols = slice(c * ff_chunk, (c + 1) * ff_chunk)
        ups = [jnp.maximum(_dot(hn, wup_ref[:, cols]), 0.0) for hn in hns]
        accs = [acc + _dot((a * a).astype(BF16), wdown_ref[cols, :]) for acc, a in zip(accs, ups)]
    for acc, r in zip(accs, groups):
        y_ref[0, r, :] = _rms(acc, gfin_ref[...]) if final_norm else acc


def _merge_ffn(x, ofox, opool, omem, ga, wg, wbf, wbp, wbm, wout, gm, wup, wdown, gfin, *, tm, streams,
               ff_chunk, final_norm):
    B, S, D = x.shape
    tile = lambda w: pl.BlockSpec((1, tm, w), lambda b, t: (b, t, 0))
    const = lambda a: pl.BlockSpec(a.shape, lambda b, t: (0,) * a.ndim, pipeline_mode=pl.Buffered(1))
    consts = (ga, wg, wbf, wbp, wbm, wout, gm, wup, wdown, gfin)
    return pl.pallas_call(
        functools.partial(_merge_ffn_kernel, streams=streams, ff_chunk=ff_chunk, final_norm=final_norm),
        grid=(B, S // tm),
        in_specs=[tile(D), tile(FOX_WIDTH), tile(POOL_WIDTH), tile(MEM_WIDTH)] + [const(a) for a in consts],
        out_specs=tile(D),
        out_shape=jax.ShapeDtypeStruct((B, S, D), F32),
        compiler_params=pltpu.CompilerParams(dimension_semantics=("parallel", "parallel"),
                                             vmem_limit_bytes=VMEM_LIMIT),
        name="merge_ffn",
    )(x, ofox, opool, omem, *consts)


def _tile_rows(n, want):
    return want if n % want == 0 else n


def _repack_w_in(w_in):
    o_f = 3 * FOX_WIDTH
    o_u = o_f + FOX_HEADS
    o_qm = o_u + POOL_WIDTH
    o_g = o_qm + MEM_WIDTH
    b16 = lambda a: a.astype(BF16)
    wq, wk, wv = (w_in[:, i * FOX_WIDTH:(i + 1) * FOX_WIDTH] for i in range(3))
    wf_rep = _rep_lanes(w_in[:, o_f:o_u])
    weights = (b16(wk), b16(wv), b16(w_in[:, o_u:o_qm]), b16(w_in[:, o_qm:o_g]), b16(wf_rep),
               b16(wq).T, b16(wv).T)
    return weights, b16(w_in[:, o_g:])


def _rep_lanes(a):
    pad = jnp.zeros(a.shape[:-1] + (LANES - BIAS_LANES,), a.dtype)
    return jnp.concatenate([a] * BIAS_PIECES + [pad], axis=-1)


def kernel(x_prompt, x_sample, mem_prompt, cache_fox_k, cache_fox_v, cache_fox_logf, cache_pool,
           cache_mem_k, cache_mem_v, g_attn, w_in, b_f, w_pool, s_pool, g_mem, w_mkv,
           w_br_fox, w_br_pool, w_br_mem, w_out, g_mlp, w_up, w_down, g_final):
    depth = w_in.shape[0]
    B, S, D = x_prompt.shape
    BS, T, _ = x_sample.shape
    P = cache_fox_k.shape[2]
    M = mem_prompt.shape[1]
    tm_p = _tile_rows(S, 512)
    blk_p = _tile_rows(tm_p, 256)
    tm_s = _tile_rows(BS * T, 256)
    chunk_s = _tile_rows(P, 256)
    gfin = g_final.reshape(1, D)

    xp, xs = x_prompt, x_sample
    outs_p = [[] for _ in range(6)]
    outs_s = [[] for _ in range(4)]
    for l in range(depth):
        weights, wg = _repack_w_in(w_in[l])
        bf_rep = _rep_lanes(b_f[l]).reshape(1, LANES)
        g_a, g_m, g_f = g_attn[l].reshape(1, D), g_mem[l].reshape(1, D), g_mlp[l].reshape(1, D)
        wpool, spool = w_pool[l].astype(BF16), s_pool[l].reshape(1, POOL_WIDTH)
        wbf, wbp, wbm = (w.astype(BF16) for w in (w_br_fox[l], w_br_pool[l], w_br_mem[l]))
        wout, wup, wdown = (w.astype(BF16) for w in (w_out[l], w_up[l], w_down[l]))
        final = l == depth - 1

        mk, mv = _mem_kv(mem_prompt, g_m, w_mkv[l].astype(BF16))
        hist0 = jnp.zeros((B, HIST_ROWS, POOL_WIDTH), F32)
        k, v, logf_t, pnew, opool, omem, qt, ka, vt = _in_proj(
            xp, g_a, weights, bf_rep, hist0, mk, mv, wpool, spool, tm=tm_p, pos0=0, augment=True,
            blk=blk_p)
        logf = jnp.transpose(logf_t, (0, 2, 1))
        ofox = _fox_prompt(qt, ka, vt, blk=blk_p)
        post = functools.partial(_merge_ffn, ga=g_a, wg=wg, wbf=wbf, wbp=wbp, wbm=wbm, wout=wout, gm=g_f,
                                 wup=wup, wdown=wdown, gfin=gfin, ff_chunk=1024, final_norm=final)
        xp = post(xp, ofox, opool, omem, tm=tm_p, streams=2 if tm_p % 512 == 0 else 1)
        for acc, val in zip(outs_p, (k.reshape(B, S, FOX_HEADS, FOX_HEAD_DIM),
                                     v.reshape(B, S, FOX_HEADS, FOX_HEAD_DIM), logf, pnew,
                                     mk.reshape(B, M, MEM_HEADS, MEM_HEAD_DIM),
                                     mv.reshape(B, M, MEM_HEADS, MEM_HEAD_DIM))):
            acc.append(val)

        hist = jnp.pad(cache_pool[l], ((0, 0), (HIST_ROWS - POOL_HIST, 0), (0, 0)))
        cmk = cache_mem_k[l].reshape(BS, M * MEM_HEADS, MEM_HEAD_DIM)
        cmv = cache_mem_v[l].reshape(BS, M * MEM_HEADS, MEM_HEAD_DIM)
        k, v, logf_t, pnew, opool, omem, q = _in_proj(
            xs, g_a, weights, bf_rep, hist, cmk, cmv, wpool, spool, tm=T, pos0=P, augment=False)
        logf = jnp.transpose(logf_t, (0, 2, 1))
        ofox = _fox_sample(q, k, v, logf_t,
                           jnp.transpose(cache_fox_k[l], (0, 2, 3, 1)), jnp.transpose(cache_fox_v[l], (0, 2, 3, 1)),
                           jnp.transpose(cache_fox_logf[l], (0, 2, 1)), chunk=chunk_s)
        flat = lambda a: a.reshape(1, BS * T, a.shape[-1])
        xs = post(flat(xs), flat(ofox), flat(opool), flat(omem), tm=tm_s, streams=1).reshape(BS, T, D)
        for acc, val in zip(outs_s, (k.reshape(BS, T, FOX_HEADS, FOX_HEAD_DIM),
                                     v.reshape(BS, T, FOX_HEADS, FOX_HEAD_DIM), logf, pnew)):
            acc.append(val)

    stack = lambda o: o[0][None] if depth == 1 else jnp.stack(o)
    return (xp, xs, *(stack(o) for o in outs_p), *(stack(o) for o in outs_s))
```

```python
import functools
import math

import jax
import jax.numpy as jnp
from jax import lax
from jax.experimental import pallas as pl
from jax.experimental.pallas import tpu as pltpu

F32 = jnp.float32
BF16 = jnp.bfloat16

EPS = 1e-6
LANES = 128
FOX_HEADS = 8
FOX_HEAD_DIM = 64
FOX_WIDTH = FOX_HEADS * FOX_HEAD_DIM
POOL_WINDOWS = (2, 4, 8, 16)
POOL_GROUPS = 4
POOL_GROUP_DIM = 128
POOL_WIDTH = POOL_GROUPS * POOL_GROUP_DIM
POOL_HIST = 15
HIST_ROWS = 16
MEM_HEADS = 4
MEM_HEAD_DIM = 128
MEM_WIDTH = MEM_HEADS * MEM_HEAD_DIM
N_BRANCH = 3
LOG2E = math.log2(math.e)
BIAS_PIECES = 3
BIAS_LANES = BIAS_PIECES * FOX_HEADS
V_SLAB = 80
KEY_BLOCKS_PER_ITER = 4
KEY_BLOCKS_PER_ITEM = 1
SCORES_AHEAD = 6
CUMSUM_CHUNK = 128
VMEM_LIMIT = 56 * 1024 * 1024


def _dot(a, b):
    return jnp.dot(a, b, preferred_element_type=F32)


def _dot_nt(a, b):
    return lax.dot_general(a, b, (((1,), (1,)), ((), ())), preferred_element_type=F32)


def _rms(x, g):
    return x * lax.rsqrt(jnp.mean(x * x, axis=-1, keepdims=True) + EPS) * g


def _iota(shape, dim):
    return lax.broadcasted_iota(jnp.int32, shape, dim)


def _split3(x):
    p0 = x.astype(BF16)
    r1 = x - p0.astype(F32)
    p1 = r1.astype(BF16)
    p2 = (r1 - p1.astype(F32)).astype(BF16)
    return p0, p1, p2


def _cumsum_rows(x, carry):
    n = x.shape[0]
    ch = min(n, CUMSUM_CHUNK)
    tril = (_iota((ch, ch), 1) <= _iota((ch, ch), 0)).astype(BF16)
    outs = []
    for c0 in range(0, n, ch):
        r = _dot(tril, jnp.concatenate(_split3(x[c0:c0 + ch]), axis=1))
        c = r[:, :LANES] + r[:, LANES:2 * LANES] + r[:, 2 * LANES:] + carry
        carry = c[ch - 1:ch]
        outs.append(c)
    return (outs[0] if len(outs) == 1 else jnp.concatenate(outs, axis=0)), carry


def _bias_pieces(c):
    b = c * (-LOG2E)
    p0 = b.astype(BF16).astype(F32)
    r1 = b - p0
    p1 = r1.astype(BF16).astype(F32)
    p2 = (r1 - p1).astype(BF16).astype(F32)
    grp = _iota(c.shape, 1) >> 3
    return jnp.where(grp == 0, p0, jnp.where(grp == 1, p1, jnp.where(grp == 2, p2, 0.0)))


def _mem_kv_kernel(mem_ref, g_ref, w_ref, mk_ref, mv_ref):
    M = mem_ref.shape[1]
    xn = _rms(mem_ref[0], g_ref[...]).astype(BF16)
    for out_ref, off in ((mk_ref, 0), (mv_ref, MEM_WIDTH)):
        kv = _dot(xn, w_ref[:, off:off + MEM_WIDTH])
        for h in range(MEM_HEADS):
            out_ref[0, pl.ds(h, M, stride=MEM_HEADS), :] = kv[:, h * MEM_HEAD_DIM:(h + 1) * MEM_HEAD_DIM]


def _mem_kv(mem, g, w_mkv):
    B, M, D = mem.shape
    return pl.pallas_call(
        _mem_kv_kernel,
        grid=(B,),
        in_specs=[pl.BlockSpec((1, M, D), lambda b: (b, 0, 0)), pl.BlockSpec((1, D), lambda b: (0, 0)),
                  pl.BlockSpec((D, 2 * MEM_WIDTH), lambda b: (0, 0))],
        out_specs=[pl.BlockSpec((1, M * MEM_HEADS, MEM_HEAD_DIM), lambda b: (b, 0, 0))] * 2,
        out_shape=[jax.ShapeDtypeStruct((B, M * MEM_HEADS, MEM_HEAD_DIM), F32)] * 2,
        compiler_params=pltpu.CompilerParams(dimension_semantics=("parallel",),
                                             vmem_limit_bytes=VMEM_LIMIT),
        name="mem_kv",
    )(mem, g, w_mkv)


def _in_proj_kernel(x_ref, g_ref, wk_ref, wv_ref, wu_ref, wqm_ref, wf_ref, wqt_ref, bf_ref, hist_ref,
                    mk_ref, mv_ref, wpool_ref, spool_ref,
                    k_ref, v_ref, logf_ref, pnew_ref, opool_ref, omem_ref, *rest, tm, pos0, augment, blk):
    extra, (uext_ref, carry_ref) = rest[:-2], rest[-2:]
    t = pl.program_id(1)
    last = pl.num_programs(1) - 1
    q_scale = FOX_HEAD_DIM ** -0.5 * LOG2E

    @pl.when(t == 0)
    def _():
        uext_ref[0:HIST_ROWS, :] = hist_ref[0]
        carry_ref[...] = jnp.zeros_like(carry_ref)

    xn = _rms(x_ref[0], g_ref[...]).astype(BF16)
    lane = _iota((tm, LANES), 1)
    mem_cols = [slice(h * MEM_HEAD_DIM, (h + 1) * MEM_HEAD_DIM) for h in range(MEM_HEADS)]

    zf = _dot(xn, wf_ref[...]) + bf_ref[...]
    qm = _dot(xn, wqm_ref[...]).astype(BF16)
    k = _dot(xn, wk_ref[...])
    logf = jnp.minimum(zf, 0.0) - jnp.log1p(jnp.exp(-jnp.abs(zf)))
    n_mem = mk_ref.shape[1] // MEM_HEADS
    mem_head = lambda ref, h: ref[0, pl.ds(h, n_mem, stride=MEM_HEADS), :].astype(BF16)
    mem_s = [_dot_nt(qm[:, cols], mem_head(mk_ref, h)) * MEM_HEAD_DIM ** -0.5
             for h, cols in enumerate(mem_cols)]
    v = _dot(xn, wv_ref[...])
    u = _dot(xn, wu_ref[...])
    if augment:
        c, carry_ref[...] = _cumsum_rows(logf, carry_ref[...])
        qt_all = (_dot_nt(wqt_ref[...], xn) * q_scale).astype(BF16)
    else:
        q = _dot_nt(xn, wqt_ref[...]) * q_scale
    for h, (s, cols) in enumerate(zip(mem_s, mem_cols)):
        e = jnp.exp(s - jnp.max(s, axis=-1, keepdims=True))
        o = _dot(e.astype(BF16), mem_head(mv_ref, h)) / jnp.sum(e, axis=-1, keepdims=True)
        omem_ref[0, :, cols] = o.astype(BF16)
    if augment:
        vt_all = v.T.astype(BF16)

    k_ref[0] = k
    v_ref[0] = v
    logf_rows = logf if tm >= LANES else jnp.concatenate([logf, jnp.zeros((LANES - tm, LANES), F32)], axis=0)
    logf_ref[0] = logf_rows.T[:FOX_HEADS, :tm]

    if augment:
        qt_ref, ka_ref, vt_ref = extra
        bias = pltpu.roll(_bias_pieces(c), FOX_HEAD_DIM, axis=1)
        in_bias = (lane >= FOX_HEAD_DIM) & (lane < FOX_HEAD_DIM + BIAS_LANES)
        for pair in range(FOX_HEADS // 2):
            even = k[:, pair * LANES:(pair + 1) * LANES]
            odd = pltpu.roll(even, FOX_HEAD_DIM, axis=1)
            for par, data in ((0, even), (1, odd)):
                h = 2 * pair + par
                tail = jnp.where(in_bias & ((lane & 7) == h), bias, 0.0)
                ka_ref[0, :, h * LANES:(h + 1) * LANES] = jnp.where(lane < FOX_HEAD_DIM, data, tail).astype(BF16)
        q_tail = jnp.where(_iota((LANES - FOX_HEAD_DIM, blk), 0) < BIAS_LANES, 1.0, 0.0).astype(BF16)
        v_tail = jnp.where(_iota((V_SLAB - FOX_HEAD_DIM, blk), 0) == 0, 1.0, 0.0).astype(BF16)
        for h in range(FOX_HEADS):
            qt = qt_all[h * FOX_HEAD_DIM:(h + 1) * FOX_HEAD_DIM]
            vt = vt_all[h * FOX_HEAD_DIM:(h + 1) * FOX_HEAD_DIM]
            for j in range(tm // blk):
                qt_ref[0, j, h * LANES:h * LANES + FOX_HEAD_DIM, :] = qt[:, j * blk:(j + 1) * blk]
                qt_ref[0, j, h * LANES + FOX_HEAD_DIM:(h + 1) * LANES, :] = q_tail
                vt_ref[0, j, h * V_SLAB:h * V_SLAB + FOX_HEAD_DIM, :] = vt[:, j * blk:(j + 1) * blk]
                vt_ref[0, j, h * V_SLAB + FOX_HEAD_DIM:(h + 1) * V_SLAB, :] = v_tail
    else:
        extra[0][0] = q

    uext_ref[HIST_ROWS:HIST_ROWS + tm, :] = u
    pos = pos0 + t * tm + _iota((tm, 1), 0)
    for gi, w in enumerate(POOL_WINDOWS):
        cols = slice(gi * POOL_GROUP_DIM, (gi + 1) * POOL_GROUP_DIM)
        sw = uext_ref[:, cols]
        shift = 1
        while shift < w:
            sw = sw + pltpu.roll(sw, shift, axis=0)
            shift *= 2
        cnt = jnp.minimum(w, pos + 1).astype(F32)
        pooled = sw[HIST_ROWS:, :] / cnt - u[:, cols]
        mixed = _dot(pooled.astype(BF16), wpool_ref[gi]) * spool_ref[:, cols]
        opool_ref[0, :, cols] = mixed.astype(BF16)

    @pl.when(t == last)
    def _():
        pnew_ref[0] = uext_ref[tm + 1:tm + HIST_ROWS, :]

    uext_ref[0:HIST_ROWS, :] = uext_ref[tm:tm + HIST_ROWS, :]


def _in_proj(x, g, weights, bf_rep, hist, mk, mv, wpool, spool, *, tm, pos0, augment, blk=None):
    B, S, D = x.shape
    nt = S // tm
    tile = lambda w: pl.BlockSpec((1, tm, w), lambda b, t: (b, t, 0))
    per_b = lambda r, w: pl.BlockSpec((1, r, w), lambda b, t: (b, 0, 0))
    const = lambda shape: pl.BlockSpec(shape, lambda b, t: (0,) * len(shape))
    if augment:
        W = FOX_HEADS * LANES
        t_tile = lambda rows: pl.BlockSpec((1, tm // blk, rows, blk), lambda b, t: (b, t, 0, 0))
        abc_specs = [t_tile(W), tile(W), t_tile(FOX_HEADS * V_SLAB)]
        abc_shapes = [jax.ShapeDtypeStruct((B, S // blk, W, blk), BF16),
                      jax.ShapeDtypeStruct((B, S, W), BF16),
                      jax.ShapeDtypeStruct((B, S // blk, FOX_HEADS * V_SLAB, blk), BF16)]
    else:
        abc_specs = [tile(FOX_WIDTH)]
        abc_shapes = [jax.ShapeDtypeStruct((B, S, FOX_WIDTH), F32)]
    return pl.pallas_call(
        functools.partial(_in_proj_kernel, tm=tm, pos0=pos0, augment=augment, blk=blk),
        grid=(B, nt),
        in_specs=[tile(D), const((1, D))] + [const(w.shape) for w in weights] + [
            const((1, LANES)), per_b(HIST_ROWS, POOL_WIDTH), per_b(*mk.shape[1:]), per_b(*mv.shape[1:]),
            const((POOL_GROUPS, POOL_GROUP_DIM, POOL_GROUP_DIM)), const((1, POOL_WIDTH))],
        out_specs=[tile(FOX_WIDTH), tile(FOX_WIDTH), pl.BlockSpec((1, FOX_HEADS, tm), lambda b, t: (b, 0, t)),
                   per_b(POOL_HIST, POOL_WIDTH), tile(POOL_WIDTH), tile(MEM_WIDTH)] + abc_specs,
        out_shape=[jax.ShapeDtypeStruct((B, S, FOX_WIDTH), F32),
                   jax.ShapeDtypeStruct((B, S, FOX_WIDTH), F32),
                   jax.ShapeDtypeStruct((B, FOX_HEADS, S), F32),
                   jax.ShapeDtypeStruct((B, POOL_HIST, POOL_WIDTH), F32),
                   jax.ShapeDtypeStruct((B, S, POOL_WIDTH), BF16),
                   jax.ShapeDtypeStruct((B, S, MEM_WIDTH), BF16)] + abc_shapes,
        scratch_shapes=[pltpu.VMEM((HIST_ROWS + tm, POOL_WIDTH), F32), pltpu.VMEM((1, LANES), F32)],
        compiler_params=pltpu.CompilerParams(dimension_semantics=("parallel", "arbitrary"),
                                             vmem_limit_bytes=VMEM_LIMIT),
        name="in_proj",
    )(x, g, *weights, bf_rep, hist, mk, mv, wpool, spool)


def _fox_prompt_kernel(qt_ref, k_ref, vt_ref, o_ref, m_ref, acc_ref, *, blk):
    i = pl.program_id(1)
    m_ref[...] = jnp.full_like(m_ref, -jnp.inf)
    acc_ref[...] = jnp.zeros_like(acc_ref)
    slabs = [slice(h * LANES, (h + 1) * LANES) for h in range(FOX_HEADS)]

    def scores(j, nb, h):
        start = pl.multiple_of(j * blk, blk)
        return _dot(k_ref[0, pl.ds(start, nb * blk), slabs[h]], qt_ref[0, 0, slabs[h], :])

    def accumulate(j, nb, h, st, masked):
        if masked:
            st = jnp.where(_iota(st.shape, 0) <= _iota(st.shape, 1), st, -jnp.inf)
        m_prev = m_ref[h]
        m_new = jnp.maximum(m_prev, jnp.max(st, axis=0, keepdims=True))
        p = jnp.exp2(st - m_new).astype(BF16)
        pv = _dot(vt_ref[0, j, h * V_SLAB:(h + 1) * V_SLAB, :], p[:blk])
        for r in range(1, nb):
            pv = pv + _dot(vt_ref[0, j + r, h * V_SLAB:(h + 1) * V_SLAB, :], p[r * blk:(r + 1) * blk])
        acc_ref[h] = jnp.exp2(m_prev - m_new) * acc_ref[h] + pv
        m_ref[h] = m_new

    def run(first, n_visible):
        spans = [(r, min(KEY_BLOCKS_PER_ITEM, n_visible - r), False)
                 for r in range(0, n_visible, KEY_BLOCKS_PER_ITEM)]
        if n_visible < KEY_BLOCKS_PER_ITER:
            spans.append((n_visible, 1, True))
        work = [(first + r, nb, h, masked) for r, nb, masked in spans for h in range(FOX_HEADS)]
        pending = [scores(*w[:3]) for w in work[:SCORES_AHEAD]]
        for n, (j, nb, h, masked) in enumerate(work):
            if n + SCORES_AHEAD < len(work):
                pending.append(scores(*work[n + SCORES_AHEAD][:3]))
            accumulate(j, nb, h, pending.pop(0), masked)

    shift = KEY_BLOCKS_PER_ITER.bit_length() - 1
    n_iter = i >> shift

    def body(jj, carry):
        run(KEY_BLOCKS_PER_ITER * jj, KEY_BLOCKS_PER_ITER)
        return carry

    lax.fori_loop(0, n_iter, body, 0)
    for rem in range(KEY_BLOCKS_PER_ITER):
        @pl.when(i - (n_iter << shift) == rem)
        def _(rem=rem):
            run(i - rem, rem)

    for h in range(FOX_HEADS):
        acc = acc_ref[h]
        o_ref[0, 0, h * FOX_HEAD_DIM:(h + 1) * FOX_HEAD_DIM, :] = (
            acc[:FOX_HEAD_DIM] / acc[FOX_HEAD_DIM:FOX_HEAD_DIM + 1])


def _fox_prompt(qt, ka, vt, *, blk):
    B, S, W = ka.shape
    nblk = S // blk
    return pl.pallas_call(
        functools.partial(_fox_prompt_kernel, blk=blk),
        grid=(B, nblk),
        in_specs=[pl.BlockSpec((1, 1, W, blk), lambda b, i: (b, i, 0, 0)),
                  pl.BlockSpec((1, S, W), lambda b, i: (b, 0, 0)),
                  pl.BlockSpec((1, nblk, FOX_HEADS * V_SLAB, blk), lambda b, i: (b, 0, 0, 0))],
        out_specs=pl.BlockSpec((1, 1, FOX_WIDTH, blk), lambda b, i: (b, i, 0, 0)),
        out_shape=jax.ShapeDtypeStruct((B, nblk, FOX_WIDTH, blk), F32),
        scratch_shapes=[pltpu.VMEM((FOX_HEADS, 1, blk), F32), pltpu.VMEM((FOX_HEADS, V_SLAB, blk), F32)],
        compiler_params=pltpu.CompilerParams(
            dimension_semantics=("parallel", "arbitrary"), vmem_limit_bytes=VMEM_LIMIT),
        name="fox_prompt",
    )(qt, ka, vt)


def _bias_piece_rows(c):
    b = c * (-LOG2E)
    p0 = b.astype(BF16).astype(F32)
    r1 = b - p0
    p1 = r1.astype(BF16).astype(F32)
    p2 = (r1 - p1).astype(BF16).astype(F32)
    return jnp.concatenate([p0, p1, p2], axis=0)


def _fox_sample_kernel(q_ref, kn_ref, vn_ref, lfn_ref, kt_ref, vt_ref, lft_ref, o_ref, *, chunk):
    T = q_ref.shape[1]
    P = lft_ref.shape[2]
    HT = FOX_HEADS * T
    t_shift = T.bit_length() - 1
    assert HT == LANES and 1 << t_shift == T

    q_rep = jnp.concatenate([q_ref[0]] * FOX_HEADS, axis=0)
    row_head = _iota((HT, FOX_WIDTH), 0) >> t_shift
    q_bd = jnp.where(row_head == (_iota((HT, FOX_WIDTH), 1) >> 6), q_rep, 0.0).astype(BF16)
    e_lane = _iota((HT, LANES), 1)
    e_bd = jnp.where((e_lane < BIAS_LANES) & ((e_lane & 7) == (_iota((HT, LANES), 0) >> t_shift)),
                     1.0, 0.0).astype(BF16)

    def bias_operand(c):
        rows = _bias_piece_rows(c)
        return jnp.concatenate([rows, jnp.zeros((LANES - BIAS_LANES, c.shape[1]), F32)], axis=0).astype(BF16)

    def running_sum(lf, carry, upper):
        pieces = jnp.concatenate(list(_split3(lf)) + [jnp.zeros(lf.shape, BF16)], axis=0)
        loc = _dot(pieces, upper)
        return loc[0:8] + loc[8:16] + loc[16:24] + carry

    upper = (_iota((chunk, chunk), 0) <= _iota((chunk, chunk), 1)).astype(BF16)
    carry = jnp.zeros((FOX_HEADS, 1), F32)
    c_parts = []
    for ci in range(P // chunk):
        c = running_sum(lft_ref[0, :, ci * chunk:(ci + 1) * chunk], carry, upper)
        carry = c[:, chunk - 1:chunk]
        c_parts.append(c)
    kt = kt_ref[0].reshape(FOX_WIDTH, P).astype(BF16)
    s_old = _dot(q_bd, kt) + _dot(e_bd, bias_operand(jnp.concatenate(c_parts, axis=1)))

    pad_lanes = lambda a: jnp.concatenate([a, jnp.zeros((a.shape[0], LANES - a.shape[1]), F32)], axis=1)
    pad_rows = lambda a: jnp.concatenate([a, jnp.zeros((LANES - T, a.shape[1]), F32)], axis=0)
    upper_new = (_iota((LANES, LANES), 0) <= _iota((LANES, LANES), 1)).astype(BF16)
    c_new = running_sum(pad_lanes(lfn_ref[0]), carry, upper_new)
    s_new = _dot_nt(q_bd, pad_rows(kn_ref[0]).astype(BF16)) + _dot(e_bd, bias_operand(c_new))
    s_new = jnp.where(_iota((HT, LANES), 1) <= (_iota((HT, LANES), 0) & (T - 1)), s_new, -jnp.inf)

    m = jnp.maximum(jnp.max(s_old, axis=1, keepdims=True), jnp.max(s_new, axis=1, keepdims=True))
    p_old = jnp.exp2(s_old - m).astype(BF16)
    p_new = jnp.exp2(s_new - m).astype(BF16)
    l = (jnp.sum(p_old.astype(F32), axis=1, keepdims=True)
         + jnp.sum(p_new.astype(F32), axis=1, keepdims=True))
    r = (_dot_nt(p_old, vt_ref[0].reshape(FOX_WIDTH, P).astype(BF16))
         + _dot(p_new, pad_rows(vn_ref[0]).astype(BF16))) / l
    col_head = _iota((T, FOX_WIDTH), 1) >> 6
    o = jnp.zeros((T, FOX_WIDTH), F32)
    for h in range(FOX_HEADS):
        o = jnp.where(col_head == h, r[h * T:(h + 1) * T, :], o)
    o_ref[0] = o.astype(BF16)


def _fox_sample(q, k_new, v_new, lf_new_t, cache_kt, cache_vt, cache_lft, *, chunk):
    B, T, _ = q.shape
    P = cache_lft.shape[2]
    new = lambda w: pl.BlockSpec((1, T, w), lambda b: (b, 0, 0))
    kv_spec = pl.BlockSpec((1, FOX_HEADS, FOX_HEAD_DIM, P), lambda b: (b, 0, 0, 0))
    return pl.pallas_call(
        functools.partial(_fox_sample_kernel, chunk=chunk),
        grid=(B,),
        in_specs=[new(FOX_WIDTH), new(FOX_WIDTH), new(FOX_WIDTH),
                  pl.BlockSpec((1, FOX_HEADS, T), lambda b: (b, 0, 0)), kv_spec, kv_spec,
                  pl.BlockSpec((1, FOX_HEADS, P), lambda b: (b, 0, 0))],
        out_specs=new(FOX_WIDTH),
        out_shape=jax.ShapeDtypeStruct((B, T, FOX_WIDTH), BF16),
        compiler_params=pltpu.CompilerParams(dimension_semantics=("parallel",),
                                             vmem_limit_bytes=VMEM_LIMIT),
        name="fox_sample",
    )(q, k_new, v_new, lf_new_t, cache_kt, cache_vt, cache_lft)


def _merge_ffn_kernel(x_ref, ofox_ref, opool_ref, omem_ref, ga_ref, wg_ref, wbf_ref, wbp_ref, wbm_ref,
                      wout_ref, gm_ref, wup_ref, wdown_ref, gfin_ref, y_ref, *, ff_chunk, final_norm):
    D = x_ref.shape[-1]
    streams = ofox_ref.shape[1]
    rows = x_ref.shape[1] // streams
    groups = [slice(i * rows, (i + 1) * rows) for i in range(streams)]

    xs = [x_ref[0, r, :] for r in groups]
    pre = []
    for si, (x, r) in enumerate(zip(xs, groups)):
        xn = _rms(x, ga_ref[...]).astype(BF16)
        outs = (ofox_ref[0, si].T.astype(BF16), opool_ref[0, r, :], omem_ref[0, r, :])
        pre.append([(_dot(xn, wg_ref[:, bi * D:(bi + 1) * D]), _dot(o, w_ref[...]))
                    for bi, (o, w_ref) in enumerate(zip(outs, (wbf_ref, wbp_ref, wbm_ref)))])
    hs = []
    for x, terms in zip(xs, pre):
        (g0, v0), (g1, v1), (g2, v2) = terms
        m = jax.nn.sigmoid(g0) * v0 + jax.nn.sigmoid(g1) * v1 + jax.nn.sigmoid(g2) * v2
        hs.append(x + _dot(m.astype(BF16), wout_ref[...]))
    hns = [_rms(h, gm_ref[...]).astype(BF16) for h in hs]
    accs = list(hs)
    for c in range(wup_ref.shape[1] // ff_chunk):
        cols = slice(c * ff_chunk, (c + 1) * ff_chunk)
        ups = [jnp.maximum(_dot(hn, wup_ref[:, cols]), 0.0) for hn in hns]
        accs = [acc + _dot((a * a).astype(BF16), wdown_ref[cols, :]) for acc, a in zip(accs, ups)]
    for acc, r in zip(accs, groups):
        y_ref[0, r, :] = _rms(acc, gfin_ref[...]) if final_norm else acc


def _merge_ffn(x, ofox_t, opool, omem, ga, wg, wbf, wbp, wbm, wout, gm, wup, wdown, gfin, *, tm,
               ff_chunk, final_norm):
    B, S, D = x.shape
    blk = ofox_t.shape[3]
    tile = lambda w: pl.BlockSpec((1, tm, w), lambda b, t: (b, t, 0))
    fox_tile = pl.BlockSpec((1, tm // blk, FOX_WIDTH, blk), lambda b, t: (b, t, 0, 0))
    const = lambda a: pl.BlockSpec(a.shape, lambda b, t: (0,) * a.ndim, pipeline_mode=pl.Buffered(1))
    consts = (ga, wg, wbf, wbp, wbm, wout, gm, wup, wdown, gfin)
    return pl.pallas_call(
        functools.partial(_merge_ffn_kernel, ff_chunk=ff_chunk, final_norm=final_norm),
        grid=(B, S // tm),
        in_specs=[tile(D), fox_tile, tile(POOL_WIDTH), tile(MEM_WIDTH)] + [const(a) for a in consts],
        out_specs=tile(D),
        out_shape=jax.ShapeDtypeStruct((B, S, D), F32),
        compiler_params=pltpu.CompilerParams(dimension_semantics=("parallel", "parallel"),
                                             vmem_limit_bytes=VMEM_LIMIT),
        name="merge_ffn",
    )(x, ofox_t, opool, omem, *consts)


def _tile_rows(n, want):
    return want if n % want == 0 else n


def _repack_w_in(w_in):
    o_f = 3 * FOX_WIDTH
    o_u = o_f + FOX_HEADS
    o_qm = o_u + POOL_WIDTH
    o_g = o_qm + MEM_WIDTH
    b16 = lambda a: a.astype(BF16)
    wq, wk, wv = (w_in[:, i * FOX_WIDTH:(i + 1) * FOX_WIDTH] for i in range(3))
    wf_rep = _rep_lanes(w_in[:, o_f:o_u])
    weights = (b16(wk), b16(wv), b16(w_in[:, o_u:o_qm]), b16(w_in[:, o_qm:o_g]), b16(wf_rep), b16(wq).T)
    return weights, b16(w_in[:, o_g:])


def _rep_lanes(a):
    pad = jnp.zeros(a.shape[:-1] + (LANES - BIAS_LANES,), a.dtype)
    return jnp.concatenate([a] * BIAS_PIECES + [pad], axis=-1)


def kernel(x_prompt, x_sample, mem_prompt, cache_fox_k, cache_fox_v, cache_fox_logf, cache_pool,
           cache_mem_k, cache_mem_v, g_attn, w_in, b_f, w_pool, s_pool, g_mem, w_mkv,
           w_br_fox, w_br_pool, w_br_mem, w_out, g_mlp, w_up, w_down, g_final):
    depth = w_in.shape[0]
    B, S, D = x_prompt.shape
    BS, T, _ = x_sample.shape
    P = cache_fox_k.shape[2]
    M = mem_prompt.shape[1]
    tm_p = _tile_rows(S, 512)
    blk_p = _tile_rows(tm_p, 256)
    tm_s = _tile_rows(BS * T, 256)
    chunk_s = _tile_rows(P, 256)
    gfin = g_final.reshape(1, D)

    xp, xs = x_prompt, x_sample
    outs_p = [[] for _ in range(6)]
    outs_s = [[] for _ in range(4)]
    for l in range(depth):
        weights, wg = _repack_w_in(w_in[l])
        bf_rep = _rep_lanes(b_f[l]).reshape(1, LANES)
        g_a, g_m, g_f = g_attn[l].reshape(1, D), g_mem[l].reshape(1, D), g_mlp[l].reshape(1, D)
        wpool, spool = w_pool[l].astype(BF16), s_pool[l].reshape(1, POOL_WIDTH)
        wbf, wbp, wbm = (w.astype(BF16) for w in (w_br_fox[l], w_br_pool[l], w_br_mem[l]))
        wout, wup, wdown = (w.astype(BF16) for w in (w_out[l], w_up[l], w_down[l]))
        final = l == depth - 1

        mk, mv = _mem_kv(mem_prompt, g_m, w_mkv[l].astype(BF16))
        hist0 = jnp.zeros((B, HIST_ROWS, POOL_WIDTH), F32)
        k, v, logf_t, pnew, opool, omem, qt, ka, vt = _in_proj(
            xp, g_a, weights, bf_rep, hist0, mk, mv, wpool, spool, tm=tm_p, pos0=0, augment=True,
            blk=blk_p)
        logf = jnp.transpose(logf_t, (0, 2, 1))
        ofox = _fox_prompt(qt, ka, vt, blk=blk_p)
        post = functools.partial(_merge_ffn, ga=g_a, wg=wg, wbf=wbf, wbp=wbp, wbm=wbm, wout=wout, gm=g_f,
                                 wup=wup, wdown=wdown, gfin=gfin, ff_chunk=1024, final_norm=final)
        xp = post(xp, ofox, opool, omem, tm=tm_p)
        for acc, val in zip(outs_p, (k.reshape(B, S, FOX_HEADS, FOX_HEAD_DIM),
                                     v.reshape(B, S, FOX_HEADS, FOX_HEAD_DIM), logf, pnew,
                                     mk.reshape(B, M, MEM_HEADS, MEM_HEAD_DIM),
                                     mv.reshape(B, M, MEM_HEADS, MEM_HEAD_DIM))):
            acc.append(val)

        hist = jnp.pad(cache_pool[l], ((0, 0), (HIST_ROWS - POOL_HIST, 0), (0, 0)))
        cmk = cache_mem_k[l].reshape(BS, M * MEM_HEADS, MEM_HEAD_DIM)
        cmv = cache_mem_v[l].reshape(BS, M * MEM_HEADS, MEM_HEAD_DIM)
        k, v, logf_t, pnew, opool, omem, q = _in_proj(
            xs, g_a, weights, bf_rep, hist, cmk, cmv, wpool, spool, tm=T, pos0=P, augment=False)
        logf = jnp.transpose(logf_t, (0, 2, 1))
        ofox = _fox_sample(q, k, v, logf_t,
                           jnp.transpose(cache_fox_k[l], (0, 2, 3, 1)), jnp.transpose(cache_fox_v[l], (0, 2, 3, 1)),
                           jnp.transpose(cache_fox_logf[l], (0, 2, 1)), chunk=chunk_s)
        flat = lambda a: a.reshape(1, BS * T, a.shape[-1])
        ofox_t = jnp.transpose(ofox.reshape(BS * T // tm_s, tm_s, FOX_WIDTH), (0, 2, 1))[None].astype(F32)
        xs = post(flat(xs), ofox_t, flat(opool), flat(omem), tm=tm_s).reshape(BS, T, D)
        for acc, val in zip(outs_s, (k.reshape(BS, T, FOX_HEADS, FOX_HEAD_DIM),
                                     v.reshape(BS, T, FOX_HEADS, FOX_HEAD_DIM), logf, pnew)):
            acc.append(val)

    stack = lambda o: o[0][None] if depth == 1 else jnp.stack(o)
    return (xp, xs, *(stack(o) for o in outs_p), *(stack(o) for o in outs_s))
```

```python
import functools
import math

import jax
import jax.numpy as jnp
from jax import lax
from jax.experimental import pallas as pl
from jax.experimental.pallas import tpu as pltpu

F32 = jnp.float32
BF16 = jnp.bfloat16

EPS = 1e-6
LANES = 128
FOX_HEADS = 8
FOX_HEAD_DIM = 64
FOX_WIDTH = FOX_HEADS * FOX_HEAD_DIM
POOL_WINDOWS = (2, 4, 8, 16)
POOL_GROUPS = 4
POOL_GROUP_DIM = 128
POOL_WIDTH = POOL_GROUPS * POOL_GROUP_DIM
POOL_HIST = 15
HIST_ROWS = 16
MEM_HEADS = 4
MEM_HEAD_DIM = 128
MEM_WIDTH = MEM_HEADS * MEM_HEAD_DIM
N_BRANCH = 3
LOG2E = math.log2(math.e)
BIAS_PIECES = 3
BIAS_LANES = BIAS_PIECES * FOX_HEADS
V_SLAB = 80
KEY_BLOCKS_PER_ITER = 4
KEY_BLOCKS_PER_ITEM = 1
SCORES_AHEAD = 6
CUMSUM_CHUNK = 128
VMEM_LIMIT = 56 * 1024 * 1024


def _dot(a, b):
    return jnp.dot(a, b, preferred_element_type=F32)


def _dot_nt(a, b):
    return lax.dot_general(a, b, (((1,), (1,)), ((), ())), preferred_element_type=F32)


def _rms(x, g):
    return x * lax.rsqrt(jnp.mean(x * x, axis=-1, keepdims=True) + EPS) * g


def _iota(shape, dim):
    return lax.broadcasted_iota(jnp.int32, shape, dim)


def _split3(x):
    p0 = x.astype(BF16)
    r1 = x - p0.astype(F32)
    p1 = r1.astype(BF16)
    p2 = (r1 - p1.astype(F32)).astype(BF16)
    return p0, p1, p2


def _cumsum_rows(x, carry):
    n = x.shape[0]
    ch = min(n, CUMSUM_CHUNK)
    tril = (_iota((ch, ch), 1) <= _iota((ch, ch), 0)).astype(BF16)
    outs = []
    for c0 in range(0, n, ch):
        r = _dot(tril, jnp.concatenate(_split3(x[c0:c0 + ch]), axis=1))
        c = r[:, :LANES] + r[:, LANES:2 * LANES] + r[:, 2 * LANES:] + carry
        carry = c[ch - 1:ch]
        outs.append(c)
    return (outs[0] if len(outs) == 1 else jnp.concatenate(outs, axis=0)), carry


def _bias_pieces(c):
    b = c * (-LOG2E)
    p0 = b.astype(BF16).astype(F32)
    r1 = b - p0
    p1 = r1.astype(BF16).astype(F32)
    p2 = (r1 - p1).astype(BF16).astype(F32)
    grp = _iota(c.shape, 1) >> 3
    return jnp.where(grp == 0, p0, jnp.where(grp == 1, p1, jnp.where(grp == 2, p2, 0.0)))


def _mem_kv_kernel(mem_ref, g_ref, w_ref, mk_ref, mv_ref):
    M = mem_ref.shape[1]
    xn = _rms(mem_ref[0], g_ref[...]).astype(BF16)
    for out_ref, off in ((mk_ref, 0), (mv_ref, MEM_WIDTH)):
        kv = _dot(xn, w_ref[:, off:off + MEM_WIDTH])
        for h in range(MEM_HEADS):
            out_ref[0, pl.ds(h, M, stride=MEM_HEADS), :] = kv[:, h * MEM_HEAD_DIM:(h + 1) * MEM_HEAD_DIM]


def _mem_kv(mem, g, w_mkv):
    B, M, D = mem.shape
    return pl.pallas_call(
        _mem_kv_kernel,
        grid=(B,),
        in_specs=[pl.BlockSpec((1, M, D), lambda b: (b, 0, 0)), pl.BlockSpec((1, D), lambda b: (0, 0)),
                  pl.BlockSpec((D, 2 * MEM_WIDTH), lambda b: (0, 0))],
        out_specs=[pl.BlockSpec((1, M * MEM_HEADS, MEM_HEAD_DIM), lambda b: (b, 0, 0))] * 2,
        out_shape=[jax.ShapeDtypeStruct((B, M * MEM_HEADS, MEM_HEAD_DIM), F32)] * 2,
        compiler_params=pltpu.CompilerParams(dimension_semantics=("parallel",),
                                             vmem_limit_bytes=VMEM_LIMIT),
        name="mem_kv",
    )(mem, g, w_mkv)


def _in_proj_kernel(x_ref, g_ref, wk_ref, wv_ref, wu_ref, wqm_ref, wf_ref, wqt_ref, bf_ref, hist_ref,
                    mk_ref, mv_ref, wpool_ref, spool_ref,
                    k_ref, v_ref, logf_ref, pnew_ref, opool_ref, omem_ref, *rest, tm, pos0, augment, blk):
    extra, (uext_ref, carry_ref) = rest[:-2], rest[-2:]
    t = pl.program_id(1)
    last = pl.num_programs(1) - 1
    q_scale = FOX_HEAD_DIM ** -0.5 * LOG2E

    @pl.when(t == 0)
    def _():
        uext_ref[0:HIST_ROWS, :] = hist_ref[0]
        carry_ref[...] = jnp.zeros_like(carry_ref)

    xn = _rms(x_ref[0], g_ref[...]).astype(BF16)
    lane = _iota((tm, LANES), 1)
    mem_cols = [slice(h * MEM_HEAD_DIM, (h + 1) * MEM_HEAD_DIM) for h in range(MEM_HEADS)]

    zf = _dot(xn, wf_ref[...]) + bf_ref[...]
    qm = _dot(xn, wqm_ref[...]).astype(BF16)
    k = _dot(xn, wk_ref[...])
    logf = jnp.minimum(zf, 0.0) - jnp.log1p(jnp.exp(-jnp.abs(zf)))
    n_mem = mk_ref.shape[1] // MEM_HEADS
    mem_head = lambda ref, h: ref[0, pl.ds(h, n_mem, stride=MEM_HEADS), :].astype(BF16)
    mem_s = [_dot_nt(qm[:, cols], mem_head(mk_ref, h)) * MEM_HEAD_DIM ** -0.5
             for h, cols in enumerate(mem_cols)]
    v = _dot(xn, wv_ref[...])
    u = _dot(xn, wu_ref[...])
    if augment:
        c, carry_ref[...] = _cumsum_rows(logf, carry_ref[...])
        qt_all = (_dot_nt(wqt_ref[...], xn) * q_scale).astype(BF16)
    else:
        q = _dot_nt(xn, wqt_ref[...]) * q_scale
    for h, (s, cols) in enumerate(zip(mem_s, mem_cols)):
        e = jnp.exp(s - jnp.max(s, axis=-1, keepdims=True))
        o = _dot(e.astype(BF16), mem_head(mv_ref, h)) / jnp.sum(e, axis=-1, keepdims=True)
        omem_ref[0, :, cols] = o.astype(BF16)
    if augment:
        vt_all = v.T.astype(BF16)

    k_ref[0] = k
    v_ref[0] = v
    logf_rows = logf if tm >= LANES else jnp.concatenate([logf, jnp.zeros((LANES - tm, LANES), F32)], axis=0)
    logf_ref[0] = logf_rows.T[:FOX_HEADS, :tm]

    if augment:
        qt_ref, ka_ref, vt_ref = extra
        bias = pltpu.roll(_bias_pieces(c), FOX_HEAD_DIM, axis=1)
        in_bias = (lane >= FOX_HEAD_DIM) & (lane < FOX_HEAD_DIM + BIAS_LANES)
        for pair in range(FOX_HEADS // 2):
            even = k[:, pair * LANES:(pair + 1) * LANES]
            odd = pltpu.roll(even, FOX_HEAD_DIM, axis=1)
            for par, data in ((0, even), (1, odd)):
                h = 2 * pair + par
                tail = jnp.where(in_bias & ((lane & 7) == h), bias, 0.0)
                ka_ref[0, :, h * LANES:(h + 1) * LANES] = jnp.where(lane < FOX_HEAD_DIM, data, tail).astype(BF16)
        q_tail = jnp.where(_iota((LANES - FOX_HEAD_DIM, blk), 0) < BIAS_LANES, 1.0, 0.0).astype(BF16)
        v_tail = jnp.where(_iota((V_SLAB - FOX_HEAD_DIM, blk), 0) == 0, 1.0, 0.0).astype(BF16)
        for h in range(FOX_HEADS):
            qt = qt_all[h * FOX_HEAD_DIM:(h + 1) * FOX_HEAD_DIM]
            vt = vt_all[h * FOX_HEAD_DIM:(h + 1) * FOX_HEAD_DIM]
            for j in range(tm // blk):
                qt_ref[0, j, h * LANES:h * LANES + FOX_HEAD_DIM, :] = qt[:, j * blk:(j + 1) * blk]
                qt_ref[0, j, h * LANES + FOX_HEAD_DIM:(h + 1) * LANES, :] = q_tail
                vt_ref[0, j, h * V_SLAB:h * V_SLAB + FOX_HEAD_DIM, :] = vt[:, j * blk:(j + 1) * blk]
                vt_ref[0, j, h * V_SLAB + FOX_HEAD_DIM:(h + 1) * V_SLAB, :] = v_tail
    else:
        extra[0][0] = q

    uext_ref[HIST_ROWS:HIST_ROWS + tm, :] = u
    pos = pos0 + t * tm + _iota((tm, 1), 0)
    for gi, w in enumerate(POOL_WINDOWS):
        cols = slice(gi * POOL_GROUP_DIM, (gi + 1) * POOL_GROUP_DIM)
        sw = uext_ref[:, cols]
        shift = 1
        while shift < w:
            sw = sw + pltpu.roll(sw, shift, axis=0)
            shift *= 2
        cnt = jnp.minimum(w, pos + 1).astype(F32)
        pooled = sw[HIST_ROWS:, :] / cnt - u[:, cols]
        mixed = _dot(pooled.astype(BF16), wpool_ref[gi]) * spool_ref[:, cols]
        opool_ref[0, :, cols] = mixed.astype(BF16)

    @pl.when(t == last)
    def _():
        pnew_ref[0] = uext_ref[tm + 1:tm + HIST_ROWS, :]

    uext_ref[0:HIST_ROWS, :] = uext_ref[tm:tm + HIST_ROWS, :]


def _in_proj(x, g, weights, bf_rep, hist, mk, mv, wpool, spool, *, tm, pos0, augment, blk=None):
    B, S, D = x.shape
    nt = S // tm
    tile = lambda w: pl.BlockSpec((1, tm, w), lambda b, t: (b, t, 0))
    per_b = lambda r, w: pl.BlockSpec((1, r, w), lambda b, t: (b, 0, 0))
    const = lambda shape: pl.BlockSpec(shape, lambda b, t: (0,) * len(shape))
    if augment:
        W = FOX_HEADS * LANES
        t_tile = lambda rows: pl.BlockSpec((1, tm // blk, rows, blk), lambda b, t: (b, t, 0, 0))
        abc_specs = [t_tile(W), tile(W), t_tile(FOX_HEADS * V_SLAB)]
        abc_shapes = [jax.ShapeDtypeStruct((B, S // blk, W, blk), BF16),
                      jax.ShapeDtypeStruct((B, S, W), BF16),
                      jax.ShapeDtypeStruct((B, S // blk, FOX_HEADS * V_SLAB, blk), BF16)]
    else:
        abc_specs = [tile(FOX_WIDTH)]
        abc_shapes = [jax.ShapeDtypeStruct((B, S, FOX_WIDTH), F32)]
    return pl.pallas_call(
        functools.partial(_in_proj_kernel, tm=tm, pos0=pos0, augment=augment, blk=blk),
        grid=(B, nt),
        in_specs=[tile(D), const((1, D))] + [const(w.shape) for w in weights] + [
            const((1, LANES)), per_b(HIST_ROWS, POOL_WIDTH), per_b(*mk.shape[1:]), per_b(*mv.shape[1:]),
            const((POOL_GROUPS, POOL_GROUP_DIM, POOL_GROUP_DIM)), const((1, POOL_WIDTH))],
        out_specs=[tile(FOX_WIDTH), tile(FOX_WIDTH), pl.BlockSpec((1, FOX_HEADS, tm), lambda b, t: (b, 0, t)),
                   per_b(POOL_HIST, POOL_WIDTH), tile(POOL_WIDTH), tile(MEM_WIDTH)] + abc_specs,
        out_shape=[jax.ShapeDtypeStruct((B, S, FOX_WIDTH), F32),
                   jax.ShapeDtypeStruct((B, S, FOX_WIDTH), F32),
                   jax.ShapeDtypeStruct((B, FOX_HEADS, S), F32),
                   jax.ShapeDtypeStruct((B, POOL_HIST, POOL_WIDTH), F32),
                   jax.ShapeDtypeStruct((B, S, POOL_WIDTH), BF16),
                   jax.ShapeDtypeStruct((B, S, MEM_WIDTH), BF16)] + abc_shapes,
        scratch_shapes=[pltpu.VMEM((HIST_ROWS + tm, POOL_WIDTH), F32), pltpu.VMEM((1, LANES), F32)],
        compiler_params=pltpu.CompilerParams(dimension_semantics=("parallel", "arbitrary"),
                                             vmem_limit_bytes=VMEM_LIMIT),
        name="in_proj",
    )(x, g, *weights, bf_rep, hist, mk, mv, wpool, spool)


def _fox_prompt_kernel(qt_ref, k_ref, vt_ref, o_ref, m_ref, acc_ref, *, blk):
    i = pl.program_id(1)
    m_ref[...] = jnp.full_like(m_ref, -jnp.inf)
    acc_ref[...] = jnp.zeros_like(acc_ref)
    slabs = [slice(h * LANES, (h + 1) * LANES) for h in range(FOX_HEADS)]

    def scores(j, nb, h):
        start = pl.multiple_of(j * blk, blk)
        return _dot(k_ref[0, pl.ds(start, nb * blk), slabs[h]], qt_ref[0, 0, slabs[h], :])

    def accumulate(j, nb, h, st, masked):
        if masked:
            st = jnp.where(_iota(st.shape, 0) <= _iota(st.shape, 1), st, -jnp.inf)
        m_prev = m_ref[h]
        m_new = jnp.maximum(m_prev, jnp.max(st, axis=0, keepdims=True))
        p = jnp.exp2(st - m_new).astype(BF16)
        pv = _dot(vt_ref[0, j, h * V_SLAB:(h + 1) * V_SLAB, :], p[:blk])
        for r in range(1, nb):
            pv = pv + _dot(vt_ref[0, j + r, h * V_SLAB:(h + 1) * V_SLAB, :], p[r * blk:(r + 1) * blk])
        acc_ref[h] = jnp.exp2(m_prev - m_new) * acc_ref[h] + pv
        m_ref[h] = m_new

    def run(first, n_visible):
        spans = [(r, min(KEY_BLOCKS_PER_ITEM, n_visible - r), False)
                 for r in range(0, n_visible, KEY_BLOCKS_PER_ITEM)]
        if n_visible < KEY_BLOCKS_PER_ITER:
            spans.append((n_visible, 1, True))
        work = [(first + r, nb, h, masked) for r, nb, masked in spans for h in range(FOX_HEADS)]
        pending = [scores(*w[:3]) for w in work[:SCORES_AHEAD]]
        for n, (j, nb, h, masked) in enumerate(work):
            if n + SCORES_AHEAD < len(work):
                pending.append(scores(*work[n + SCORES_AHEAD][:3]))
            accumulate(j, nb, h, pending.pop(0), masked)

    shift = KEY_BLOCKS_PER_ITER.bit_length() - 1
    n_iter = i >> shift

    def body(jj, carry):
        run(KEY_BLOCKS_PER_ITER * jj, KEY_BLOCKS_PER_ITER)
        return carry

    lax.fori_loop(0, n_iter, body, 0)
    for rem in range(KEY_BLOCKS_PER_ITER):
        @pl.when(i - (n_iter << shift) == rem)
        def _(rem=rem):
            run(i - rem, rem)

    for h in range(FOX_HEADS):
        acc = acc_ref[h]
        o_ref[0, 0, h * FOX_HEAD_DIM:(h + 1) * FOX_HEAD_DIM, :] = (
            acc[:FOX_HEAD_DIM] / acc[FOX_HEAD_DIM:FOX_HEAD_DIM + 1])


def _fox_prompt(qt, ka, vt, *, blk):
    B, S, W = ka.shape
    nblk = S // blk
    return pl.pallas_call(
        functools.partial(_fox_prompt_kernel, blk=blk),
        grid=(B, nblk),
        in_specs=[pl.BlockSpec((1, 1, W, blk), lambda b, i: (b, i, 0, 0)),
                  pl.BlockSpec((1, S, W), lambda b, i: (b, 0, 0)),
                  pl.BlockSpec((1, nblk, FOX_HEADS * V_SLAB, blk), lambda b, i: (b, 0, 0, 0))],
        out_specs=pl.BlockSpec((1, 1, FOX_WIDTH, blk), lambda b, i: (b, i, 0, 0)),
        out_shape=jax.ShapeDtypeStruct((B, nblk, FOX_WIDTH, blk), F32),
        scratch_shapes=[pltpu.VMEM((FOX_HEADS, 1, blk), F32), pltpu.VMEM((FOX_HEADS, V_SLAB, blk), F32)],
        compiler_params=pltpu.CompilerParams(
            dimension_semantics=("parallel", "arbitrary"), vmem_limit_bytes=VMEM_LIMIT),
        name="fox_prompt",
    )(qt, ka, vt)


def _bias_piece_rows(c):
    b = c * (-LOG2E)
    p0 = b.astype(BF16).astype(F32)
    r1 = b - p0
    p1 = r1.astype(BF16).astype(F32)
    p2 = (r1 - p1).astype(BF16).astype(F32)
    return jnp.concatenate([p0, p1, p2], axis=0)


def _fox_sample_kernel(q_ref, kn_ref, vn_ref, lfn_ref, kt_ref, vt_ref, lft_ref, o_ref, *, chunk):
    T = q_ref.shape[1]
    P = lft_ref.shape[2]
    HT = FOX_HEADS * T
    t_shift = T.bit_length() - 1
    assert HT == LANES and 1 << t_shift == T

    q_rep = jnp.concatenate([q_ref[0]] * FOX_HEADS, axis=0)
    row_head = _iota((HT, FOX_WIDTH), 0) >> t_shift
    q_bd = jnp.where(row_head == (_iota((HT, FOX_WIDTH), 1) >> 6), q_rep, 0.0).astype(BF16)
    e_lane = _iota((HT, LANES), 1)
    e_bd = jnp.where((e_lane < BIAS_LANES) & ((e_lane & 7) == (_iota((HT, LANES), 0) >> t_shift)),
                     1.0, 0.0).astype(BF16)

    def bias_operand(c):
        rows = _bias_piece_rows(c)
        return jnp.concatenate([rows, jnp.zeros((LANES - BIAS_LANES, c.shape[1]), F32)], axis=0).astype(BF16)

    def running_sum(lf, carry, upper):
        pieces = jnp.concatenate(list(_split3(lf)) + [jnp.zeros(lf.shape, BF16)], axis=0)
        loc = _dot(pieces, upper)
        return loc[0:8] + loc[8:16] + loc[16:24] + carry

    upper = (_iota((chunk, chunk), 0) <= _iota((chunk, chunk), 1)).astype(BF16)
    carry = jnp.zeros((FOX_HEADS, 1), F32)
    c_parts = []
    for ci in range(P // chunk):
        c = running_sum(lft_ref[0, :, ci * chunk:(ci + 1) * chunk], carry, upper)
        carry = c[:, chunk - 1:chunk]
        c_parts.append(c)
    kt = kt_ref[0].reshape(FOX_WIDTH, P).astype(BF16)
    s_old = _dot(q_bd, kt) + _dot(e_bd, bias_operand(jnp.concatenate(c_parts, axis=1)))

    pad_lanes = lambda a: jnp.concatenate([a, jnp.zeros((a.shape[0], LANES - a.shape[1]), F32)], axis=1)
    pad_rows = lambda a: jnp.concatenate([a, jnp.zeros((LANES - T, a.shape[1]), F32)], axis=0)
    upper_new = (_iota((LANES, LANES), 0) <= _iota((LANES, LANES), 1)).astype(BF16)
    c_new = running_sum(pad_lanes(lfn_ref[0]), carry, upper_new)
    s_new = _dot_nt(q_bd, pad_rows(kn_ref[0]).astype(BF16)) + _dot(e_bd, bias_operand(c_new))
    s_new = jnp.where(_iota((HT, LANES), 1) <= (_iota((HT, LANES), 0) & (T - 1)), s_new, -jnp.inf)

    m = jnp.maximum(jnp.max(s_old, axis=1, keepdims=True), jnp.max(s_new, axis=1, keepdims=True))
    p_old = jnp.exp2(s_old - m).astype(BF16)
    p_new = jnp.exp2(s_new - m).astype(BF16)
    l = (jnp.sum(p_old.astype(F32), axis=1, keepdims=True)
         + jnp.sum(p_new.astype(F32), axis=1, keepdims=True))
    r = (_dot_nt(p_old, vt_ref[0].reshape(FOX_WIDTH, P).astype(BF16))
         + _dot(p_new, pad_rows(vn_ref[0]).astype(BF16))) / l
    col_head = _iota((T, FOX_WIDTH), 1) >> 6
    o = jnp.zeros((T, FOX_WIDTH), F32)
    for h in range(FOX_HEADS):
        o = jnp.where(col_head == h, r[h * T:(h + 1) * T, :], o)
    o_ref[0] = o.astype(BF16)


def _fox_sample(q, k_new, v_new, lf_new_t, cache_kt, cache_vt, cache_lft, *, chunk):
    B, T, _ = q.shape
    P = cache_lft.shape[2]
    new = lambda w: pl.BlockSpec((1, T, w), lambda b: (b, 0, 0))
    kv_spec = pl.BlockSpec((1, FOX_HEADS, FOX_HEAD_DIM, P), lambda b: (b, 0, 0, 0))
    return pl.pallas_call(
        functools.partial(_fox_sample_kernel, chunk=chunk),
        grid=(B,),
        in_specs=[new(FOX_WIDTH), new(FOX_WIDTH), new(FOX_WIDTH),
                  pl.BlockSpec((1, FOX_HEADS, T), lambda b: (b, 0, 0)), kv_spec, kv_spec,
                  pl.BlockSpec((1, FOX_HEADS, P), lambda b: (b, 0, 0))],
        out_specs=new(FOX_WIDTH),
        out_shape=jax.ShapeDtypeStruct((B, T, FOX_WIDTH), BF16),
        compiler_params=pltpu.CompilerParams(dimension_semantics=("parallel",),
                                             vmem_limit_bytes=VMEM_LIMIT),
        name="fox_sample",
    )(q, k_new, v_new, lf_new_t, cache_kt, cache_vt, cache_lft)


def _merge_ffn_kernel(x_ref, ofox_ref, opool_ref, omem_ref, ga_ref, wg_ref, wbf_ref, wbp_ref, wbm_ref,
                      wout_ref, gm_ref, wup_ref, wdown_ref, gfin_ref, y_ref, *, ff_chunk, final_norm):
    D = x_ref.shape[-1]
    streams = ofox_ref.shape[1]
    rows = x_ref.shape[1] // streams
    groups = [slice(i * rows, (i + 1) * rows) for i in range(streams)]

    xs = [x_ref[0, r, :] for r in groups]
    pre = []
    for si, (x, r) in enumerate(zip(xs, groups)):
        xn = _rms(x, ga_ref[...]).astype(BF16)
        outs = (ofox_ref[0, si].T.astype(BF16), opool_ref[0, r, :], omem_ref[0, r, :])
        pre.append([(_dot(xn, wg_ref[:, bi * D:(bi + 1) * D]), _dot(o, w_ref[...]))
                    for bi, (o, w_ref) in enumerate(zip(outs, (wbf_ref, wbp_ref, wbm_ref)))])
    hs = []
    for x, terms in zip(xs, pre):
        (g0, v0), (g1, v1), (g2, v2) = terms
        m = jax.nn.sigmoid(g0) * v0 + jax.nn.sigmoid(g1) * v1 + jax.nn.sigmoid(g2) * v2
        hs.append(x + _dot(m.astype(BF16), wout_ref[...]))
    hns = [_rms(h, gm_ref[...]).astype(BF16) for h in hs]
    accs = list(hs)
    for c in range(wup_ref.shape[1] // ff_chunk):
        cols = slice(c * ff_chunk, (c + 1) * ff_chunk)
        ups = [jnp.maximum(_dot(hn, wup_ref[:, cols]), 0.0) for hn in hns]
        accs = [acc + _dot((a * a).astype(BF16), wdown_ref[cols, :]) for acc, a in zip(accs, ups)]
    for acc, r in zip(accs, groups):
        y_ref[0, r, :] = _rms(acc, gfin_ref[...]) if final_norm else acc


def _merge_ffn(x, ofox_t, opool, omem, ga, wg, wbf, wbp, wbm, wout, gm, wup, wdown, gfin, *, tm,
               ff_chunk, final_norm):
    B, S, D = x.shape
    blk = ofox_t.shape[3]
    tile = lambda w: pl.BlockSpec((1, tm, w), lambda b, t: (b, t, 0))
    fox_tile = pl.BlockSpec((1, tm // blk, FOX_WIDTH, blk), lambda b, t: (b, t, 0, 0))
    const = lambda a: pl.BlockSpec(a.shape, lambda b, t: (0,) * a.ndim, pipeline_mode=pl.Buffered(1))
    consts = (ga, wg, wbf, wbp, wbm, wout, gm, wup, wdown, gfin)
    return pl.pallas_call(
        functools.partial(_merge_ffn_kernel, ff_chunk=ff_chunk, final_norm=final_norm),
        grid=(B, S // tm),
        in_specs=[tile(D), fox_tile, tile(POOL_WIDTH), tile(MEM_WIDTH)] + [const(a) for a in consts],
        out_specs=tile(D),
        out_shape=jax.ShapeDtypeStruct((B, S, D), F32),
        compiler_params=pltpu.CompilerParams(dimension_semantics=("parallel", "parallel"),
                                             vmem_limit_bytes=VMEM_LIMIT),
        name="merge_ffn",
    )(x, ofox_t, opool, omem, *consts)


def _tile_rows(n, want):
    return want if n % want == 0 else n


def _repack_w_in(w_in):
    o_f = 3 * FOX_WIDTH
    o_u = o_f + FOX_HEADS
    o_qm = o_u + POOL_WIDTH
    o_g = o_qm + MEM_WIDTH
    b16 = lambda a: a.astype(BF16)
    wq, wk, wv = (w_in[:, i * FOX_WIDTH:(i + 1) * FOX_WIDTH] for i in range(3))
    wf_rep = _rep_lanes(w_in[:, o_f:o_u])
    weights = (b16(wk), b16(wv), b16(w_in[:, o_u:o_qm]), b16(w_in[:, o_qm:o_g]), b16(wf_rep), b16(wq).T)
    return weights, b16(w_in[:, o_g:])


def _rep_lanes(a):
    pad = jnp.zeros(a.shape[:-1] + (LANES - BIAS_LANES,), a.dtype)
    return jnp.concatenate([a] * BIAS_PIECES + [pad], axis=-1)


def kernel(x_prompt, x_sample, mem_prompt, cache_fox_k, cache_fox_v, cache_fox_logf, cache_pool,
           cache_mem_k, cache_mem_v, g_attn, w_in, b_f, w_pool, s_pool, g_mem, w_mkv,
           w_br_fox, w_br_pool, w_br_mem, w_out, g_mlp, w_up, w_down, g_final):
    depth = w_in.shape[0]
    B, S, D = x_prompt.shape
    BS, T, _ = x_sample.shape
    P = cache_fox_k.shape[2]
    M = mem_prompt.shape[1]
    tm_p = _tile_rows(S, 512)
    blk_p = _tile_rows(tm_p, 256)
    tm_s = _tile_rows(BS * T, 256)
    chunk_s = _tile_rows(P, 256)
    gfin = g_final.reshape(1, D)

    xp, xs = x_prompt, x_sample
    outs_p = [[] for _ in range(6)]
    outs_s = [[] for _ in range(4)]
    for l in range(depth):
        weights, wg = _repack_w_in(w_in[l])
        bf_rep = _rep_lanes(b_f[l]).reshape(1, LANES)
        g_a, g_m, g_f = g_attn[l].reshape(1, D), g_mem[l].reshape(1, D), g_mlp[l].reshape(1, D)
        wpool, spool = w_pool[l].astype(BF16), s_pool[l].reshape(1, POOL_WIDTH)
        wbf, wbp, wbm = (w.astype(BF16) for w in (w_br_fox[l], w_br_pool[l], w_br_mem[l]))
        wout, wup, wdown = (w.astype(BF16) for w in (w_out[l], w_up[l], w_down[l]))
        final = l == depth - 1
        post = functools.partial(_merge_ffn, ga=g_a, wg=wg, wbf=wbf, wbp=wbp, wbm=wbm, wout=wout, gm=g_f,
                                 wup=wup, wdown=wdown, gfin=gfin, ff_chunk=1024, final_norm=final)

        hist = jnp.pad(cache_pool[l], ((0, 0), (HIST_ROWS - POOL_HIST, 0), (0, 0)))
        cmk = cache_mem_k[l].reshape(BS, M * MEM_HEADS, MEM_HEAD_DIM)
        cmv = cache_mem_v[l].reshape(BS, M * MEM_HEADS, MEM_HEAD_DIM)
        k, v, logf_t, pnew, opool, omem, q = _in_proj(
            xs, g_a, weights, bf_rep, hist, cmk, cmv, wpool, spool, tm=T, pos0=P, augment=False)
        logf = jnp.transpose(logf_t, (0, 2, 1))
        ofox = _fox_sample(q, k, v, logf_t,
                           jnp.transpose(cache_fox_k[l], (0, 2, 3, 1)), jnp.transpose(cache_fox_v[l], (0, 2, 3, 1)),
                           jnp.transpose(cache_fox_logf[l], (0, 2, 1)), chunk=chunk_s)
        flat = lambda a: a.reshape(1, BS * T, a.shape[-1])
        ofox_t = jnp.transpose(ofox.reshape(BS * T // tm_s, tm_s, FOX_WIDTH), (0, 2, 1))[None].astype(F32)
        xs = post(flat(xs), ofox_t, flat(opool), flat(omem), tm=tm_s).reshape(BS, T, D)
        for acc, val in zip(outs_s, (k.reshape(BS, T, FOX_HEADS, FOX_HEAD_DIM),
                                     v.reshape(BS, T, FOX_HEADS, FOX_HEAD_DIM), logf, pnew)):
            acc.append(val)

        mk, mv = _mem_kv(mem_prompt, g_m, w_mkv[l].astype(BF16))
        hist0 = jnp.zeros((B, HIST_ROWS, POOL_WIDTH), F32)
        k, v, logf_t, pnew, opool, omem, qt, ka, vt = _in_proj(
            xp, g_a, weights, bf_rep, hist0, mk, mv, wpool, spool, tm=tm_p, pos0=0, augment=True,
            blk=blk_p)
        logf = jnp.transpose(logf_t, (0, 2, 1))
        ofox = _fox_prompt(qt, ka, vt, blk=blk_p)
        xp = post(xp, ofox, opool, omem, tm=tm_p)
        for acc, val in zip(outs_p, (k.reshape(B, S, FOX_HEADS, FOX_HEAD_DIM),
                                     v.reshape(B, S, FOX_HEADS, FOX_HEAD_DIM), logf, pnew,
                                     mk.reshape(B, M, MEM_HEADS, MEM_HEAD_DIM),
                                     mv.reshape(B, M, MEM_HEADS, MEM_HEAD_DIM))):
            acc.append(val)

    stack = lambda o: o[0][None] if depth == 1 else jnp.stack(o)
    return (xp, xs, *(stack(o) for o in outs_p), *(stack(o) for o in outs_s))
```

```python
import functools
import math

import jax
import jax.numpy as jnp
from jax import lax
from jax.experimental import pallas as pl
from jax.experimental.pallas import tpu as pltpu

F32 = jnp.float32
BF16 = jnp.bfloat16

EPS = 1e-6
LANES = 128
FOX_HEADS = 8
FOX_HEAD_DIM = 64
FOX_WIDTH = FOX_HEADS * FOX_HEAD_DIM
POOL_WINDOWS = (2, 4, 8, 16)
POOL_GROUPS = 4
POOL_GROUP_DIM = 128
POOL_WIDTH = POOL_GROUPS * POOL_GROUP_DIM
POOL_HIST = 15
HIST_ROWS = 16
MEM_HEADS = 4
MEM_HEAD_DIM = 128
MEM_WIDTH = MEM_HEADS * MEM_HEAD_DIM
N_BRANCH = 3
LOG2E = math.log2(math.e)
BIAS_PIECES = 3
BIAS_LANES = BIAS_PIECES * FOX_HEADS
V_SLAB = 80
KEY_BLOCKS_PER_ITER = 4
KEY_BLOCKS_PER_ITEM = 1
SCORES_AHEAD = 6
CUMSUM_CHUNK = 128
VMEM_LIMIT = 56 * 1024 * 1024


def _dot(a, b):
    return jnp.dot(a, b, preferred_element_type=F32)


def _dot_nt(a, b):
    return lax.dot_general(a, b, (((1,), (1,)), ((), ())), preferred_element_type=F32)


def _rms(x, g):
    return x * lax.rsqrt(jnp.mean(x * x, axis=-1, keepdims=True) + EPS) * g


def _iota(shape, dim):
    return lax.broadcasted_iota(jnp.int32, shape, dim)


def _split3(x):
    p0 = x.astype(BF16)
    r1 = x - p0.astype(F32)
    p1 = r1.astype(BF16)
    p2 = (r1 - p1.astype(F32)).astype(BF16)
    return p0, p1, p2


def _cumsum_rows(x, carry):
    n = x.shape[0]
    ch = min(n, CUMSUM_CHUNK)
    tril = (_iota((ch, ch), 1) <= _iota((ch, ch), 0)).astype(BF16)
    outs = []
    for c0 in range(0, n, ch):
        r = _dot(tril, jnp.concatenate(_split3(x[c0:c0 + ch]), axis=1))
        c = r[:, :LANES] + r[:, LANES:2 * LANES] + r[:, 2 * LANES:] + carry
        carry = c[ch - 1:ch]
        outs.append(c)
    return (outs[0] if len(outs) == 1 else jnp.concatenate(outs, axis=0)), carry


def _bias_pieces(c):
    b = c * (-LOG2E)
    p0 = b.astype(BF16).astype(F32)
    r1 = b - p0
    p1 = r1.astype(BF16).astype(F32)
    p2 = (r1 - p1).astype(BF16).astype(F32)
    grp = _iota(c.shape, 1) >> 3
    return jnp.where(grp == 0, p0, jnp.where(grp == 1, p1, jnp.where(grp == 2, p2, 0.0)))


def _mem_kv_kernel(mem_ref, g_ref, w_ref, mk_ref, mv_ref):
    M = mem_ref.shape[1]
    xn = _rms(mem_ref[0], g_ref[...]).astype(BF16)
    for out_ref, off in ((mk_ref, 0), (mv_ref, MEM_WIDTH)):
        kv = _dot(xn, w_ref[:, off:off + MEM_WIDTH])
        for h in range(MEM_HEADS):
            out_ref[0, pl.ds(h, M, stride=MEM_HEADS), :] = kv[:, h * MEM_HEAD_DIM:(h + 1) * MEM_HEAD_DIM]


def _mem_kv(mem, g, w_mkv):
    B, M, D = mem.shape
    return pl.pallas_call(
        _mem_kv_kernel,
        grid=(B,),
        in_specs=[pl.BlockSpec((1, M, D), lambda b: (b, 0, 0)), pl.BlockSpec((1, D), lambda b: (0, 0)),
                  pl.BlockSpec((D, 2 * MEM_WIDTH), lambda b: (0, 0))],
        out_specs=[pl.BlockSpec((1, M * MEM_HEADS, MEM_HEAD_DIM), lambda b: (b, 0, 0))] * 2,
        out_shape=[jax.ShapeDtypeStruct((B, M * MEM_HEADS, MEM_HEAD_DIM), F32)] * 2,
        compiler_params=pltpu.CompilerParams(dimension_semantics=("parallel",),
                                             vmem_limit_bytes=VMEM_LIMIT),
        name="mem_kv",
    )(mem, g, w_mkv)


def _in_proj_kernel(x_ref, g_ref, wk_ref, wv_ref, wu_ref, wqm_ref, wf_ref, wqt_ref, bf_ref, hist_ref,
                    mk_ref, mv_ref, wpool_ref, spool_ref,
                    k_ref, v_ref, logf_ref, pnew_ref, opool_ref, omem_ref, *rest, tm, pos0, augment, blk):
    extra, (uext_ref, carry_ref) = rest[:-2], rest[-2:]
    t = pl.program_id(1)
    last = pl.num_programs(1) - 1
    q_scale = FOX_HEAD_DIM ** -0.5 * LOG2E

    @pl.when(t == 0)
    def _():
        uext_ref[0:HIST_ROWS, :] = hist_ref[0]
        carry_ref[...] = jnp.zeros_like(carry_ref)

    xn = _rms(x_ref[0], g_ref[...]).astype(BF16)
    lane = _iota((tm, LANES), 1)
    mem_cols = [slice(h * MEM_HEAD_DIM, (h + 1) * MEM_HEAD_DIM) for h in range(MEM_HEADS)]

    zf = _dot(xn, wf_ref[...]) + bf_ref[...]
    qm = _dot(xn, wqm_ref[...]).astype(BF16)
    k = _dot(xn, wk_ref[...])
    logf = jnp.minimum(zf, 0.0) - jnp.log1p(jnp.exp(-jnp.abs(zf)))
    n_mem = mk_ref.shape[1] // MEM_HEADS
    mem_head = lambda ref, h: ref[0, pl.ds(h, n_mem, stride=MEM_HEADS), :].astype(BF16)
    mem_s = [_dot_nt(qm[:, cols], mem_head(mk_ref, h)) * MEM_HEAD_DIM ** -0.5
             for h, cols in enumerate(mem_cols)]
    v = _dot(xn, wv_ref[...])
    u = _dot(xn, wu_ref[...])
    if augment:
        c, carry_ref[...] = _cumsum_rows(logf, carry_ref[...])
        qt_all = (_dot_nt(wqt_ref[...], xn) * q_scale).astype(BF16)
    else:
        q = _dot_nt(xn, wqt_ref[...]) * q_scale
    for h, (s, cols) in enumerate(zip(mem_s, mem_cols)):
        e = jnp.exp(s - jnp.max(s, axis=-1, keepdims=True))
        o = _dot(e.astype(BF16), mem_head(mv_ref, h)) / jnp.sum(e, axis=-1, keepdims=True)
        omem_ref[0, :, cols] = o.astype(BF16)
    if augment:
        vt_all = v.T.astype(BF16)

    k_ref[0] = k
    v_ref[0] = v
    logf_rows = logf if tm >= LANES else jnp.concatenate([logf, jnp.zeros((LANES - tm, LANES), F32)], axis=0)
    logf_ref[0] = logf_rows.T[:FOX_HEADS, :tm]

    if augment:
        qt_ref, ka_ref, vt_ref = extra
        bias = pltpu.roll(_bias_pieces(c), FOX_HEAD_DIM, axis=1)
        in_bias = (lane >= FOX_HEAD_DIM) & (lane < FOX_HEAD_DIM + BIAS_LANES)
        for pair in range(FOX_HEADS // 2):
            even = k[:, pair * LANES:(pair + 1) * LANES]
            odd = pltpu.roll(even, FOX_HEAD_DIM, axis=1)
            for par, data in ((0, even), (1, odd)):
                h = 2 * pair + par
                tail = jnp.where(in_bias & ((lane & 7) == h), bias, 0.0)
                ka_ref[0, :, h * LANES:(h + 1) * LANES] = jnp.where(lane < FOX_HEAD_DIM, data, tail).astype(BF16)
        q_tail = jnp.where(_iota((LANES - FOX_HEAD_DIM, blk), 0) < BIAS_LANES, 1.0, 0.0).astype(BF16)
        v_tail = jnp.where(_iota((V_SLAB - FOX_HEAD_DIM, blk), 0) == 0, 1.0, 0.0).astype(BF16)
        for h in range(FOX_HEADS):
            qt = qt_all[h * FOX_HEAD_DIM:(h + 1) * FOX_HEAD_DIM]
            vt = vt_all[h * FOX_HEAD_DIM:(h + 1) * FOX_HEAD_DIM]
            for j in range(tm // blk):
                qt_ref[0, j, h * LANES:h * LANES + FOX_HEAD_DIM, :] = qt[:, j * blk:(j + 1) * blk]
                qt_ref[0, j, h * LANES + FOX_HEAD_DIM:(h + 1) * LANES, :] = q_tail
                vt_ref[0, j, h * V_SLAB:h * V_SLAB + FOX_HEAD_DIM, :] = vt[:, j * blk:(j + 1) * blk]
                vt_ref[0, j, h * V_SLAB + FOX_HEAD_DIM:(h + 1) * V_SLAB, :] = v_tail
    else:
        extra[0][0] = q

    uext_ref[HIST_ROWS:HIST_ROWS + tm, :] = u
    pos = pos0 + t * tm + _iota((tm, 1), 0)
    for gi, w in enumerate(POOL_WINDOWS):
        cols = slice(gi * POOL_GROUP_DIM, (gi + 1) * POOL_GROUP_DIM)
        sw = uext_ref[:, cols]
        shift = 1
        while shift < w:
            sw = sw + pltpu.roll(sw, shift, axis=0)
            shift *= 2
        cnt = jnp.minimum(w, pos + 1).astype(F32)
        pooled = sw[HIST_ROWS:, :] / cnt - u[:, cols]
        mixed = _dot(pooled.astype(BF16), wpool_ref[gi]) * spool_ref[:, cols]
        opool_ref[0, :, cols] = mixed.astype(BF16)

    @pl.when(t == last)
    def _():
        pnew_ref[0] = uext_ref[tm + 1:tm + HIST_ROWS, :]

    uext_ref[0:HIST_ROWS, :] = uext_ref[tm:tm + HIST_ROWS, :]


def _in_proj(x, g, weights, bf_rep, hist, mk, mv, wpool, spool, *, tm, pos0, augment, blk=None):
    B, S, D = x.shape
    nt = S // tm
    tile = lambda w: pl.BlockSpec((1, tm, w), lambda b, t: (b, t, 0))
    per_b = lambda r, w: pl.BlockSpec((1, r, w), lambda b, t: (b, 0, 0))
    const = lambda shape: pl.BlockSpec(shape, lambda b, t: (0,) * len(shape))
    if augment:
        W = FOX_HEADS * LANES
        t_tile = lambda rows: pl.BlockSpec((1, tm // blk, rows, blk), lambda b, t: (b, t, 0, 0))
        abc_specs = [t_tile(W), tile(W), t_tile(FOX_HEADS * V_SLAB)]
        abc_shapes = [jax.ShapeDtypeStruct((B, S // blk, W, blk), BF16),
                      jax.ShapeDtypeStruct((B, S, W), BF16),
                      jax.ShapeDtypeStruct((B, S // blk, FOX_HEADS * V_SLAB, blk), BF16)]
    else:
        abc_specs = [tile(FOX_WIDTH)]
        abc_shapes = [jax.ShapeDtypeStruct((B, S, FOX_WIDTH), F32)]
    return pl.pallas_call(
        functools.partial(_in_proj_kernel, tm=tm, pos0=pos0, augment=augment, blk=blk),
        grid=(B, nt),
        in_specs=[tile(D), const((1, D))] + [const(w.shape) for w in weights] + [
            const((1, LANES)), per_b(HIST_ROWS, POOL_WIDTH), per_b(*mk.shape[1:]), per_b(*mv.shape[1:]),
            const((POOL_GROUPS, POOL_GROUP_DIM, POOL_GROUP_DIM)), const((1, POOL_WIDTH))],
        out_specs=[tile(FOX_WIDTH), tile(FOX_WIDTH), pl.BlockSpec((1, FOX_HEADS, tm), lambda b, t: (b, 0, t)),
                   per_b(POOL_HIST, POOL_WIDTH), tile(POOL_WIDTH), tile(MEM_WIDTH)] + abc_specs,
        out_shape=[jax.ShapeDtypeStruct((B, S, FOX_WIDTH), F32),
                   jax.ShapeDtypeStruct((B, S, FOX_WIDTH), F32),
                   jax.ShapeDtypeStruct((B, FOX_HEADS, S), F32),
                   jax.ShapeDtypeStruct((B, POOL_HIST, POOL_WIDTH), F32),
                   jax.ShapeDtypeStruct((B, S, POOL_WIDTH), BF16),
                   jax.ShapeDtypeStruct((B, S, MEM_WIDTH), BF16)] + abc_shapes,
        scratch_shapes=[pltpu.VMEM((HIST_ROWS + tm, POOL_WIDTH), F32), pltpu.VMEM((1, LANES), F32)],
        compiler_params=pltpu.CompilerParams(dimension_semantics=("parallel", "arbitrary"),
                                             vmem_limit_bytes=VMEM_LIMIT),
        name="in_proj",
    )(x, g, *weights, bf_rep, hist, mk, mv, wpool, spool)


def _fox_prompt_kernel(qt_ref, k_ref, vt_ref, o_ref, m_ref, acc_ref, *, blk):
    i = pl.program_id(1)
    m_ref[...] = jnp.full_like(m_ref, -jnp.inf)
    acc_ref[...] = jnp.zeros_like(acc_ref)
    slabs = [slice(h * LANES, (h + 1) * LANES) for h in range(FOX_HEADS)]

    def scores(j, nb, h):
        start = pl.multiple_of(j * blk, blk)
        return _dot(k_ref[0, pl.ds(start, nb * blk), slabs[h]], qt_ref[0, 0, slabs[h], :])

    def accumulate(j, nb, h, st, masked):
        if masked:
            st = jnp.where(_iota(st.shape, 0) <= _iota(st.shape, 1), st, -jnp.inf)
        m_prev = m_ref[h]
        m_new = jnp.maximum(m_prev, jnp.max(st, axis=0, keepdims=True))
        p = jnp.exp2(st - m_new).astype(BF16)
        pv = _dot(vt_ref[0, j, h * V_SLAB:(h + 1) * V_SLAB, :], p[:blk])
        for r in range(1, nb):
            pv = pv + _dot(vt_ref[0, j + r, h * V_SLAB:(h + 1) * V_SLAB, :], p[r * blk:(r + 1) * blk])
        acc_ref[h] = jnp.exp2(m_prev - m_new) * acc_ref[h] + pv
        m_ref[h] = m_new

    def run(first, n_visible):
        spans = [(r, min(KEY_BLOCKS_PER_ITEM, n_visible - r), False)
                 for r in range(0, n_visible, KEY_BLOCKS_PER_ITEM)]
        if n_visible < KEY_BLOCKS_PER_ITER:
            spans.append((n_visible, 1, True))
        work = [(first + r, nb, h, masked) for r, nb, masked in spans for h in range(FOX_HEADS)]
        pending = [scores(*w[:3]) for w in work[:SCORES_AHEAD]]
        for n, (j, nb, h, masked) in enumerate(work):
            if n + SCORES_AHEAD < len(work):
                pending.append(scores(*work[n + SCORES_AHEAD][:3]))
            accumulate(j, nb, h, pending.pop(0), masked)

    shift = KEY_BLOCKS_PER_ITER.bit_length() - 1
    n_iter = i >> shift

    def body(jj, carry):
        run(KEY_BLOCKS_PER_ITER * jj, KEY_BLOCKS_PER_ITER)
        return carry

    lax.fori_loop(0, n_iter, body, 0)
    for rem in range(KEY_BLOCKS_PER_ITER):
        @pl.when(i - (n_iter << shift) == rem)
        def _(rem=rem):
            run(i - rem, rem)

    for h in range(FOX_HEADS):
        acc = acc_ref[h]
        o_ref[0, 0, h * FOX_HEAD_DIM:(h + 1) * FOX_HEAD_DIM, :] = (
            acc[:FOX_HEAD_DIM] / acc[FOX_HEAD_DIM:FOX_HEAD_DIM + 1])


def _fox_prompt(qt, ka, vt, *, blk):
    B, S, W = ka.shape
    nblk = S // blk
    return pl.pallas_call(
        functools.partial(_fox_prompt_kernel, blk=blk),
        grid=(B, nblk),
        in_specs=[pl.BlockSpec((1, 1, W, blk), lambda b, i: (b, i, 0, 0)),
                  pl.BlockSpec((1, S, W), lambda b, i: (b, 0, 0)),
                  pl.BlockSpec((1, nblk, FOX_HEADS * V_SLAB, blk), lambda b, i: (b, 0, 0, 0))],
        out_specs=pl.BlockSpec((1, 1, FOX_WIDTH, blk), lambda b, i: (b, i, 0, 0)),
        out_shape=jax.ShapeDtypeStruct((B, nblk, FOX_WIDTH, blk), F32),
        scratch_shapes=[pltpu.VMEM((FOX_HEADS, 1, blk), F32), pltpu.VMEM((FOX_HEADS, V_SLAB, blk), F32)],
        compiler_params=pltpu.CompilerParams(
            dimension_semantics=("parallel", "arbitrary"), vmem_limit_bytes=VMEM_LIMIT),
        name="fox_prompt",
    )(qt, ka, vt)


def _bias_piece_rows(c):
    b = c * (-LOG2E)
    p0 = b.astype(BF16).astype(F32)
    r1 = b - p0
    p1 = r1.astype(BF16).astype(F32)
    p2 = (r1 - p1).astype(BF16).astype(F32)
    return jnp.concatenate([p0, p1, p2], axis=0)


def _fox_sample_kernel(q_ref, kn_ref, vn_ref, lfn_ref, kt_ref, vt_ref, lft_ref, o_ref, *, chunk):
    T = q_ref.shape[1]
    P = lft_ref.shape[2]
    HT = FOX_HEADS * T
    t_shift = T.bit_length() - 1
    assert HT == LANES and 1 << t_shift == T

    q_rep = jnp.concatenate([q_ref[0]] * FOX_HEADS, axis=0)
    row_head = _iota((HT, FOX_WIDTH), 0) >> t_shift
    q_bd = jnp.where(row_head == (_iota((HT, FOX_WIDTH), 1) >> 6), q_rep, 0.0).astype(BF16)
    e_lane = _iota((HT, LANES), 1)
    e_bd = jnp.where((e_lane < BIAS_LANES) & ((e_lane & 7) == (_iota((HT, LANES), 0) >> t_shift)),
                     1.0, 0.0).astype(BF16)

    def bias_operand(c):
        rows = _bias_piece_rows(c)
        return jnp.concatenate([rows, jnp.zeros((LANES - BIAS_LANES, c.shape[1]), F32)], axis=0).astype(BF16)

    def running_sum(lf, carry, upper):
        pieces = jnp.concatenate(list(_split3(lf)) + [jnp.zeros(lf.shape, BF16)], axis=0)
        loc = _dot(pieces, upper)
        return loc[0:8] + loc[8:16] + loc[16:24] + carry

    upper = (_iota((chunk, chunk), 0) <= _iota((chunk, chunk), 1)).astype(BF16)
    carry = jnp.zeros((FOX_HEADS, 1), F32)
    c_parts = []
    for ci in range(P // chunk):
        c = running_sum(lft_ref[0, :, ci * chunk:(ci + 1) * chunk], carry, upper)
        carry = c[:, chunk - 1:chunk]
        c_parts.append(c)
    kt = kt_ref[0].reshape(FOX_WIDTH, P).astype(BF16)
    s_old = _dot(q_bd, kt) + _dot(e_bd, bias_operand(jnp.concatenate(c_parts, axis=1)))

    pad_lanes = lambda a: jnp.concatenate([a, jnp.zeros((a.shape[0], LANES - a.shape[1]), F32)], axis=1)
    pad_rows = lambda a: jnp.concatenate([a, jnp.zeros((LANES - T, a.shape[1]), F32)], axis=0)
    upper_new = (_iota((LANES, LANES), 0) <= _iota((LANES, LANES), 1)).astype(BF16)
    c_new = running_sum(pad_lanes(lfn_ref[0]), carry, upper_new)
    s_new = _dot_nt(q_bd, pad_rows(kn_ref[0]).astype(BF16)) + _dot(e_bd, bias_operand(c_new))
    s_new = jnp.where(_iota((HT, LANES), 1) <= (_iota((HT, LANES), 0) & (T - 1)), s_new, -jnp.inf)

    m = jnp.maximum(jnp.max(s_old, axis=1, keepdims=True), jnp.max(s_new, axis=1, keepdims=True))
    p_old = jnp.exp2(s_old - m).astype(BF16)
    p_new = jnp.exp2(s_new - m).astype(BF16)
    l = (jnp.sum(p_old.astype(F32), axis=1, keepdims=True)
         + jnp.sum(p_new.astype(F32), axis=1, keepdims=True))
    r = (_dot_nt(p_old, vt_ref[0].reshape(FOX_WIDTH, P).astype(BF16))
         + _dot(p_new, pad_rows(vn_ref[0]).astype(BF16))) / l
    col_head = _iota((T, FOX_WIDTH), 1) >> 6
    o = jnp.zeros((T, FOX_WIDTH), F32)
    for h in range(FOX_HEADS):
        o = jnp.where(col_head == h, r[h * T:(h + 1) * T, :], o)
    o_ref[0] = o.astype(BF16)


def _fox_sample(q, k_new, v_new, lf_new_t, cache_kt, cache_vt, cache_lft, *, chunk):
    B, T, _ = q.shape
    P = cache_lft.shape[2]
    new = lambda w: pl.BlockSpec((1, T, w), lambda b: (b, 0, 0))
    kv_spec = pl.BlockSpec((1, FOX_HEADS, FOX_HEAD_DIM, P), lambda b: (b, 0, 0, 0))
    return pl.pallas_call(
        functools.partial(_fox_sample_kernel, chunk=chunk),
        grid=(B,),
        in_specs=[new(FOX_WIDTH), new(FOX_WIDTH), new(FOX_WIDTH),
                  pl.BlockSpec((1, FOX_HEADS, T), lambda b: (b, 0, 0)), kv_spec, kv_spec,
                  pl.BlockSpec((1, FOX_HEADS, P), lambda b: (b, 0, 0))],
        out_specs=new(FOX_WIDTH),
        out_shape=jax.ShapeDtypeStruct((B, T, FOX_WIDTH), BF16),
        compiler_params=pltpu.CompilerParams(dimension_semantics=("parallel",),
                                             vmem_limit_bytes=VMEM_LIMIT),
        name="fox_sample",
    )(q, k_new, v_new, lf_new_t, cache_kt, cache_vt, cache_lft)


def _merge_ffn_kernel(x_ref, ofox_ref, opool_ref, omem_ref, ga_ref, wg_ref, wbf_ref, wbp_ref, wbm_ref,
                      wout_ref, gm_ref, wup_ref, wdown_ref, gfin_ref, y_ref, *, ff_chunk, final_norm):
    D = x_ref.shape[-1]
    streams = ofox_ref.shape[1]
    rows = x_ref.shape[1] // streams
    groups = [slice(i * rows, (i + 1) * rows) for i in range(streams)]

    xs = [x_ref[0, r, :] for r in groups]
    pre = []
    for si, (x, r) in enumerate(zip(xs, groups)):
        v1 = _dot(opool_ref[0, r, :], wbp_ref[...])
        v2 = _dot(omem_ref[0, r, :], wbm_ref[...])
        v0 = _dot(ofox_ref[0, si].T.astype(BF16), wbf_ref[...])
        xn = _rms(x, ga_ref[...]).astype(BF16)
        gates = [_dot(xn, wg_ref[:, bi * D:(bi + 1) * D]) for bi in range(N_BRANCH)]
        pre.append((gates, (v0, v1, v2)))
    hs = []
    for x, ((g0, g1, g2), (v0, v1, v2)) in zip(xs, pre):
        m = jax.nn.sigmoid(g0) * v0 + jax.nn.sigmoid(g1) * v1 + jax.nn.sigmoid(g2) * v2
        hs.append(x + _dot(m.astype(BF16), wout_ref[...]))
    hns = [_rms(h, gm_ref[...]).astype(BF16) for h in hs]
    accs = list(hs)
    for c in range(wup_ref.shape[1] // ff_chunk):
        cols = slice(c * ff_chunk, (c + 1) * ff_chunk)
        ups = [jnp.maximum(_dot(hn, wup_ref[:, cols]), 0.0) for hn in hns]
        accs = [acc + _dot((a * a).astype(BF16), wdown_ref[cols, :]) for acc, a in zip(accs, ups)]
    for acc, r in zip(accs, groups):
        y_ref[0, r, :] = _rms(acc, gfin_ref[...]) if final_norm else acc


def _merge_ffn(x, ofox_t, opool, omem, ga, wg, wbf, wbp, wbm, wout, gm, wup, wdown, gfin, *, tm,
               ff_chunk, final_norm):
    B, S, D = x.shape
    blk = ofox_t.shape[3]
    tile = lambda w: pl.BlockSpec((1, tm, w), lambda b, t: (b, t, 0))
    fox_tile = pl.BlockSpec((1, tm // blk, FOX_WIDTH, blk), lambda b, t: (b, t, 0, 0))
    const = lambda a: pl.BlockSpec(a.shape, lambda b, t: (0,) * a.ndim, pipeline_mode=pl.Buffered(1))
    consts = (ga, wg, wbf, wbp, wbm, wout, gm, wup, wdown, gfin)
    return pl.pallas_call(
        functools.partial(_merge_ffn_kernel, ff_chunk=ff_chunk, final_norm=final_norm),
        grid=(B, S // tm),
        in_specs=[tile(D), fox_tile, tile(POOL_WIDTH), tile(MEM_WIDTH)] + [const(a) for a in consts],
        out_specs=tile(D),
        out_shape=jax.ShapeDtypeStruct((B, S, D), F32),
        compiler_params=pltpu.CompilerParams(dimension_semantics=("parallel", "parallel"),
                                             vmem_limit_bytes=VMEM_LIMIT),
        name="merge_ffn",
    )(x, ofox_t, opool, omem, *consts)


def _tile_rows(n, want):
    return want if n % want == 0 else n


def _repack_w_in(w_in):
    o_f = 3 * FOX_WIDTH
    o_u = o_f + FOX_HEADS
    o_qm = o_u + POOL_WIDTH
    o_g = o_qm + MEM_WIDTH
    b16 = lambda a: a.astype(BF16)
    wq, wk, wv = (w_in[:, i * FOX_WIDTH:(i + 1) * FOX_WIDTH] for i in range(3))
    wf_rep = _rep_lanes(w_in[:, o_f:o_u])
    weights = (b16(wk), b16(wv), b16(w_in[:, o_u:o_qm]), b16(w_in[:, o_qm:o_g]), b16(wf_rep), b16(wq).T)
    return weights, b16(w_in[:, o_g:])


def _rep_lanes(a):
    pad = jnp.zeros(a.shape[:-1] + (LANES - BIAS_LANES,), a.dtype)
    return jnp.concatenate([a] * BIAS_PIECES + [pad], axis=-1)


def kernel(x_prompt, x_sample, mem_prompt, cache_fox_k, cache_fox_v, cache_fox_logf, cache_pool,
           cache_mem_k, cache_mem_v, g_attn, w_in, b_f, w_pool, s_pool, g_mem, w_mkv,
           w_br_fox, w_br_pool, w_br_mem, w_out, g_mlp, w_up, w_down, g_final):
    depth = w_in.shape[0]
    B, S, D = x_prompt.shape
    BS, T, _ = x_sample.shape
    P = cache_fox_k.shape[2]
    M = mem_prompt.shape[1]
    tm_p = _tile_rows(S, 512)
    blk_p = _tile_rows(tm_p, 256)
    tm_s = _tile_rows(BS * T, 256)
    chunk_s = _tile_rows(P, 256)
    gfin = g_final.reshape(1, D)

    xp, xs = x_prompt, x_sample
    outs_p = [[] for _ in range(6)]
    outs_s = [[] for _ in range(4)]
    for l in range(depth):
        weights, wg = _repack_w_in(w_in[l])
        bf_rep = _rep_lanes(b_f[l]).reshape(1, LANES)
        g_a, g_m, g_f = g_attn[l].reshape(1, D), g_mem[l].reshape(1, D), g_mlp[l].reshape(1, D)
        wpool, spool = w_pool[l].astype(BF16), s_pool[l].reshape(1, POOL_WIDTH)
        wbf, wbp, wbm = (w.astype(BF16) for w in (w_br_fox[l], w_br_pool[l], w_br_mem[l]))
        wout, wup, wdown = (w.astype(BF16) for w in (w_out[l], w_up[l], w_down[l]))
        final = l == depth - 1
        post = functools.partial(_merge_ffn, ga=g_a, wg=wg, wbf=wbf, wbp=wbp, wbm=wbm, wout=wout, gm=g_f,
                                 wup=wup, wdown=wdown, gfin=gfin, ff_chunk=1024, final_norm=final)

        hist = jnp.pad(cache_pool[l], ((0, 0), (HIST_ROWS - POOL_HIST, 0), (0, 0)))
        cmk = cache_mem_k[l].reshape(BS, M * MEM_HEADS, MEM_HEAD_DIM)
        cmv = cache_mem_v[l].reshape(BS, M * MEM_HEADS, MEM_HEAD_DIM)
        k, v, logf_t, pnew, opool, omem, q = _in_proj(
            xs, g_a, weights, bf_rep, hist, cmk, cmv, wpool, spool, tm=T, pos0=P, augment=False)
        logf = jnp.transpose(logf_t, (0, 2, 1))
        ofox = _fox_sample(q, k, v, logf_t,
                           jnp.transpose(cache_fox_k[l], (0, 2, 3, 1)), jnp.transpose(cache_fox_v[l], (0, 2, 3, 1)),
                           jnp.transpose(cache_fox_logf[l], (0, 2, 1)), chunk=chunk_s)
        flat = lambda a: a.reshape(1, BS * T, a.shape[-1])
        ofox_t = jnp.transpose(ofox.reshape(BS * T // tm_s, tm_s, FOX_WIDTH), (0, 2, 1))[None].astype(F32)
        xs = post(flat(xs), ofox_t, flat(opool), flat(omem), tm=tm_s).reshape(BS, T, D)
        for acc, val in zip(outs_s, (k.reshape(BS, T, FOX_HEADS, FOX_HEAD_DIM),
                                     v.reshape(BS, T, FOX_HEADS, FOX_HEAD_DIM), logf, pnew)):
            acc.append(val)

        mk, mv = _mem_kv(mem_prompt, g_m, w_mkv[l].astype(BF16))
        hist0 = jnp.zeros((B, HIST_ROWS, POOL_WIDTH), F32)
        k, v, logf_t, pnew, opool, omem, qt, ka, vt = _in_proj(
            xp, g_a, weights, bf_rep, hist0, mk, mv, wpool, spool, tm=tm_p, pos0=0, augment=True,
            blk=blk_p)
        logf = jnp.transpose(logf_t, (0, 2, 1))
        ofox = _fox_prompt(qt, ka, vt, blk=blk_p)
        xp = post(xp, ofox, opool, omem, tm=tm_p)
        for acc, val in zip(outs_p, (k.reshape(B, S, FOX_HEADS, FOX_HEAD_DIM),
                                     v.reshape(B, S, FOX_HEADS, FOX_HEAD_DIM), logf, pnew,
                                     mk.reshape(B, M, MEM_HEADS, MEM_HEAD_DIM),
                                     mv.reshape(B, M, MEM_HEADS, MEM_HEAD_DIM))):
            acc.append(val)

    stack = lambda o: o[0][None] if depth == 1 else jnp.stack(o)
    return (xp, xs, *(stack(o) for o in outs_p), *(stack(o) for o in outs_s))
```

```python
import functools
import math

import jax
import jax.numpy as jnp
from jax import lax
from jax.experimental import pallas as pl
from jax.experimental.pallas import tpu as pltpu

F32 = jnp.float32
BF16 = jnp.bfloat16

EPS = 1e-6
LANES = 128
FOX_HEADS = 8
FOX_HEAD_DIM = 64
FOX_WIDTH = FOX_HEADS * FOX_HEAD_DIM
POOL_WINDOWS = (2, 4, 8, 16)
POOL_GROUPS = 4
POOL_GROUP_DIM = 128
POOL_WIDTH = POOL_GROUPS * POOL_GROUP_DIM
POOL_HIST = 15
HIST_ROWS = 16
MEM_HEADS = 4
MEM_HEAD_DIM = 128
MEM_WIDTH = MEM_HEADS * MEM_HEAD_DIM
N_BRANCH = 3
LOG2E = math.log2(math.e)
BIAS_PIECES = 3
BIAS_LANES = BIAS_PIECES * FOX_HEADS
V_SLAB = 80
KEY_BLOCKS_PER_ITER = 8
KEY_BLOCKS_PER_ITEM = 1
SCORES_AHEAD = 6
CUMSUM_CHUNK = 128
VMEM_LIMIT = 56 * 1024 * 1024


def _dot(a, b):
    return jnp.dot(a, b, preferred_element_type=F32)


def _dot_nt(a, b):
    return lax.dot_general(a, b, (((1,), (1,)), ((), ())), preferred_element_type=F32)


def _rms(x, g):
    return x * lax.rsqrt(jnp.mean(x * x, axis=-1, keepdims=True) + EPS) * g


def _iota(shape, dim):
    return lax.broadcasted_iota(jnp.int32, shape, dim)


def _split3(x):
    p0 = x.astype(BF16)
    r1 = x - p0.astype(F32)
    p1 = r1.astype(BF16)
    p2 = (r1 - p1.astype(F32)).astype(BF16)
    return p0, p1, p2


def _cumsum_rows(x, carry):
    n = x.shape[0]
    ch = min(n, CUMSUM_CHUNK)
    tril = (_iota((ch, ch), 1) <= _iota((ch, ch), 0)).astype(BF16)
    outs = []
    for c0 in range(0, n, ch):
        r = _dot(tril, jnp.concatenate(_split3(x[c0:c0 + ch]), axis=1))
        c = r[:, :LANES] + r[:, LANES:2 * LANES] + r[:, 2 * LANES:] + carry
        carry = c[ch - 1:ch]
        outs.append(c)
    return (outs[0] if len(outs) == 1 else jnp.concatenate(outs, axis=0)), carry


def _bias_pieces(c):
    b = c * (-LOG2E)
    p0 = b.astype(BF16).astype(F32)
    r1 = b - p0
    p1 = r1.astype(BF16).astype(F32)
    p2 = (r1 - p1).astype(BF16).astype(F32)
    grp = _iota(c.shape, 1) >> 3
    return jnp.where(grp == 0, p0, jnp.where(grp == 1, p1, jnp.where(grp == 2, p2, 0.0)))


def _mem_kv_kernel(mem_ref, g_ref, w_ref, mk_ref, mv_ref):
    M = mem_ref.shape[1]
    xn = _rms(mem_ref[0], g_ref[...]).astype(BF16)
    for out_ref, off in ((mk_ref, 0), (mv_ref, MEM_WIDTH)):
        kv = _dot(xn, w_ref[:, off:off + MEM_WIDTH])
        for h in range(MEM_HEADS):
            out_ref[0, pl.ds(h, M, stride=MEM_HEADS), :] = kv[:, h * MEM_HEAD_DIM:(h + 1) * MEM_HEAD_DIM]


def _mem_kv(mem, g, w_mkv):
    B, M, D = mem.shape
    return pl.pallas_call(
        _mem_kv_kernel,
        grid=(B,),
        in_specs=[pl.BlockSpec((1, M, D), lambda b: (b, 0, 0)), pl.BlockSpec((1, D), lambda b: (0, 0)),
                  pl.BlockSpec((D, 2 * MEM_WIDTH), lambda b: (0, 0))],
        out_specs=[pl.BlockSpec((1, M * MEM_HEADS, MEM_HEAD_DIM), lambda b: (b, 0, 0))] * 2,
        out_shape=[jax.ShapeDtypeStruct((B, M * MEM_HEADS, MEM_HEAD_DIM), F32)] * 2,
        compiler_params=pltpu.CompilerParams(dimension_semantics=("parallel",),
                                             vmem_limit_bytes=VMEM_LIMIT),
        name="mem_kv",
    )(mem, g, w_mkv)


def _in_proj_kernel(x_ref, g_ref, wk_ref, wv_ref, wu_ref, wqm_ref, wf_ref, wqt_ref, bf_ref, hist_ref,
                    mk_ref, mv_ref, wpool_ref, spool_ref,
                    k_ref, v_ref, logf_ref, pnew_ref, opool_ref, omem_ref, *rest, tm, pos0, augment, blk):
    extra, (uext_ref, carry_ref) = rest[:-2], rest[-2:]
    t = pl.program_id(1)
    last = pl.num_programs(1) - 1
    q_scale = FOX_HEAD_DIM ** -0.5 * LOG2E

    @pl.when(t == 0)
    def _():
        uext_ref[0:HIST_ROWS, :] = hist_ref[0]
        carry_ref[...] = jnp.zeros_like(carry_ref)

    xn = _rms(x_ref[0], g_ref[...]).astype(BF16)
    lane = _iota((tm, LANES), 1)
    mem_cols = [slice(h * MEM_HEAD_DIM, (h + 1) * MEM_HEAD_DIM) for h in range(MEM_HEADS)]

    zf = _dot(xn, wf_ref[...]) + bf_ref[...]
    qm = _dot(xn, wqm_ref[...]).astype(BF16)
    k = _dot(xn, wk_ref[...])
    logf = jnp.minimum(zf, 0.0) - jnp.log1p(jnp.exp(-jnp.abs(zf)))
    n_mem = mk_ref.shape[1] // MEM_HEADS
    mem_head = lambda ref, h: ref[0, pl.ds(h, n_mem, stride=MEM_HEADS), :].astype(BF16)
    mem_s = [_dot_nt(qm[:, cols], mem_head(mk_ref, h)) * MEM_HEAD_DIM ** -0.5
             for h, cols in enumerate(mem_cols)]
    v = _dot(xn, wv_ref[...])
    u = _dot(xn, wu_ref[...])
    if augment:
        c, carry_ref[...] = _cumsum_rows(logf, carry_ref[...])
        qt_all = (_dot_nt(wqt_ref[...], xn) * q_scale).astype(BF16)
    else:
        q = _dot_nt(xn, wqt_ref[...]) * q_scale
    for h, (s, cols) in enumerate(zip(mem_s, mem_cols)):
        e = jnp.exp(s - jnp.max(s, axis=-1, keepdims=True))
        o = _dot(e.astype(BF16), mem_head(mv_ref, h)) / jnp.sum(e, axis=-1, keepdims=True)
        omem_ref[0, :, cols] = o.astype(BF16)
    if augment:
        vt_all = v.T.astype(BF16)

    k_ref[0] = k
    v_ref[0] = v
    logf_rows = logf if tm >= LANES else jnp.concatenate([logf, jnp.zeros((LANES - tm, LANES), F32)], axis=0)
    logf_ref[0] = logf_rows.T[:FOX_HEADS, :tm]

    if augment:
        qt_ref, ka_ref, vt_ref = extra
        bias = pltpu.roll(_bias_pieces(c), FOX_HEAD_DIM, axis=1)
        in_bias = (lane >= FOX_HEAD_DIM) & (lane < FOX_HEAD_DIM + BIAS_LANES)
        for pair in range(FOX_HEADS // 2):
            even = k[:, pair * LANES:(pair + 1) * LANES]
            odd = pltpu.roll(even, FOX_HEAD_DIM, axis=1)
            for par, data in ((0, even), (1, odd)):
                h = 2 * pair + par
                tail = jnp.where(in_bias & ((lane & 7) == h), bias, 0.0)
                ka_ref[0, :, h * LANES:(h + 1) * LANES] = jnp.where(lane < FOX_HEAD_DIM, data, tail).astype(BF16)
        q_tail = jnp.where(_iota((LANES - FOX_HEAD_DIM, blk), 0) < BIAS_LANES, 1.0, 0.0).astype(BF16)
        v_tail = jnp.where(_iota((V_SLAB - FOX_HEAD_DIM, blk), 0) == 0, 1.0, 0.0).astype(BF16)
        for h in range(FOX_HEADS):
            qt = qt_all[h * FOX_HEAD_DIM:(h + 1) * FOX_HEAD_DIM]
            vt = vt_all[h * FOX_HEAD_DIM:(h + 1) * FOX_HEAD_DIM]
            for j in range(tm // blk):
                qt_ref[0, j, h * LANES:h * LANES + FOX_HEAD_DIM, :] = qt[:, j * blk:(j + 1) * blk]
                qt_ref[0, j, h * LANES + FOX_HEAD_DIM:(h + 1) * LANES, :] = q_tail
                vt_ref[0, j, h * V_SLAB:h * V_SLAB + FOX_HEAD_DIM, :] = vt[:, j * blk:(j + 1) * blk]
                vt_ref[0, j, h * V_SLAB + FOX_HEAD_DIM:(h + 1) * V_SLAB, :] = v_tail
    else:
        extra[0][0] = q

    uext_ref[HIST_ROWS:HIST_ROWS + tm, :] = u
    pos = pos0 + t * tm + _iota((tm, 1), 0)
    for gi, w in enumerate(POOL_WINDOWS):
        cols = slice(gi * POOL_GROUP_DIM, (gi + 1) * POOL_GROUP_DIM)
        sw = uext_ref[:, cols]
        shift = 1
        while shift < w:
            sw = sw + pltpu.roll(sw, shift, axis=0)
            shift *= 2
        cnt = jnp.minimum(w, pos + 1).astype(F32)
        pooled = sw[HIST_ROWS:, :] / cnt - u[:, cols]
        mixed = _dot(pooled.astype(BF16), wpool_ref[gi]) * spool_ref[:, cols]
        opool_ref[0, :, cols] = mixed.astype(BF16)

    @pl.when(t == last)
    def _():
        pnew_ref[0] = uext_ref[tm + 1:tm + HIST_ROWS, :]

    uext_ref[0:HIST_ROWS, :] = uext_ref[tm:tm + HIST_ROWS, :]


def _in_proj(x, g, weights, bf_rep, hist, mk, mv, wpool, spool, *, tm, pos0, augment, blk=None):
    B, S, D = x.shape
    nt = S // tm
    tile = lambda w: pl.BlockSpec((1, tm, w), lambda b, t: (b, t, 0))
    per_b = lambda r, w: pl.BlockSpec((1, r, w), lambda b, t: (b, 0, 0))
    const = lambda shape: pl.BlockSpec(shape, lambda b, t: (0,) * len(shape))
    if augment:
        W = FOX_HEADS * LANES
        t_tile = lambda rows: pl.BlockSpec((1, tm // blk, rows, blk), lambda b, t: (b, t, 0, 0))
        abc_specs = [t_tile(W), tile(W), t_tile(FOX_HEADS * V_SLAB)]
        abc_shapes = [jax.ShapeDtypeStruct((B, S // blk, W, blk), BF16),
                      jax.ShapeDtypeStruct((B, S, W), BF16),
                      jax.ShapeDtypeStruct((B, S // blk, FOX_HEADS * V_SLAB, blk), BF16)]
    else:
        abc_specs = [tile(FOX_WIDTH)]
        abc_shapes = [jax.ShapeDtypeStruct((B, S, FOX_WIDTH), F32)]
    return pl.pallas_call(
        functools.partial(_in_proj_kernel, tm=tm, pos0=pos0, augment=augment, blk=blk),
        grid=(B, nt),
        in_specs=[tile(D), const((1, D))] + [const(w.shape) for w in weights] + [
            const((1, LANES)), per_b(HIST_ROWS, POOL_WIDTH), per_b(*mk.shape[1:]), per_b(*mv.shape[1:]),
            const((POOL_GROUPS, POOL_GROUP_DIM, POOL_GROUP_DIM)), const((1, POOL_WIDTH))],
        out_specs=[tile(FOX_WIDTH), tile(FOX_WIDTH), pl.BlockSpec((1, FOX_HEADS, tm), lambda b, t: (b, 0, t)),
                   per_b(POOL_HIST, POOL_WIDTH), tile(POOL_WIDTH), tile(MEM_WIDTH)] + abc_specs,
        out_shape=[jax.ShapeDtypeStruct((B, S, FOX_WIDTH), F32),
                   jax.ShapeDtypeStruct((B, S, FOX_WIDTH), F32),
                   jax.ShapeDtypeStruct((B, FOX_HEADS, S), F32),
                   jax.ShapeDtypeStruct((B, POOL_HIST, POOL_WIDTH), F32),
                   jax.ShapeDtypeStruct((B, S, POOL_WIDTH), BF16),
                   jax.ShapeDtypeStruct((B, S, MEM_WIDTH), BF16)] + abc_shapes,
        scratch_shapes=[pltpu.VMEM((HIST_ROWS + tm, POOL_WIDTH), F32), pltpu.VMEM((1, LANES), F32)],
        compiler_params=pltpu.CompilerParams(dimension_semantics=("parallel", "arbitrary"),
                                             vmem_limit_bytes=VMEM_LIMIT),
        name="in_proj",
    )(x, g, *weights, bf_rep, hist, mk, mv, wpool, spool)


def _fox_prompt_kernel(qt_ref, k_ref, vt_ref, o_ref, m_ref, acc_ref, *, blk):
    i = pl.program_id(1)
    m_ref[...] = jnp.full_like(m_ref, -jnp.inf)
    acc_ref[...] = jnp.zeros_like(acc_ref)
    slabs = [slice(h * LANES, (h + 1) * LANES) for h in range(FOX_HEADS)]

    def scores(j, nb, h):
        start = pl.multiple_of(j * blk, blk)
        return _dot(k_ref[0, pl.ds(start, nb * blk), slabs[h]], qt_ref[0, 0, slabs[h], :])

    def accumulate(j, nb, h, st, masked):
        if masked:
            st = jnp.where(_iota(st.shape, 0) <= _iota(st.shape, 1), st, -jnp.inf)
        m_prev = m_ref[h]
        m_new = jnp.maximum(m_prev, jnp.max(st, axis=0, keepdims=True))
        p = jnp.exp2(st - m_new).astype(BF16)
        pv = _dot(vt_ref[0, j, h * V_SLAB:(h + 1) * V_SLAB, :], p[:blk])
        for r in range(1, nb):
            pv = pv + _dot(vt_ref[0, j + r, h * V_SLAB:(h + 1) * V_SLAB, :], p[r * blk:(r + 1) * blk])
        acc_ref[h] = jnp.exp2(m_prev - m_new) * acc_ref[h] + pv
        m_ref[h] = m_new

    def run(first, n_visible):
        spans = [(r, min(KEY_BLOCKS_PER_ITEM, n_visible - r), False)
                 for r in range(0, n_visible, KEY_BLOCKS_PER_ITEM)]
        if n_visible < KEY_BLOCKS_PER_ITER:
            spans.append((n_visible, 1, True))
        work = [(first + r, nb, h, masked) for r, nb, masked in spans for h in range(FOX_HEADS)]
        pending = [scores(*w[:3]) for w in work[:SCORES_AHEAD]]
        for n, (j, nb, h, masked) in enumerate(work):
            if n + SCORES_AHEAD < len(work):
                pending.append(scores(*work[n + SCORES_AHEAD][:3]))
            accumulate(j, nb, h, pending.pop(0), masked)

    shift = KEY_BLOCKS_PER_ITER.bit_length() - 1
    n_iter = i >> shift

    def body(jj, carry):
        run(KEY_BLOCKS_PER_ITER * jj, KEY_BLOCKS_PER_ITER)
        return carry

    lax.fori_loop(0, n_iter, body, 0)
    for rem in range(KEY_BLOCKS_PER_ITER):
        @pl.when(i - (n_iter << shift) == rem)
        def _(rem=rem):
            run(i - rem, rem)

    for h in range(FOX_HEADS):
        acc = acc_ref[h]
        o_ref[0, 0, h * FOX_HEAD_DIM:(h + 1) * FOX_HEAD_DIM, :] = (
            acc[:FOX_HEAD_DIM] / acc[FOX_HEAD_DIM:FOX_HEAD_DIM + 1])


def _fox_prompt(qt, ka, vt, *, blk):
    B, S, W = ka.shape
    nblk = S // blk
    return pl.pallas_call(
        functools.partial(_fox_prompt_kernel, blk=blk),
        grid=(B, nblk),
        in_specs=[pl.BlockSpec((1, 1, W, blk), lambda b, i: (b, i, 0, 0)),
                  pl.BlockSpec((1, S, W), lambda b, i: (b, 0, 0)),
                  pl.BlockSpec((1, nblk, FOX_HEADS * V_SLAB, blk), lambda b, i: (b, 0, 0, 0))],
        out_specs=pl.BlockSpec((1, 1, FOX_WIDTH, blk), lambda b, i: (b, i, 0, 0)),
        out_shape=jax.ShapeDtypeStruct((B, nblk, FOX_WIDTH, blk), F32),
        scratch_shapes=[pltpu.VMEM((FOX_HEADS, 1, blk), F32), pltpu.VMEM((FOX_HEADS, V_SLAB, blk), F32)],
        compiler_params=pltpu.CompilerParams(
            dimension_semantics=("parallel", "arbitrary"), vmem_limit_bytes=VMEM_LIMIT),
        name="fox_prompt",
    )(qt, ka, vt)


def _bias_piece_rows(c):
    b = c * (-LOG2E)
    p0 = b.astype(BF16).astype(F32)
    r1 = b - p0
    p1 = r1.astype(BF16).astype(F32)
    p2 = (r1 - p1).astype(BF16).astype(F32)
    return jnp.concatenate([p0, p1, p2], axis=0)


def _fox_sample_kernel(q_ref, kn_ref, vn_ref, lfn_ref, kt_ref, vt_ref, lft_ref, o_ref, *, chunk):
    T = q_ref.shape[1]
    P = lft_ref.shape[2]
    HT = FOX_HEADS * T
    t_shift = T.bit_length() - 1
    assert HT == LANES and 1 << t_shift == T

    q_rep = jnp.concatenate([q_ref[0]] * FOX_HEADS, axis=0)
    row_head = _iota((HT, FOX_WIDTH), 0) >> t_shift
    q_bd = jnp.where(row_head == (_iota((HT, FOX_WIDTH), 1) >> 6), q_rep, 0.0).astype(BF16)
    e_lane = _iota((HT, LANES), 1)
    e_bd = jnp.where((e_lane < BIAS_LANES) & ((e_lane & 7) == (_iota((HT, LANES), 0) >> t_shift)),
                     1.0, 0.0).astype(BF16)

    def bias_operand(c):
        rows = _bias_piece_rows(c)
        return jnp.concatenate([rows, jnp.zeros((LANES - BIAS_LANES, c.shape[1]), F32)], axis=0).astype(BF16)

    def running_sum(lf, carry, upper):
        pieces = jnp.concatenate(list(_split3(lf)) + [jnp.zeros(lf.shape, BF16)], axis=0)
        loc = _dot(pieces, upper)
        return loc[0:8] + loc[8:16] + loc[16:24] + carry

    upper = (_iota((chunk, chunk), 0) <= _iota((chunk, chunk), 1)).astype(BF16)
    carry = jnp.zeros((FOX_HEADS, 1), F32)
    c_parts = []
    for ci in range(P // chunk):
        c = running_sum(lft_ref[0, :, ci * chunk:(ci + 1) * chunk], carry, upper)
        carry = c[:, chunk - 1:chunk]
        c_parts.append(c)
    kt = kt_ref[0].reshape(FOX_WIDTH, P).astype(BF16)
    s_old = _dot(q_bd, kt) + _dot(e_bd, bias_operand(jnp.concatenate(c_parts, axis=1)))

    pad_lanes = lambda a: jnp.concatenate([a, jnp.zeros((a.shape[0], LANES - a.shape[1]), F32)], axis=1)
    pad_rows = lambda a: jnp.concatenate([a, jnp.zeros((LANES - T, a.shape[1]), F32)], axis=0)
    upper_new = (_iota((LANES, LANES), 0) <= _iota((LANES, LANES), 1)).astype(BF16)
    c_new = running_sum(pad_lanes(lfn_ref[0]), carry, upper_new)
    s_new = _dot_nt(q_bd, pad_rows(kn_ref[0]).astype(BF16)) + _dot(e_bd, bias_operand(c_new))
    s_new = jnp.where(_iota((HT, LANES), 1) <= (_iota((HT, LANES), 0) & (T - 1)), s_new, -jnp.inf)

    m = jnp.maximum(jnp.max(s_old, axis=1, keepdims=True), jnp.max(s_new, axis=1, keepdims=True))
    p_old = jnp.exp2(s_old - m).astype(BF16)
    p_new = jnp.exp2(s_new - m).astype(BF16)
    l = (jnp.sum(p_old.astype(F32), axis=1, keepdims=True)
         + jnp.sum(p_new.astype(F32), axis=1, keepdims=True))
    r = (_dot_nt(p_old, vt_ref[0].reshape(FOX_WIDTH, P).astype(BF16))
         + _dot(p_new, pad_rows(vn_ref[0]).astype(BF16))) / l
    col_head = _iota((T, FOX_WIDTH), 1) >> 6
    o = jnp.zeros((T, FOX_WIDTH), F32)
    for h in range(FOX_HEADS):
        o = jnp.where(col_head == h, r[h * T:(h + 1) * T, :], o)
    o_ref[0] = o.astype(BF16)


def _fox_sample(q, k_new, v_new, lf_new_t, cache_kt, cache_vt, cache_lft, *, chunk):
    B, T, _ = q.shape
    P = cache_lft.shape[2]
    new = lambda w: pl.BlockSpec((1, T, w), lambda b: (b, 0, 0))
    kv_spec = pl.BlockSpec((1, FOX_HEADS, FOX_HEAD_DIM, P), lambda b: (b, 0, 0, 0))
    return pl.pallas_call(
        functools.partial(_fox_sample_kernel, chunk=chunk),
        grid=(B,),
        in_specs=[new(FOX_WIDTH), new(FOX_WIDTH), new(FOX_WIDTH),
                  pl.BlockSpec((1, FOX_HEADS, T), lambda b: (b, 0, 0)), kv_spec, kv_spec,
                  pl.BlockSpec((1, FOX_HEADS, P), lambda b: (b, 0, 0))],
        out_specs=new(FOX_WIDTH),
        out_shape=jax.ShapeDtypeStruct((B, T, FOX_WIDTH), BF16),
        compiler_params=pltpu.CompilerParams(dimension_semantics=("parallel",),
                                             vmem_limit_bytes=VMEM_LIMIT),
        name="fox_sample",
    )(q, k_new, v_new, lf_new_t, cache_kt, cache_vt, cache_lft)


def _merge_ffn_kernel(x_ref, ofox_ref, opool_ref, omem_ref, ga_ref, wg_ref, wbf_ref, wbp_ref, wbm_ref,
                      wout_ref, gm_ref, wup_ref, wdown_ref, gfin_ref, y_ref, *, ff_chunk, final_norm):
    D = x_ref.shape[-1]
    streams = ofox_ref.shape[1]
    rows = x_ref.shape[1] // streams
    groups = [slice(i * rows, (i + 1) * rows) for i in range(streams)]

    xs = [x_ref[0, r, :] for r in groups]
    pre = []
    for si, (x, r) in enumerate(zip(xs, groups)):
        v1 = _dot(opool_ref[0, r, :], wbp_ref[...])
        v2 = _dot(omem_ref[0, r, :], wbm_ref[...])
        v0 = _dot(ofox_ref[0, si].T.astype(BF16), wbf_ref[...])
        xn = _rms(x, ga_ref[...]).astype(BF16)
        gates = [_dot(xn, wg_ref[:, bi * D:(bi + 1) * D]) for bi in range(N_BRANCH)]
        pre.append((gates, (v0, v1, v2)))
    hs = []
    for x, ((g0, g1, g2), (v0, v1, v2)) in zip(xs, pre):
        m = jax.nn.sigmoid(g0) * v0 + jax.nn.sigmoid(g1) * v1 + jax.nn.sigmoid(g2) * v2
        hs.append(x + _dot(m.astype(BF16), wout_ref[...]))
    hns = [_rms(h, gm_ref[...]).astype(BF16) for h in hs]
    accs = list(hs)
    for c in range(wup_ref.shape[1] // ff_chunk):
        cols = slice(c * ff_chunk, (c + 1) * ff_chunk)
        ups = [jnp.maximum(_dot(hn, wup_ref[:, cols]), 0.0) for hn in hns]
        accs = [acc + _dot((a * a).astype(BF16), wdown_ref[cols, :]) for acc, a in zip(accs, ups)]
    for acc, r in zip(accs, groups):
        y_ref[0, r, :] = _rms(acc, gfin_ref[...]) if final_norm else acc


def _merge_ffn(x, ofox_t, opool, omem, ga, wg, wbf, wbp, wbm, wout, gm, wup, wdown, gfin, *, tm,
               ff_chunk, final_norm):
    B, S, D = x.shape
    blk = ofox_t.shape[3]
    tile = lambda w: pl.BlockSpec((1, tm, w), lambda b, t: (b, t, 0))
    fox_tile = pl.BlockSpec((1, tm // blk, FOX_WIDTH, blk), lambda b, t: (b, t, 0, 0))
    const = lambda a: pl.BlockSpec(a.shape, lambda b, t: (0,) * a.ndim, pipeline_mode=pl.Buffered(1))
    consts = (ga, wg, wbf, wbp, wbm, wout, gm, wup, wdown, gfin)
    return pl.pallas_call(
        functools.partial(_merge_ffn_kernel, ff_chunk=ff_chunk, final_norm=final_norm),
        grid=(B, S // tm),
        in_specs=[tile(D), fox_tile, tile(POOL_WIDTH), tile(MEM_WIDTH)] + [const(a) for a in consts],
        out_specs=tile(D),
        out_shape=jax.ShapeDtypeStruct((B, S, D), F32),
        compiler_params=pltpu.CompilerParams(dimension_semantics=("parallel", "parallel"),
                                             vmem_limit_bytes=VMEM_LIMIT),
        name="merge_ffn",
    )(x, ofox_t, opool, omem, *consts)


def _tile_rows(n, want):
    return want if n % want == 0 else n


def _repack_w_in(w_in):
    o_f = 3 * FOX_WIDTH
    o_u = o_f + FOX_HEADS
    o_qm = o_u + POOL_WIDTH
    o_g = o_qm + MEM_WIDTH
    b16 = lambda a: a.astype(BF16)
    wq, wk, wv = (w_in[:, i * FOX_WIDTH:(i + 1) * FOX_WIDTH] for i in range(3))
    wf_rep = _rep_lanes(w_in[:, o_f:o_u])
    weights = (b16(wk), b16(wv), b16(w_in[:, o_u:o_qm]), b16(w_in[:, o_qm:o_g]), b16(wf_rep), b16(wq).T)
    return weights, b16(w_in[:, o_g:])


def _rep_lanes(a):
    pad = jnp.zeros(a.shape[:-1] + (LANES - BIAS_LANES,), a.dtype)
    return jnp.concatenate([a] * BIAS_PIECES + [pad], axis=-1)


def kernel(x_prompt, x_sample, mem_prompt, cache_fox_k, cache_fox_v, cache_fox_logf, cache_pool,
           cache_mem_k, cache_mem_v, g_attn, w_in, b_f, w_pool, s_pool, g_mem, w_mkv,
           w_br_fox, w_br_pool, w_br_mem, w_out, g_mlp, w_up, w_down, g_final):
    depth = w_in.shape[0]
    B, S, D = x_prompt.shape
    BS, T, _ = x_sample.shape
    P = cache_fox_k.shape[2]
    M = mem_prompt.shape[1]
    tm_p = _tile_rows(S, 512)
    blk_p = _tile_rows(tm_p, 256)
    tm_s = _tile_rows(BS * T, 256)
    chunk_s = _tile_rows(P, 256)
    gfin = g_final.reshape(1, D)

    xp, xs = x_prompt, x_sample
    outs_p = [[] for _ in range(6)]
    outs_s = [[] for _ in range(4)]
    for l in range(depth):
        weights, wg = _repack_w_in(w_in[l])
        bf_rep = _rep_lanes(b_f[l]).reshape(1, LANES)
        g_a, g_m, g_f = g_attn[l].reshape(1, D), g_mem[l].reshape(1, D), g_mlp[l].reshape(1, D)
        wpool, spool = w_pool[l].astype(BF16), s_pool[l].reshape(1, POOL_WIDTH)
        wbf, wbp, wbm = (w.astype(BF16) for w in (w_br_fox[l], w_br_pool[l], w_br_mem[l]))
        wout, wup, wdown = (w.astype(BF16) for w in (w_out[l], w_up[l], w_down[l]))
        final = l == depth - 1
        post = functools.partial(_merge_ffn, ga=g_a, wg=wg, wbf=wbf, wbp=wbp, wbm=wbm, wout=wout, gm=g_f,
                                 wup=wup, wdown=wdown, gfin=gfin, ff_chunk=1024, final_norm=final)

        hist = jnp.pad(cache_pool[l], ((0, 0), (HIST_ROWS - POOL_HIST, 0), (0, 0)))
        cmk = cache_mem_k[l].reshape(BS, M * MEM_HEADS, MEM_HEAD_DIM)
        cmv = cache_mem_v[l].reshape(BS, M * MEM_HEADS, MEM_HEAD_DIM)
        k, v, logf_t, pnew, opool, omem, q = _in_proj(
            xs, g_a, weights, bf_rep, hist, cmk, cmv, wpool, spool, tm=T, pos0=P, augment=False)
        logf = jnp.transpose(logf_t, (0, 2, 1))
        ofox = _fox_sample(q, k, v, logf_t,
                           jnp.transpose(cache_fox_k[l], (0, 2, 3, 1)), jnp.transpose(cache_fox_v[l], (0, 2, 3, 1)),
                           jnp.transpose(cache_fox_logf[l], (0, 2, 1)), chunk=chunk_s)
        flat = lambda a: a.reshape(1, BS * T, a.shape[-1])
        ofox_t = jnp.transpose(ofox.reshape(BS * T // tm_s, tm_s, FOX_WIDTH), (0, 2, 1))[None].astype(F32)
        xs = post(flat(xs), ofox_t, flat(opool), flat(omem), tm=tm_s).reshape(BS, T, D)
        for acc, val in zip(outs_s, (k.reshape(BS, T, FOX_HEADS, FOX_HEAD_DIM),
                                     v.reshape(BS, T, FOX_HEADS, FOX_HEAD_DIM), logf, pnew)):
            acc.append(val)

        mk, mv = _mem_kv(mem_prompt, g_m, w_mkv[l].astype(BF16))
        hist0 = jnp.zeros((B, HIST_ROWS, POOL_WIDTH), F32)
        k, v, logf_t, pnew, opool, omem, qt, ka, vt = _in_proj(
            xp, g_a, weights, bf_rep, hist0, mk, mv, wpool, spool, tm=tm_p, pos0=0, augment=True,
            blk=blk_p)
        logf = jnp.transpose(logf_t, (0, 2, 1))
        ofox = _fox_prompt(qt, ka, vt, blk=blk_p)
        xp = post(xp, ofox, opool, omem, tm=tm_p)
        for acc, val in zip(outs_p, (k.reshape(B, S, FOX_HEADS, FOX_HEAD_DIM),
                                     v.reshape(B, S, FOX_HEADS, FOX_HEAD_DIM), logf, pnew,
                                     mk.reshape(B, M, MEM_HEADS, MEM_HEAD_DIM),
                                     mv.reshape(B, M, MEM_HEADS, MEM_HEAD_DIM))):
            acc.append(val)

    stack = lambda o: o[0][None] if depth == 1 else jnp.stack(o)
    return (xp, xs, *(stack(o) for o in outs_p), *(stack(o) for o in outs_s))
```

```python
import functools
import math

import jax
import jax.numpy as jnp
from jax import lax
from jax.experimental import pallas as pl
from jax.experimental.pallas import tpu as pltpu

F32 = jnp.float32
BF16 = jnp.bfloat16

EPS = 1e-6
LANES = 128
FOX_HEADS = 8
FOX_HEAD_DIM = 64
FOX_WIDTH = FOX_HEADS * FOX_HEAD_DIM
POOL_WINDOWS = (2, 4, 8, 16)
POOL_GROUPS = 4
POOL_GROUP_DIM = 128
POOL_WIDTH = POOL_GROUPS * POOL_GROUP_DIM
POOL_HIST = 15
HIST_ROWS = 16
MEM_HEADS = 4
MEM_HEAD_DIM = 128
MEM_WIDTH = MEM_HEADS * MEM_HEAD_DIM
N_BRANCH = 3
LOG2E = math.log2(math.e)
BIAS_PIECES = 3
BIAS_LANES = BIAS_PIECES * FOX_HEADS
V_SLAB = 80
KEY_BLOCKS_PER_ITER = 8
KEY_BLOCKS_PER_ITEM = 1
SCORES_AHEAD = 5
CUMSUM_CHUNK = 128
VMEM_LIMIT = 56 * 1024 * 1024


def _dot(a, b):
    return jnp.dot(a, b, preferred_element_type=F32)


def _dot_nt(a, b):
    return lax.dot_general(a, b, (((1,), (1,)), ((), ())), preferred_element_type=F32)


def _rms(x, g):
    return x * lax.rsqrt(jnp.mean(x * x, axis=-1, keepdims=True) + EPS) * g


def _iota(shape, dim):
    return lax.broadcasted_iota(jnp.int32, shape, dim)


def _split3(x):
    p0 = x.astype(BF16)
    r1 = x - p0.astype(F32)
    p1 = r1.astype(BF16)
    p2 = (r1 - p1.astype(F32)).astype(BF16)
    return p0, p1, p2


def _cumsum_rows(x, carry):
    n = x.shape[0]
    ch = min(n, CUMSUM_CHUNK)
    tril = (_iota((ch, ch), 1) <= _iota((ch, ch), 0)).astype(BF16)
    outs = []
    for c0 in range(0, n, ch):
        r = _dot(tril, jnp.concatenate(_split3(x[c0:c0 + ch]), axis=1))
        c = r[:, :LANES] + r[:, LANES:2 * LANES] + r[:, 2 * LANES:] + carry
        carry = c[ch - 1:ch]
        outs.append(c)
    return (outs[0] if len(outs) == 1 else jnp.concatenate(outs, axis=0)), carry


def _bias_pieces(c):
    b = c * (-LOG2E)
    p0 = b.astype(BF16).astype(F32)
    r1 = b - p0
    p1 = r1.astype(BF16).astype(F32)
    p2 = (r1 - p1).astype(BF16).astype(F32)
    grp = _iota(c.shape, 1) >> 3
    return jnp.where(grp == 0, p0, jnp.where(grp == 1, p1, jnp.where(grp == 2, p2, 0.0)))


def _mem_kv_kernel(mem_ref, g_ref, w_ref, mk_ref, mv_ref):
    M = mem_ref.shape[1]
    xn = _rms(mem_ref[0], g_ref[...]).astype(BF16)
    for out_ref, off in ((mk_ref, 0), (mv_ref, MEM_WIDTH)):
        kv = _dot(xn, w_ref[:, off:off + MEM_WIDTH])
        for h in range(MEM_HEADS):
            out_ref[0, pl.ds(h, M, stride=MEM_HEADS), :] = kv[:, h * MEM_HEAD_DIM:(h + 1) * MEM_HEAD_DIM]


def _mem_kv(mem, g, w_mkv):
    B, M, D = mem.shape
    return pl.pallas_call(
        _mem_kv_kernel,
        grid=(B,),
        in_specs=[pl.BlockSpec((1, M, D), lambda b: (b, 0, 0)), pl.BlockSpec((1, D), lambda b: (0, 0)),
                  pl.BlockSpec((D, 2 * MEM_WIDTH), lambda b: (0, 0))],
        out_specs=[pl.BlockSpec((1, M * MEM_HEADS, MEM_HEAD_DIM), lambda b: (b, 0, 0))] * 2,
        out_shape=[jax.ShapeDtypeStruct((B, M * MEM_HEADS, MEM_HEAD_DIM), F32)] * 2,
        compiler_params=pltpu.CompilerParams(dimension_semantics=("parallel",),
                                             vmem_limit_bytes=VMEM_LIMIT),
        name="mem_kv",
    )(mem, g, w_mkv)


def _in_proj_kernel(x_ref, g_ref, wk_ref, wv_ref, wu_ref, wqm_ref, wf_ref, wqt_ref, bf_ref, hist_ref,
                    mk_ref, mv_ref, wpool_ref, spool_ref,
                    k_ref, v_ref, logf_ref, pnew_ref, opool_ref, omem_ref, *rest, tm, pos0, augment, blk):
    extra, (uext_ref, carry_ref) = rest[:-2], rest[-2:]
    t = pl.program_id(1)
    last = pl.num_programs(1) - 1
    q_scale = FOX_HEAD_DIM ** -0.5 * LOG2E

    @pl.when(t == 0)
    def _():
        uext_ref[0:HIST_ROWS, :] = hist_ref[0]
        carry_ref[...] = jnp.zeros_like(carry_ref)

    xn = _rms(x_ref[0], g_ref[...]).astype(BF16)
    lane = _iota((tm, LANES), 1)
    mem_cols = [slice(h * MEM_HEAD_DIM, (h + 1) * MEM_HEAD_DIM) for h in range(MEM_HEADS)]

    zf = _dot(xn, wf_ref[...]) + bf_ref[...]
    qm = _dot(xn, wqm_ref[...]).astype(BF16)
    k = _dot(xn, wk_ref[...])
    logf = jnp.minimum(zf, 0.0) - jnp.log1p(jnp.exp(-jnp.abs(zf)))
    n_mem = mk_ref.shape[1] // MEM_HEADS
    mem_head = lambda ref, h: ref[0, pl.ds(h, n_mem, stride=MEM_HEADS), :].astype(BF16)
    mem_s = [_dot_nt(qm[:, cols], mem_head(mk_ref, h)) * MEM_HEAD_DIM ** -0.5
             for h, cols in enumerate(mem_cols)]
    v = _dot(xn, wv_ref[...])
    u = _dot(xn, wu_ref[...])
    if augment:
        c, carry_ref[...] = _cumsum_rows(logf, carry_ref[...])
        qt_all = (_dot_nt(wqt_ref[...], xn) * q_scale).astype(BF16)
    else:
        q = _dot_nt(xn, wqt_ref[...]) * q_scale
    for h, (s, cols) in enumerate(zip(mem_s, mem_cols)):
        e = jnp.exp(s - jnp.max(s, axis=-1, keepdims=True))
        o = _dot(e.astype(BF16), mem_head(mv_ref, h)) / jnp.sum(e, axis=-1, keepdims=True)
        omem_ref[0, :, cols] = o.astype(BF16)
    if augment:
        vt_all = v.T.astype(BF16)

    k_ref[0] = k
    v_ref[0] = v
    logf_rows = logf if tm >= LANES else jnp.concatenate([logf, jnp.zeros((LANES - tm, LANES), F32)], axis=0)
    logf_ref[0] = logf_rows.T[:FOX_HEADS, :tm]

    if augment:
        qt_ref, ka_ref, vt_ref = extra
        bias = pltpu.roll(_bias_pieces(c), FOX_HEAD_DIM, axis=1)
        in_bias = (lane >= FOX_HEAD_DIM) & (lane < FOX_HEAD_DIM + BIAS_LANES)
        for pair in range(FOX_HEADS // 2):
            even = k[:, pair * LANES:(pair + 1) * LANES]
            odd = pltpu.roll(even, FOX_HEAD_DIM, axis=1)
            for par, data in ((0, even), (1, odd)):
                h = 2 * pair + par
                tail = jnp.where(in_bias & ((lane & 7) == h), bias, 0.0)
                ka_ref[0, :, h * LANES:(h + 1) * LANES] = jnp.where(lane < FOX_HEAD_DIM, data, tail).astype(BF16)
        q_tail = jnp.where(_iota((LANES - FOX_HEAD_DIM, blk), 0) < BIAS_LANES, 1.0, 0.0).astype(BF16)
        v_tail = jnp.where(_iota((V_SLAB - FOX_HEAD_DIM, blk), 0) == 0, 1.0, 0.0).astype(BF16)
        for h in range(FOX_HEADS):
            qt = qt_all[h * FOX_HEAD_DIM:(h + 1) * FOX_HEAD_DIM]
            vt = vt_all[h * FOX_HEAD_DIM:(h + 1) * FOX_HEAD_DIM]
            for j in range(tm // blk):
                qt_ref[0, j, h * LANES:h * LANES + FOX_HEAD_DIM, :] = qt[:, j * blk:(j + 1) * blk]
                qt_ref[0, j, h * LANES + FOX_HEAD_DIM:(h + 1) * LANES, :] = q_tail
                vt_ref[0, j, h * V_SLAB:h * V_SLAB + FOX_HEAD_DIM, :] = vt[:, j * blk:(j + 1) * blk]
                vt_ref[0, j, h * V_SLAB + FOX_HEAD_DIM:(h + 1) * V_SLAB, :] = v_tail
    else:
        extra[0][0] = q

    uext_ref[HIST_ROWS:HIST_ROWS + tm, :] = u
    pos = pos0 + t * tm + _iota((tm, 1), 0)
    for gi, w in enumerate(POOL_WINDOWS):
        cols = slice(gi * POOL_GROUP_DIM, (gi + 1) * POOL_GROUP_DIM)
        sw = uext_ref[:, cols]
        shift = 1
        while shift < w:
            sw = sw + pltpu.roll(sw, shift, axis=0)
            shift *= 2
        cnt = jnp.minimum(w, pos + 1).astype(F32)
        pooled = sw[HIST_ROWS:, :] / cnt - u[:, cols]
        mixed = _dot(pooled.astype(BF16), wpool_ref[gi]) * spool_ref[:, cols]
        opool_ref[0, :, cols] = mixed.astype(BF16)

    @pl.when(t == last)
    def _():
        pnew_ref[0] = uext_ref[tm + 1:tm + HIST_ROWS, :]

    uext_ref[0:HIST_ROWS, :] = uext_ref[tm:tm + HIST_ROWS, :]


def _in_proj(x, g, weights, bf_rep, hist, mk, mv, wpool, spool, *, tm, pos0, augment, blk=None):
    B, S, D = x.shape
    nt = S // tm
    tile = lambda w: pl.BlockSpec((1, tm, w), lambda b, t: (b, t, 0))
    per_b = lambda r, w: pl.BlockSpec((1, r, w), lambda b, t: (b, 0, 0))
    const = lambda shape: pl.BlockSpec(shape, lambda b, t: (0,) * len(shape))
    if augment:
        W = FOX_HEADS * LANES
        t_tile = lambda rows: pl.BlockSpec((1, tm // blk, rows, blk), lambda b, t: (b, t, 0, 0))
        abc_specs = [t_tile(W), tile(W), t_tile(FOX_HEADS * V_SLAB)]
        abc_shapes = [jax.ShapeDtypeStruct((B, S // blk, W, blk), BF16),
                      jax.ShapeDtypeStruct((B, S, W), BF16),
                      jax.ShapeDtypeStruct((B, S // blk, FOX_HEADS * V_SLAB, blk), BF16)]
    else:
        abc_specs = [tile(FOX_WIDTH)]
        abc_shapes = [jax.ShapeDtypeStruct((B, S, FOX_WIDTH), F32)]
    return pl.pallas_call(
        functools.partial(_in_proj_kernel, tm=tm, pos0=pos0, augment=augment, blk=blk),
        grid=(B, nt),
        in_specs=[tile(D), const((1, D))] + [const(w.shape) for w in weights] + [
            const((1, LANES)), per_b(HIST_ROWS, POOL_WIDTH), per_b(*mk.shape[1:]), per_b(*mv.shape[1:]),
            const((POOL_GROUPS, POOL_GROUP_DIM, POOL_GROUP_DIM)), const((1, POOL_WIDTH))],
        out_specs=[tile(FOX_WIDTH), tile(FOX_WIDTH), pl.BlockSpec((1, FOX_HEADS, tm), lambda b, t: (b, 0, t)),
                   per_b(POOL_HIST, POOL_WIDTH), tile(POOL_WIDTH), tile(MEM_WIDTH)] + abc_specs,
        out_shape=[jax.ShapeDtypeStruct((B, S, FOX_WIDTH), F32),
                   jax.ShapeDtypeStruct((B, S, FOX_WIDTH), F32),
                   jax.ShapeDtypeStruct((B, FOX_HEADS, S), F32),
                   jax.ShapeDtypeStruct((B, POOL_HIST, POOL_WIDTH), F32),
                   jax.ShapeDtypeStruct((B, S, POOL_WIDTH), BF16),
                   jax.ShapeDtypeStruct((B, S, MEM_WIDTH), BF16)] + abc_shapes,
        scratch_shapes=[pltpu.VMEM((HIST_ROWS + tm, POOL_WIDTH), F32), pltpu.VMEM((1, LANES), F32)],
        compiler_params=pltpu.CompilerParams(dimension_semantics=("parallel", "arbitrary"),
                                             vmem_limit_bytes=VMEM_LIMIT),
        name="in_proj",
    )(x, g, *weights, bf_rep, hist, mk, mv, wpool, spool)


def _fox_prompt_kernel(qt_ref, k_ref, vt_ref, o_ref, m_ref, acc_ref, *, blk):
    i = pl.program_id(1)
    m_ref[...] = jnp.full_like(m_ref, -jnp.inf)
    acc_ref[...] = jnp.zeros_like(acc_ref)
    slabs = [slice(h * LANES, (h + 1) * LANES) for h in range(FOX_HEADS)]

    def scores(j, nb, h):
        start = pl.multiple_of(j * blk, blk)
        return _dot(k_ref[0, pl.ds(start, nb * blk), slabs[h]], qt_ref[0, 0, slabs[h], :])

    def accumulate(j, nb, h, st, masked):
        if masked:
            st = jnp.where(_iota(st.shape, 0) <= _iota(st.shape, 1), st, -jnp.inf)
        m_prev = m_ref[h]
        m_new = jnp.maximum(m_prev, jnp.max(st, axis=0, keepdims=True))
        p = jnp.exp2(st - m_new).astype(BF16)
        pv = _dot(vt_ref[0, j, h * V_SLAB:(h + 1) * V_SLAB, :], p[:blk])
        for r in range(1, nb):
            pv = pv + _dot(vt_ref[0, j + r, h * V_SLAB:(h + 1) * V_SLAB, :], p[r * blk:(r + 1) * blk])
        acc_ref[h] = jnp.exp2(m_prev - m_new) * acc_ref[h] + pv
        m_ref[h] = m_new

    def run(first, n_visible):
        spans = [(r, min(KEY_BLOCKS_PER_ITEM, n_visible - r), False)
                 for r in range(0, n_visible, KEY_BLOCKS_PER_ITEM)]
        if n_visible < KEY_BLOCKS_PER_ITER:
            spans.append((n_visible, 1, True))
        work = [(first + r, nb, h, masked) for r, nb, masked in spans for h in range(FOX_HEADS)]
        pending = [scores(*w[:3]) for w in work[:SCORES_AHEAD]]
        for n, (j, nb, h, masked) in enumerate(work):
            if n + SCORES_AHEAD < len(work):
                pending.append(scores(*work[n + SCORES_AHEAD][:3]))
            accumulate(j, nb, h, pending.pop(0), masked)

    shift = KEY_BLOCKS_PER_ITER.bit_length() - 1
    n_iter = i >> shift

    def body(jj, carry):
        run(KEY_BLOCKS_PER_ITER * jj, KEY_BLOCKS_PER_ITER)
        return carry

    lax.fori_loop(0, n_iter, body, 0)
    for rem in range(KEY_BLOCKS_PER_ITER):
        @pl.when(i - (n_iter << shift) == rem)
        def _(rem=rem):
            run(i - rem, rem)

    for h in range(FOX_HEADS):
        acc = acc_ref[h]
        o_ref[0, 0, h * FOX_HEAD_DIM:(h + 1) * FOX_HEAD_DIM, :] = (
            acc[:FOX_HEAD_DIM] / acc[FOX_HEAD_DIM:FOX_HEAD_DIM + 1])


def _fox_prompt(qt, ka, vt, *, blk):
    B, S, W = ka.shape
    nblk = S // blk
    return pl.pallas_call(
        functools.partial(_fox_prompt_kernel, blk=blk),
        grid=(B, nblk),
        in_specs=[pl.BlockSpec((1, 1, W, blk), lambda b, i: (b, i, 0, 0)),
                  pl.BlockSpec((1, S, W), lambda b, i: (b, 0, 0)),
                  pl.BlockSpec((1, nblk, FOX_HEADS * V_SLAB, blk), lambda b, i: (b, 0, 0, 0))],
        out_specs=pl.BlockSpec((1, 1, FOX_WIDTH, blk), lambda b, i: (b, i, 0, 0)),
        out_shape=jax.ShapeDtypeStruct((B, nblk, FOX_WIDTH, blk), F32),
        scratch_shapes=[pltpu.VMEM((FOX_HEADS, 1, blk), F32), pltpu.VMEM((FOX_HEADS, V_SLAB, blk), F32)],
        compiler_params=pltpu.CompilerParams(
            dimension_semantics=("parallel", "arbitrary"), vmem_limit_bytes=VMEM_LIMIT),
        name="fox_prompt",
    )(qt, ka, vt)


def _bias_piece_rows(c):
    b = c * (-LOG2E)
    p0 = b.astype(BF16).astype(F32)
    r1 = b - p0
    p1 = r1.astype(BF16).astype(F32)
    p2 = (r1 - p1).astype(BF16).astype(F32)
    return jnp.concatenate([p0, p1, p2], axis=0)


def _fox_sample_kernel(q_ref, kn_ref, vn_ref, lfn_ref, kt_ref, vt_ref, lft_ref, o_ref, *, chunk):
    T = q_ref.shape[1]
    P = lft_ref.shape[2]
    HT = FOX_HEADS * T
    t_shift = T.bit_length() - 1
    assert HT == LANES and 1 << t_shift == T

    q_rep = jnp.concatenate([q_ref[0]] * FOX_HEADS, axis=0)
    row_head = _iota((HT, FOX_WIDTH), 0) >> t_shift
    q_bd = jnp.where(row_head == (_iota((HT, FOX_WIDTH), 1) >> 6), q_rep, 0.0).astype(BF16)
    e_lane = _iota((HT, LANES), 1)
    e_bd = jnp.where((e_lane < BIAS_LANES) & ((e_lane & 7) == (_iota((HT, LANES), 0) >> t_shift)),
                     1.0, 0.0).astype(BF16)

    def bias_operand(c):
        rows = _bias_piece_rows(c)
        return jnp.concatenate([rows, jnp.zeros((LANES - BIAS_LANES, c.shape[1]), F32)], axis=0).astype(BF16)

    def running_sum(lf, carry, upper):
        pieces = jnp.concatenate(list(_split3(lf)) + [jnp.zeros(lf.shape, BF16)], axis=0)
        loc = _dot(pieces, upper)
        return loc[0:8] + loc[8:16] + loc[16:24] + carry

    upper = (_iota((chunk, chunk), 0) <= _iota((chunk, chunk), 1)).astype(BF16)
    carry = jnp.zeros((FOX_HEADS, 1), F32)
    c_parts = []
    for ci in range(P // chunk):
        c = running_sum(lft_ref[0, :, ci * chunk:(ci + 1) * chunk], carry, upper)
        carry = c[:, chunk - 1:chunk]
        c_parts.append(c)
    kt = kt_ref[0].reshape(FOX_WIDTH, P).astype(BF16)
    s_old = _dot(q_bd, kt) + _dot(e_bd, bias_operand(jnp.concatenate(c_parts, axis=1)))

    pad_lanes = lambda a: jnp.concatenate([a, jnp.zeros((a.shape[0], LANES - a.shape[1]), F32)], axis=1)
    pad_rows = lambda a: jnp.concatenate([a, jnp.zeros((LANES - T, a.shape[1]), F32)], axis=0)
    upper_new = (_iota((LANES, LANES), 0) <= _iota((LANES, LANES), 1)).astype(BF16)
    c_new = running_sum(pad_lanes(lfn_ref[0]), carry, upper_new)
    s_new = _dot_nt(q_bd, pad_rows(kn_ref[0]).astype(BF16)) + _dot(e_bd, bias_operand(c_new))
    s_new = jnp.where(_iota((HT, LANES), 1) <= (_iota((HT, LANES), 0) & (T - 1)), s_new, -jnp.inf)

    m = jnp.maximum(jnp.max(s_old, axis=1, keepdims=True), jnp.max(s_new, axis=1, keepdims=True))
    p_old = jnp.exp2(s_old - m).astype(BF16)
    p_new = jnp.exp2(s_new - m).astype(BF16)
    l = (jnp.sum(p_old.astype(F32), axis=1, keepdims=True)
         + jnp.sum(p_new.astype(F32), axis=1, keepdims=True))
    r = (_dot_nt(p_old, vt_ref[0].reshape(FOX_WIDTH, P).astype(BF16))
         + _dot(p_new, pad_rows(vn_ref[0]).astype(BF16))) / l
    col_head = _iota((T, FOX_WIDTH), 1) >> 6
    o = jnp.zeros((T, FOX_WIDTH), F32)
    for h in range(FOX_HEADS):
        o = jnp.where(col_head == h, r[h * T:(h + 1) * T, :], o)
    o_ref[0] = o.astype(BF16)


def _fox_sample(q, k_new, v_new, lf_new_t, cache_kt, cache_vt, cache_lft, *, chunk):
    B, T, _ = q.shape
    P = cache_lft.shape[2]
    new = lambda w: pl.BlockSpec((1, T, w), lambda b: (b, 0, 0))
    kv_spec = pl.BlockSpec((1, FOX_HEADS, FOX_HEAD_DIM, P), lambda b: (b, 0, 0, 0))
    return pl.pallas_call(
        functools.partial(_fox_sample_kernel, chunk=chunk),
        grid=(B,),
        in_specs=[new(FOX_WIDTH), new(FOX_WIDTH), new(FOX_WIDTH),
                  pl.BlockSpec((1, FOX_HEADS, T), lambda b: (b, 0, 0)), kv_spec, kv_spec,
                  pl.BlockSpec((1, FOX_HEADS, P), lambda b: (b, 0, 0))],
        out_specs=new(FOX_WIDTH),
        out_shape=jax.ShapeDtypeStruct((B, T, FOX_WIDTH), BF16),
        compiler_params=pltpu.CompilerParams(dimension_semantics=("parallel",),
                                             vmem_limit_bytes=VMEM_LIMIT),
        name="fox_sample",
    )(q, k_new, v_new, lf_new_t, cache_kt, cache_vt, cache_lft)


def _merge_ffn_kernel(x_ref, ofox_ref, opool_ref, omem_ref, ga_ref, wg_ref, wbf_ref, wbp_ref, wbm_ref,
                      wout_ref, gm_ref, wup_ref, wdown_ref, gfin_ref, y_ref, *, ff_chunk, final_norm):
    D = x_ref.shape[-1]
    streams = ofox_ref.shape[1]
    rows = x_ref.shape[1] // streams
    groups = [slice(i * rows, (i + 1) * rows) for i in range(streams)]

    xs = [x_ref[0, r, :] for r in groups]
    pre = []
    for si, (x, r) in enumerate(zip(xs, groups)):
        v1 = _dot(opool_ref[0, r, :], wbp_ref[...])
        v2 = _dot(omem_ref[0, r, :], wbm_ref[...])
        v0 = _dot(ofox_ref[0, si].T.astype(BF16), wbf_ref[...])
        xn = _rms(x, ga_ref[...]).astype(BF16)
        gates = [_dot(xn, wg_ref[:, bi * D:(bi + 1) * D]) for bi in range(N_BRANCH)]
        pre.append((gates, (v0, v1, v2)))
    hs = []
    for x, ((g0, g1, g2), (v0, v1, v2)) in zip(xs, pre):
        m = jax.nn.sigmoid(g0) * v0 + jax.nn.sigmoid(g1) * v1 + jax.nn.sigmoid(g2) * v2
        hs.append(x + _dot(m.astype(BF16), wout_ref[...]))
    hns = [_rms(h, gm_ref[...]).astype(BF16) for h in hs]
    accs = list(hs)
    for c in range(wup_ref.shape[1] // ff_chunk):
        cols = slice(c * ff_chunk, (c + 1) * ff_chunk)
        ups = [jnp.maximum(_dot(hn, wup_ref[:, cols]), 0.0) for hn in hns]
        accs = [acc + _dot((a * a).astype(BF16), wdown_ref[cols, :]) for acc, a in zip(accs, ups)]
    for acc, r in zip(accs, groups):
        y_ref[0, r, :] = _rms(acc, gfin_ref[...]) if final_norm else acc


def _merge_ffn(x, ofox_t, opool, omem, ga, wg, wbf, wbp, wbm, wout, gm, wup, wdown, gfin, *, tm,
               ff_chunk, final_norm):
    B, S, D = x.shape
    blk = ofox_t.shape[3]
    tile = lambda w: pl.BlockSpec((1, tm, w), lambda b, t: (b, t, 0))
    fox_tile = pl.BlockSpec((1, tm // blk, FOX_WIDTH, blk), lambda b, t: (b, t, 0, 0))
    const = lambda a: pl.BlockSpec(a.shape, lambda b, t: (0,) * a.ndim, pipeline_mode=pl.Buffered(1))
    consts = (ga, wg, wbf, wbp, wbm, wout, gm, wup, wdown, gfin)
    return pl.pallas_call(
        functools.partial(_merge_ffn_kernel, ff_chunk=ff_chunk, final_norm=final_norm),
        grid=(B, S // tm),
        in_specs=[tile(D), fox_tile, tile(POOL_WIDTH), tile(MEM_WIDTH)] + [const(a) for a in consts],
        out_specs=tile(D),
        out_shape=jax.ShapeDtypeStruct((B, S, D), F32),
        compiler_params=pltpu.CompilerParams(dimension_semantics=("parallel", "parallel"),
                                             vmem_limit_bytes=VMEM_LIMIT),
        name="merge_ffn",
    )(x, ofox_t, opool, omem, *consts)


def _tile_rows(n, want):
    return want if n % want == 0 else n


def _repack_w_in(w_in):
    o_f = 3 * FOX_WIDTH
    o_u = o_f + FOX_HEADS
    o_qm = o_u + POOL_WIDTH
    o_g = o_qm + MEM_WIDTH
    b16 = lambda a: a.astype(BF16)
    wq, wk, wv = (w_in[:, i * FOX_WIDTH:(i + 1) * FOX_WIDTH] for i in range(3))
    wf_rep = _rep_lanes(w_in[:, o_f:o_u])
    weights = (b16(wk), b16(wv), b16(w_in[:, o_u:o_qm]), b16(w_in[:, o_qm:o_g]), b16(wf_rep), b16(wq).T)
    return weights, b16(w_in[:, o_g:])


def _rep_lanes(a):
    pad = jnp.zeros(a.shape[:-1] + (LANES - BIAS_LANES,), a.dtype)
    return jnp.concatenate([a] * BIAS_PIECES + [pad], axis=-1)


def kernel(x_prompt, x_sample, mem_prompt, cache_fox_k, cache_fox_v, cache_fox_logf, cache_pool,
           cache_mem_k, cache_mem_v, g_attn, w_in, b_f, w_pool, s_pool, g_mem, w_mkv,
           w_br_fox, w_br_pool, w_br_mem, w_out, g_mlp, w_up, w_down, g_final):
    depth = w_in.shape[0]
    B, S, D = x_prompt.shape
    BS, T, _ = x_sample.shape
    P = cache_fox_k.shape[2]
    M = mem_prompt.shape[1]
    tm_p = _tile_rows(S, 512)
    blk_p = _tile_rows(tm_p, 256)
    tm_s = _tile_rows(BS * T, 256)
    chunk_s = _tile_rows(P, 256)
    gfin = g_final.reshape(1, D)

    xp, xs = x_prompt, x_sample
    outs_p = [[] for _ in range(6)]
    outs_s = [[] for _ in range(4)]
    for l in range(depth):
        weights, wg = _repack_w_in(w_in[l])
        bf_rep = _rep_lanes(b_f[l]).reshape(1, LANES)
        g_a, g_m, g_f = g_attn[l].reshape(1, D), g_mem[l].reshape(1, D), g_mlp[l].reshape(1, D)
        wpool, spool = w_pool[l].astype(BF16), s_pool[l].reshape(1, POOL_WIDTH)
        wbf, wbp, wbm = (w.astype(BF16) for w in (w_br_fox[l], w_br_pool[l], w_br_mem[l]))
        wout, wup, wdown = (w.astype(BF16) for w in (w_out[l], w_up[l], w_down[l]))
        final = l == depth - 1
        post = functools.partial(_merge_ffn, ga=g_a, wg=wg, wbf=wbf, wbp=wbp, wbm=wbm, wout=wout, gm=g_f,
                                 wup=wup, wdown=wdown, gfin=gfin, ff_chunk=1024, final_norm=final)

        hist = jnp.pad(cache_pool[l], ((0, 0), (HIST_ROWS - POOL_HIST, 0), (0, 0)))
        cmk = cache_mem_k[l].reshape(BS, M * MEM_HEADS, MEM_HEAD_DIM)
        cmv = cache_mem_v[l].reshape(BS, M * MEM_HEADS, MEM_HEAD_DIM)
        k, v, logf_t, pnew, opool, omem, q = _in_proj(
            xs, g_a, weights, bf_rep, hist, cmk, cmv, wpool, spool, tm=T, pos0=P, augment=False)
        logf = jnp.transpose(logf_t, (0, 2, 1))
        ofox = _fox_sample(q, k, v, logf_t,
                           jnp.transpose(cache_fox_k[l], (0, 2, 3, 1)), jnp.transpose(cache_fox_v[l], (0, 2, 3, 1)),
                           jnp.transpose(cache_fox_logf[l], (0, 2, 1)), chunk=chunk_s)
        flat = lambda a: a.reshape(1, BS * T, a.shape[-1])
        ofox_t = jnp.transpose(ofox.reshape(BS * T // tm_s, tm_s, FOX_WIDTH), (0, 2, 1))[None].astype(F32)
        xs = post(flat(xs), ofox_t, flat(opool), flat(omem), tm=tm_s).reshape(BS, T, D)
        for acc, val in zip(outs_s, (k.reshape(BS, T, FOX_HEADS, FOX_HEAD_DIM),
                                     v.reshape(BS, T, FOX_HEADS, FOX_HEAD_DIM), logf, pnew)):
            acc.append(val)

        mk, mv = _mem_kv(mem_prompt, g_m, w_mkv[l].astype(BF16))
        hist0 = jnp.zeros((B, HIST_ROWS, POOL_WIDTH), F32)
        k, v, logf_t, pnew, opool, omem, qt, ka, vt = _in_proj(
            xp, g_a, weights, bf_rep, hist0, mk, mv, wpool, spool, tm=tm_p, pos0=0, augment=True,
            blk=blk_p)
        logf = jnp.transpose(logf_t, (0, 2, 1))
        ofox = _fox_prompt(qt, ka, vt, blk=blk_p)
        xp = post(xp, ofox, opool, omem, tm=tm_p)
        for acc, val in zip(outs_p, (k.reshape(B, S, FOX_HEADS, FOX_HEAD_DIM),
                                     v.reshape(B, S, FOX_HEADS, FOX_HEAD_DIM), logf, pnew,
                                     mk.reshape(B, M, MEM_HEADS, MEM_HEAD_DIM),
                                     mv.reshape(B, M, MEM_HEADS, MEM_HEAD_DIM))):
            acc.append(val)

    stack = lambda o: o[0][None] if depth == 1 else jnp.stack(o)
    return (xp, xs, *(stack(o) for o in outs_p), *(stack(o) for o in outs_s))
```

```python
import functools
import math

import jax
import jax.numpy as jnp
from jax import lax
from jax.experimental import pallas as pl
from jax.experimental.pallas import tpu as pltpu

F32 = jnp.float32
BF16 = jnp.bfloat16

EPS = 1e-6
LANES = 128
FOX_HEADS = 8
FOX_HEAD_DIM = 64
FOX_WIDTH = FOX_HEADS * FOX_HEAD_DIM
POOL_WINDOWS = (2, 4, 8, 16)
POOL_GROUPS = 4
POOL_GROUP_DIM = 128
POOL_WIDTH = POOL_GROUPS * POOL_GROUP_DIM
POOL_HIST = 15
HIST_ROWS = 16
MEM_HEADS = 4
MEM_HEAD_DIM = 128
MEM_WIDTH = MEM_HEADS * MEM_HEAD_DIM
N_BRANCH = 3
LOG2E = math.log2(math.e)
BIAS_PIECES = 3
BIAS_LANES = BIAS_PIECES * FOX_HEADS
V_SLAB = 80
KEY_BLOCKS_PER_ITER = 8
KEY_BLOCKS_PER_ITEM = 1
SCORES_AHEAD = 5
CUMSUM_CHUNK = 128
VMEM_LIMIT = 56 * 1024 * 1024


def _dot(a, b):
    return jnp.dot(a, b, preferred_element_type=F32)


def _dot_nt(a, b):
    return lax.dot_general(a, b, (((1,), (1,)), ((), ())), preferred_element_type=F32)


def _rms(x, g):
    return x * lax.rsqrt(jnp.mean(x * x, axis=-1, keepdims=True) + EPS) * g


def _iota(shape, dim):
    return lax.broadcasted_iota(jnp.int32, shape, dim)


def _split3(x):
    p0 = x.astype(BF16)
    r1 = x - p0.astype(F32)
    p1 = r1.astype(BF16)
    p2 = (r1 - p1.astype(F32)).astype(BF16)
    return p0, p1, p2


def _cumsum_rows(x, carry):
    n = x.shape[0]
    ch = min(n, CUMSUM_CHUNK)
    tril = (_iota((ch, ch), 1) <= _iota((ch, ch), 0)).astype(BF16)
    outs = []
    for c0 in range(0, n, ch):
        r = _dot(tril, jnp.concatenate(_split3(x[c0:c0 + ch]), axis=1))
        c = r[:, :LANES] + r[:, LANES:2 * LANES] + r[:, 2 * LANES:] + carry
        carry = c[ch - 1:ch]
        outs.append(c)
    return (outs[0] if len(outs) == 1 else jnp.concatenate(outs, axis=0)), carry


def _bias_pieces(c):
    b = c * (-LOG2E)
    p0 = b.astype(BF16).astype(F32)
    r1 = b - p0
    p1 = r1.astype(BF16).astype(F32)
    p2 = (r1 - p1).astype(BF16).astype(F32)
    grp = _iota(c.shape, 1) >> 3
    return jnp.where(grp == 0, p0, jnp.where(grp == 1, p1, jnp.where(grp == 2, p2, 0.0)))


def _mem_kv_kernel(mem_ref, g_ref, w_ref, mk_ref, mv_ref):
    M = mem_ref.shape[1]
    xn = _rms(mem_ref[0], g_ref[...]).astype(BF16)
    for out_ref, off in ((mk_ref, 0), (mv_ref, MEM_WIDTH)):
        kv = _dot(xn, w_ref[:, off:off + MEM_WIDTH])
        for h in range(MEM_HEADS):
            out_ref[0, pl.ds(h, M, stride=MEM_HEADS), :] = kv[:, h * MEM_HEAD_DIM:(h + 1) * MEM_HEAD_DIM]


def _mem_kv(mem, g, w_mkv):
    B, M, D = mem.shape
    return pl.pallas_call(
        _mem_kv_kernel,
        grid=(B,),
        in_specs=[pl.BlockSpec((1, M, D), lambda b: (b, 0, 0)), pl.BlockSpec((1, D), lambda b: (0, 0)),
                  pl.BlockSpec((D, 2 * MEM_WIDTH), lambda b: (0, 0))],
        out_specs=[pl.BlockSpec((1, M * MEM_HEADS, MEM_HEAD_DIM), lambda b: (b, 0, 0))] * 2,
        out_shape=[jax.ShapeDtypeStruct((B, M * MEM_HEADS, MEM_HEAD_DIM), F32)] * 2,
        compiler_params=pltpu.CompilerParams(dimension_semantics=("parallel",),
                                             vmem_limit_bytes=VMEM_LIMIT),
        name="mem_kv",
    )(mem, g, w_mkv)


def _in_proj_kernel(x_ref, g_ref, wk_ref, wv_ref, wu_ref, wqm_ref, wf_ref, wqt_ref, bf_ref, hist_ref,
                    mk_ref, mv_ref, wpool_ref, spool_ref,
                    k_ref, v_ref, logf_ref, pnew_ref, opool_ref, omem_ref, *rest, tm, pos0, augment, blk):
    extra, (uext_ref, carry_ref) = rest[:-2], rest[-2:]
    t = pl.program_id(1)
    last = pl.num_programs(1) - 1
    q_scale = FOX_HEAD_DIM ** -0.5 * LOG2E

    @pl.when(t == 0)
    def _():
        uext_ref[0:HIST_ROWS, :] = hist_ref[0]
        carry_ref[...] = jnp.zeros_like(carry_ref)

    xn = _rms(x_ref[0], g_ref[...]).astype(BF16)
    lane = _iota((tm, LANES), 1)
    mem_cols = [slice(h * MEM_HEAD_DIM, (h + 1) * MEM_HEAD_DIM) for h in range(MEM_HEADS)]

    zf = _dot(xn, wf_ref[...]) + bf_ref[...]
    qm = _dot(xn, wqm_ref[...]).astype(BF16)
    k = _dot(xn, wk_ref[...])
    logf = jnp.minimum(zf, 0.0) - jnp.log1p(jnp.exp(-jnp.abs(zf)))
    n_mem = mk_ref.shape[1] // MEM_HEADS
    mem_head = lambda ref, h: ref[0, pl.ds(h, n_mem, stride=MEM_HEADS), :].astype(BF16)
    mem_s = [_dot_nt(qm[:, cols], mem_head(mk_ref, h)) * MEM_HEAD_DIM ** -0.5
             for h, cols in enumerate(mem_cols)]
    v = _dot(xn, wv_ref[...])
    u = _dot(xn, wu_ref[...])
    if augment:
        c, carry_ref[...] = _cumsum_rows(logf, carry_ref[...])
        qt_all = (_dot_nt(wqt_ref[...], xn) * q_scale).astype(BF16)
    else:
        q = _dot_nt(xn, wqt_ref[...]) * q_scale
    for h, (s, cols) in enumerate(zip(mem_s, mem_cols)):
        e = jnp.exp(s - jnp.max(s, axis=-1, keepdims=True))
        o = _dot(e.astype(BF16), mem_head(mv_ref, h)) / jnp.sum(e, axis=-1, keepdims=True)
        omem_ref[0, :, cols] = o.astype(BF16)
    if augment:
        vt_all = v.T.astype(BF16)

    k_ref[0] = k
    v_ref[0] = v
    logf_rows = logf if tm >= LANES else jnp.concatenate([logf, jnp.zeros((LANES - tm, LANES), F32)], axis=0)
    logf_ref[0] = logf_rows.T[:FOX_HEADS, :tm]

    if augment:
        qt_ref, ka_ref, vt_ref = extra
        bias = pltpu.roll(_bias_pieces(c), FOX_HEAD_DIM, axis=1)
        in_bias = (lane >= FOX_HEAD_DIM) & (lane < FOX_HEAD_DIM + BIAS_LANES)
        for pair in range(FOX_HEADS // 2):
            even = k[:, pair * LANES:(pair + 1) * LANES]
            odd = pltpu.roll(even, FOX_HEAD_DIM, axis=1)
            for par, data in ((0, even), (1, odd)):
                h = 2 * pair + par
                tail = jnp.where(in_bias & ((lane & 7) == h), bias, 0.0)
                ka_ref[0, :, h * LANES:(h + 1) * LANES] = jnp.where(lane < FOX_HEAD_DIM, data, tail).astype(BF16)
        q_tail = jnp.where(_iota((LANES - FOX_HEAD_DIM, blk), 0) < BIAS_LANES, 1.0, 0.0).astype(BF16)
        v_tail = jnp.where(_iota((V_SLAB - FOX_HEAD_DIM, blk), 0) == 0, 1.0, 0.0).astype(BF16)
        for h in range(FOX_HEADS):
            qt = qt_all[h * FOX_HEAD_DIM:(h + 1) * FOX_HEAD_DIM]
            vt = vt_all[h * FOX_HEAD_DIM:(h + 1) * FOX_HEAD_DIM]
            for j in range(tm // blk):
                qt_ref[0, j, h * LANES:h * LANES + FOX_HEAD_DIM, :] = qt[:, j * blk:(j + 1) * blk]
                qt_ref[0, j, h * LANES + FOX_HEAD_DIM:(h + 1) * LANES, :] = q_tail
                vt_ref[0, j, h * V_SLAB:h * V_SLAB + FOX_HEAD_DIM, :] = vt[:, j * blk:(j + 1) * blk]
                vt_ref[0, j, h * V_SLAB + FOX_HEAD_DIM:(h + 1) * V_SLAB, :] = v_tail
    else:
        extra[0][0] = q

    uext_ref[HIST_ROWS:HIST_ROWS + tm, :] = u
    pos = pos0 + t * tm + _iota((tm, 1), 0)
    for gi, w in enumerate(POOL_WINDOWS):
        cols = slice(gi * POOL_GROUP_DIM, (gi + 1) * POOL_GROUP_DIM)
        sw = uext_ref[:, cols]
        shift = 1
        while shift < w:
            sw = sw + pltpu.roll(sw, shift, axis=0)
            shift *= 2
        cnt = jnp.minimum(w, pos + 1).astype(F32)
        pooled = sw[HIST_ROWS:, :] / cnt - u[:, cols]
        mixed = _dot(pooled.astype(BF16), wpool_ref[gi]) * spool_ref[:, cols]
        opool_ref[0, :, cols] = mixed.astype(BF16)

    @pl.when(t == last)
    def _():
        pnew_ref[0] = uext_ref[tm + 1:tm + HIST_ROWS, :]

    uext_ref[0:HIST_ROWS, :] = uext_ref[tm:tm + HIST_ROWS, :]


def _in_proj(x, g, weights, bf_rep, hist, mk, mv, wpool, spool, *, tm, pos0, augment, blk=None):
    B, S, D = x.shape
    nt = S // tm
    tile = lambda w: pl.BlockSpec((1, tm, w), lambda b, t: (b, t, 0))
    per_b = lambda r, w: pl.BlockSpec((1, r, w), lambda b, t: (b, 0, 0))
    const = lambda shape: pl.BlockSpec(shape, lambda b, t: (0,) * len(shape), pipeline_mode=pl.Buffered(1))
    if augment:
        W = FOX_HEADS * LANES
        t_tile = lambda rows: pl.BlockSpec((1, tm // blk, rows, blk), lambda b, t: (b, t, 0, 0))
        abc_specs = [t_tile(W), tile(W), t_tile(FOX_HEADS * V_SLAB)]
        abc_shapes = [jax.ShapeDtypeStruct((B, S // blk, W, blk), BF16),
                      jax.ShapeDtypeStruct((B, S, W), BF16),
                      jax.ShapeDtypeStruct((B, S // blk, FOX_HEADS * V_SLAB, blk), BF16)]
    else:
        abc_specs = [tile(FOX_WIDTH)]
        abc_shapes = [jax.ShapeDtypeStruct((B, S, FOX_WIDTH), F32)]
    return pl.pallas_call(
        functools.partial(_in_proj_kernel, tm=tm, pos0=pos0, augment=augment, blk=blk),
        grid=(B, nt),
        in_specs=[tile(D), const((1, D))] + [const(w.shape) for w in weights] + [
            const((1, LANES)), per_b(HIST_ROWS, POOL_WIDTH), per_b(*mk.shape[1:]), per_b(*mv.shape[1:]),
            const((POOL_GROUPS, POOL_GROUP_DIM, POOL_GROUP_DIM)), const((1, POOL_WIDTH))],
        out_specs=[tile(FOX_WIDTH), tile(FOX_WIDTH), pl.BlockSpec((1, FOX_HEADS, tm), lambda b, t: (b, 0, t)),
                   per_b(POOL_HIST, POOL_WIDTH), tile(POOL_WIDTH), tile(MEM_WIDTH)] + abc_specs,
        out_shape=[jax.ShapeDtypeStruct((B, S, FOX_WIDTH), F32),
                   jax.ShapeDtypeStruct((B, S, FOX_WIDTH), F32),
                   jax.ShapeDtypeStruct((B, FOX_HEADS, S), F32),
                   jax.ShapeDtypeStruct((B, POOL_HIST, POOL_WIDTH), F32),
                   jax.ShapeDtypeStruct((B, S, POOL_WIDTH), BF16),
                   jax.ShapeDtypeStruct((B, S, MEM_WIDTH), BF16)] + abc_shapes,
        scratch_shapes=[pltpu.VMEM((HIST_ROWS + tm, POOL_WIDTH), F32), pltpu.VMEM((1, LANES), F32)],
        compiler_params=pltpu.CompilerParams(dimension_semantics=("parallel", "arbitrary"),
                                             vmem_limit_bytes=VMEM_LIMIT),
        name="in_proj",
    )(x, g, *weights, bf_rep, hist, mk, mv, wpool, spool)


def _fox_prompt_kernel(qt_ref, k_ref, vt_ref, o_ref, m_ref, acc_ref, *, blk):
    i = pl.program_id(1)
    m_ref[...] = jnp.full_like(m_ref, -jnp.inf)
    acc_ref[...] = jnp.zeros_like(acc_ref)
    slabs = [slice(h * LANES, (h + 1) * LANES) for h in range(FOX_HEADS)]

    def scores(j, nb, h):
        start = pl.multiple_of(j * blk, blk)
        return _dot(k_ref[0, pl.ds(start, nb * blk), slabs[h]], qt_ref[0, 0, slabs[h], :])

    def accumulate(j, nb, h, st, masked):
        if masked:
            st = jnp.where(_iota(st.shape, 0) <= _iota(st.shape, 1), st, -jnp.inf)
        m_prev = m_ref[h]
        m_new = jnp.maximum(m_prev, jnp.max(st, axis=0, keepdims=True))
        p = jnp.exp2(st - m_new).astype(BF16)
        pv = _dot(vt_ref[0, j, h * V_SLAB:(h + 1) * V_SLAB, :], p[:blk])
        for r in range(1, nb):
            pv = pv + _dot(vt_ref[0, j + r, h * V_SLAB:(h + 1) * V_SLAB, :], p[r * blk:(r + 1) * blk])
        acc_ref[h] = jnp.exp2(m_prev - m_new) * acc_ref[h] + pv
        m_ref[h] = m_new

    def run(first, n_visible):
        spans = [(r, min(KEY_BLOCKS_PER_ITEM, n_visible - r), False)
                 for r in range(0, n_visible, KEY_BLOCKS_PER_ITEM)]
        if n_visible < KEY_BLOCKS_PER_ITER:
            spans.append((n_visible, 1, True))
        work = [(first + r, nb, h, masked) for r, nb, masked in spans for h in range(FOX_HEADS)]
        pending = [scores(*w[:3]) for w in work[:SCORES_AHEAD]]
        for n, (j, nb, h, masked) in enumerate(work):
            if n + SCORES_AHEAD < len(work):
                pending.append(scores(*work[n + SCORES_AHEAD][:3]))
            accumulate(j, nb, h, pending.pop(0), masked)

    shift = KEY_BLOCKS_PER_ITER.bit_length() - 1
    n_iter = i >> shift

    def body(jj, carry):
        run(KEY_BLOCKS_PER_ITER * jj, KEY_BLOCKS_PER_ITER)
        return carry

    lax.fori_loop(0, n_iter, body, 0)
    for rem in range(KEY_BLOCKS_PER_ITER):
        @pl.when(i - (n_iter << shift) == rem)
        def _(rem=rem):
            run(i - rem, rem)

    for h in range(FOX_HEADS):
        acc = acc_ref[h]
        o_ref[0, 0, h * FOX_HEAD_DIM:(h + 1) * FOX_HEAD_DIM, :] = (
            acc[:FOX_HEAD_DIM] / acc[FOX_HEAD_DIM:FOX_HEAD_DIM + 1])


def _fox_prompt(qt, ka, vt, *, blk):
    B, S, W = ka.shape
    nblk = S // blk
    return pl.pallas_call(
        functools.partial(_fox_prompt_kernel, blk=blk),
        grid=(B, nblk),
        in_specs=[pl.BlockSpec((1, 1, W, blk), lambda b, i: (b, i, 0, 0)),
                  pl.BlockSpec((1, S, W), lambda b, i: (b, 0, 0)),
                  pl.BlockSpec((1, nblk, FOX_HEADS * V_SLAB, blk), lambda b, i: (b, 0, 0, 0))],
        out_specs=pl.BlockSpec((1, 1, FOX_WIDTH, blk), lambda b, i: (b, i, 0, 0)),
        out_shape=jax.ShapeDtypeStruct((B, nblk, FOX_WIDTH, blk), F32),
        scratch_shapes=[pltpu.VMEM((FOX_HEADS, 1, blk), F32), pltpu.VMEM((FOX_HEADS, V_SLAB, blk), F32)],
        compiler_params=pltpu.CompilerParams(
            dimension_semantics=("parallel", "arbitrary"), vmem_limit_bytes=VMEM_LIMIT),
        name="fox_prompt",
    )(qt, ka, vt)


def _bias_piece_rows(c):
    b = c * (-LOG2E)
    p0 = b.astype(BF16).astype(F32)
    r1 = b - p0
    p1 = r1.astype(BF16).astype(F32)
    p2 = (r1 - p1).astype(BF16).astype(F32)
    return jnp.concatenate([p0, p1, p2], axis=0)


def _fox_sample_kernel(q_ref, kn_ref, vn_ref, lfn_ref, kt_ref, vt_ref, lft_ref, o_ref, *, chunk):
    T = q_ref.shape[1]
    P = lft_ref.shape[2]
    HT = FOX_HEADS * T
    t_shift = T.bit_length() - 1
    assert HT == LANES and 1 << t_shift == T

    q_rep = jnp.concatenate([q_ref[0]] * FOX_HEADS, axis=0)
    row_head = _iota((HT, FOX_WIDTH), 0) >> t_shift
    q_bd = jnp.where(row_head == (_iota((HT, FOX_WIDTH), 1) >> 6), q_rep, 0.0).astype(BF16)
    e_lane = _iota((HT, LANES), 1)
    e_bd = jnp.where((e_lane < BIAS_LANES) & ((e_lane & 7) == (_iota((HT, LANES), 0) >> t_shift)),
                     1.0, 0.0).astype(BF16)

    def bias_operand(c):
        rows = _bias_piece_rows(c)
        return jnp.concatenate([rows, jnp.zeros((LANES - BIAS_LANES, c.shape[1]), F32)], axis=0).astype(BF16)

    def running_sum(lf, carry, upper):
        pieces = jnp.concatenate(list(_split3(lf)) + [jnp.zeros(lf.shape, BF16)], axis=0)
        loc = _dot(pieces, upper)
        return loc[0:8] + loc[8:16] + loc[16:24] + carry

    upper = (_iota((chunk, chunk), 0) <= _iota((chunk, chunk), 1)).astype(BF16)
    carry = jnp.zeros((FOX_HEADS, 1), F32)
    c_parts = []
    for ci in range(P // chunk):
        c = running_sum(lft_ref[0, :, ci * chunk:(ci + 1) * chunk], carry, upper)
        carry = c[:, chunk - 1:chunk]
        c_parts.append(c)
    kt = kt_ref[0].reshape(FOX_WIDTH, P).astype(BF16)
    s_old = _dot(q_bd, kt) + _dot(e_bd, bias_operand(jnp.concatenate(c_parts, axis=1)))

    pad_lanes = lambda a: jnp.concatenate([a, jnp.zeros((a.shape[0], LANES - a.shape[1]), F32)], axis=1)
    pad_rows = lambda a: jnp.concatenate([a, jnp.zeros((LANES - T, a.shape[1]), F32)], axis=0)
    upper_new = (_iota((LANES, LANES), 0) <= _iota((LANES, LANES), 1)).astype(BF16)
    c_new = running_sum(pad_lanes(lfn_ref[0]), carry, upper_new)
    s_new = _dot_nt(q_bd, pad_rows(kn_ref[0]).astype(BF16)) + _dot(e_bd, bias_operand(c_new))
    s_new = jnp.where(_iota((HT, LANES), 1) <= (_iota((HT, LANES), 0) & (T - 1)), s_new, -jnp.inf)

    m = jnp.maximum(jnp.max(s_old, axis=1, keepdims=True), jnp.max(s_new, axis=1, keepdims=True))
    p_old = jnp.exp2(s_old - m).astype(BF16)
    p_new = jnp.exp2(s_new - m).astype(BF16)
    l = (jnp.sum(p_old.astype(F32), axis=1, keepdims=True)
         + jnp.sum(p_new.astype(F32), axis=1, keepdims=True))
    r = (_dot_nt(p_old, vt_ref[0].reshape(FOX_WIDTH, P).astype(BF16))
         + _dot(p_new, pad_rows(vn_ref[0]).astype(BF16))) / l
    col_head = _iota((T, FOX_WIDTH), 1) >> 6
    o = jnp.zeros((T, FOX_WIDTH), F32)
    for h in range(FOX_HEADS):
        o = jnp.where(col_head == h, r[h * T:(h + 1) * T, :], o)
    o_ref[0] = o.astype(BF16)


def _fox_sample(q, k_new, v_new, lf_new_t, cache_kt, cache_vt, cache_lft, *, chunk):
    B, T, _ = q.shape
    P = cache_lft.shape[2]
    new = lambda w: pl.BlockSpec((1, T, w), lambda b: (b, 0, 0))
    kv_spec = pl.BlockSpec((1, FOX_HEADS, FOX_HEAD_DIM, P), lambda b: (b, 0, 0, 0))
    return pl.pallas_call(
        functools.partial(_fox_sample_kernel, chunk=chunk),
        grid=(B,),
        in_specs=[new(FOX_WIDTH), new(FOX_WIDTH), new(FOX_WIDTH),
                  pl.BlockSpec((1, FOX_HEADS, T), lambda b: (b, 0, 0)), kv_spec, kv_spec,
                  pl.BlockSpec((1, FOX_HEADS, P), lambda b: (b, 0, 0))],
        out_specs=new(FOX_WIDTH),
        out_shape=jax.ShapeDtypeStruct((B, T, FOX_WIDTH), BF16),
        compiler_params=pltpu.CompilerParams(dimension_semantics=("parallel",),
                                             vmem_limit_bytes=VMEM_LIMIT),
        name="fox_sample",
    )(q, k_new, v_new, lf_new_t, cache_kt, cache_vt, cache_lft)


def _merge_ffn_kernel(x_ref, ofox_ref, opool_ref, omem_ref, ga_ref, wg_ref, wbf_ref, wbp_ref, wbm_ref,
                      wout_ref, gm_ref, wup_ref, wdown_ref, gfin_ref, y_ref, *, ff_chunk, final_norm):
    D = x_ref.shape[-1]
    streams = ofox_ref.shape[1]
    rows = x_ref.shape[1] // streams
    groups = [slice(i * rows, (i + 1) * rows) for i in range(streams)]

    xs = [x_ref[0, r, :] for r in groups]
    pre = []
    for si, (x, r) in enumerate(zip(xs, groups)):
        v1 = _dot(opool_ref[0, r, :], wbp_ref[...])
        v2 = _dot(omem_ref[0, r, :], wbm_ref[...])
        v0 = _dot(ofox_ref[0, si].T.astype(BF16), wbf_ref[...])
        xn = _rms(x, ga_ref[...]).astype(BF16)
        gates = [_dot(xn, wg_ref[:, bi * D:(bi + 1) * D]) for bi in range(N_BRANCH)]
        pre.append((gates, (v0, v1, v2)))
    hs = []
    for x, ((g0, g1, g2), (v0, v1, v2)) in zip(xs, pre):
        m = jax.nn.sigmoid(g0) * v0 + jax.nn.sigmoid(g1) * v1 + jax.nn.sigmoid(g2) * v2
        hs.append(x + _dot(m.astype(BF16), wout_ref[...]))
    hns = [_rms(h, gm_ref[...]).astype(BF16) for h in hs]
    accs = list(hs)
    for c in range(wup_ref.shape[1] // ff_chunk):
        cols = slice(c * ff_chunk, (c + 1) * ff_chunk)
        ups = [jnp.maximum(_dot(hn, wup_ref[:, cols]), 0.0) for hn in hns]
        accs = [acc + _dot((a * a).astype(BF16), wdown_ref[cols, :]) for acc, a in zip(accs, ups)]
    for acc, r in zip(accs, groups):
        y_ref[0, r, :] = _rms(acc, gfin_ref[...]) if final_norm else acc


def _merge_ffn(x, ofox_t, opool, omem, ga, wg, wbf, wbp, wbm, wout, gm, wup, wdown, gfin, *, tm,
               ff_chunk, final_norm):
    B, S, D = x.shape
    blk = ofox_t.shape[3]
    tile = lambda w: pl.BlockSpec((1, tm, w), lambda b, t: (b, t, 0))
    fox_tile = pl.BlockSpec((1, tm // blk, FOX_WIDTH, blk), lambda b, t: (b, t, 0, 0))
    const = lambda a: pl.BlockSpec(a.shape, lambda b, t: (0,) * a.ndim, pipeline_mode=pl.Buffered(1))
    consts = (ga, wg, wbf, wbp, wbm, wout, gm, wup, wdown, gfin)
    return pl.pallas_call(
        functools.partial(_merge_ffn_kernel, ff_chunk=ff_chunk, final_norm=final_norm),
        grid=(B, S // tm),
        in_specs=[tile(D), fox_tile, tile(POOL_WIDTH), tile(MEM_WIDTH)] + [const(a) for a in consts],
        out_specs=tile(D),
        out_shape=jax.ShapeDtypeStruct((B, S, D), F32),
        compiler_params=pltpu.CompilerParams(dimension_semantics=("parallel", "parallel"),
                                             vmem_limit_bytes=VMEM_LIMIT),
        name="merge_ffn",
    )(x, ofox_t, opool, omem, *consts)


def _tile_rows(n, want):
    return want if n % want == 0 else n


def _repack_w_in(w_in):
    o_f = 3 * FOX_WIDTH
    o_u = o_f + FOX_HEADS
    o_qm = o_u + POOL_WIDTH
    o_g = o_qm + MEM_WIDTH
    b16 = lambda a: a.astype(BF16)
    wq, wk, wv = (w_in[:, i * FOX_WIDTH:(i + 1) * FOX_WIDTH] for i in range(3))
    wf_rep = _rep_lanes(w_in[:, o_f:o_u])
    weights = (b16(wk), b16(wv), b16(w_in[:, o_u:o_qm]), b16(w_in[:, o_qm:o_g]), b16(wf_rep), b16(wq).T)
    return weights, b16(w_in[:, o_g:])


def _rep_lanes(a):
    pad = jnp.zeros(a.shape[:-1] + (LANES - BIAS_LANES,), a.dtype)
    return jnp.concatenate([a] * BIAS_PIECES + [pad], axis=-1)


def kernel(x_prompt, x_sample, mem_prompt, cache_fox_k, cache_fox_v, cache_fox_logf, cache_pool,
           cache_mem_k, cache_mem_v, g_attn, w_in, b_f, w_pool, s_pool, g_mem, w_mkv,
           w_br_fox, w_br_pool, w_br_mem, w_out, g_mlp, w_up, w_down, g_final):
    depth = w_in.shape[0]
    B, S, D = x_prompt.shape
    BS, T, _ = x_sample.shape
    P = cache_fox_k.shape[2]
    M = mem_prompt.shape[1]
    tm_p = _tile_rows(S, 512)
    blk_p = _tile_rows(tm_p, 256)
    tm_s = _tile_rows(BS * T, 256)
    chunk_s = _tile_rows(P, 256)
    gfin = g_final.reshape(1, D)

    xp, xs = x_prompt, x_sample
    outs_p = [[] for _ in range(6)]
    outs_s = [[] for _ in range(4)]
    for l in range(depth):
        weights, wg = _repack_w_in(w_in[l])
        bf_rep = _rep_lanes(b_f[l]).reshape(1, LANES)
        g_a, g_m, g_f = g_attn[l].reshape(1, D), g_mem[l].reshape(1, D), g_mlp[l].reshape(1, D)
        wpool, spool = w_pool[l].astype(BF16), s_pool[l].reshape(1, POOL_WIDTH)
        wbf, wbp, wbm = (w.astype(BF16) for w in (w_br_fox[l], w_br_pool[l], w_br_mem[l]))
        wout, wup, wdown = (w.astype(BF16) for w in (w_out[l], w_up[l], w_down[l]))
        final = l == depth - 1
        post = functools.partial(_merge_ffn, ga=g_a, wg=wg, wbf=wbf, wbp=wbp, wbm=wbm, wout=wout, gm=g_f,
                                 wup=wup, wdown=wdown, gfin=gfin, ff_chunk=1024, final_norm=final)

        hist = jnp.pad(cache_pool[l], ((0, 0), (HIST_ROWS - POOL_HIST, 0), (0, 0)))
        cmk = cache_mem_k[l].reshape(BS, M * MEM_HEADS, MEM_HEAD_DIM)
        cmv = cache_mem_v[l].reshape(BS, M * MEM_HEADS, MEM_HEAD_DIM)
        k, v, logf_t, pnew, opool, omem, q = _in_proj(
            xs, g_a, weights, bf_rep, hist, cmk, cmv, wpool, spool, tm=T, pos0=P, augment=False)
        logf = jnp.transpose(logf_t, (0, 2, 1))
        ofox = _fox_sample(q, k, v, logf_t,
                           jnp.transpose(cache_fox_k[l], (0, 2, 3, 1)), jnp.transpose(cache_fox_v[l], (0, 2, 3, 1)),
                           jnp.transpose(cache_fox_logf[l], (0, 2, 1)), chunk=chunk_s)
        flat = lambda a: a.reshape(1, BS * T, a.shape[-1])
        ofox_t = jnp.transpose(ofox.reshape(BS * T // tm_s, tm_s, FOX_WIDTH), (0, 2, 1))[None].astype(F32)
        xs = post(flat(xs), ofox_t, flat(opool), flat(omem), tm=tm_s).reshape(BS, T, D)
        for acc, val in zip(outs_s, (k.reshape(BS, T, FOX_HEADS, FOX_HEAD_DIM),
                                     v.reshape(BS, T, FOX_HEADS, FOX_HEAD_DIM), logf, pnew)):
            acc.append(val)

        mk, mv = _mem_kv(mem_prompt, g_m, w_mkv[l].astype(BF16))
        hist0 = jnp.zeros((B, HIST_ROWS, POOL_WIDTH), F32)
        k, v, logf_t, pnew, opool, omem, qt, ka, vt = _in_proj(
            xp, g_a, weights, bf_rep, hist0, mk, mv, wpool, spool, tm=tm_p, pos0=0, augment=True,
            blk=blk_p)
        logf = jnp.transpose(logf_t, (0, 2, 1))
        ofox = _fox_prompt(qt, ka, vt, blk=blk_p)
        xp = post(xp, ofox, opool, omem, tm=tm_p)
        for acc, val in zip(outs_p, (k.reshape(B, S, FOX_HEADS, FOX_HEAD_DIM),
                                     v.reshape(B, S, FOX_HEADS, FOX_HEAD_DIM), logf, pnew,
                                     mk.reshape(B, M, MEM_HEADS, MEM_HEAD_DIM),
                                     mv.reshape(B, M, MEM_HEADS, MEM_HEAD_DIM))):
            acc.append(val)

    stack = lambda o: o[0][None] if depth == 1 else jnp.stack(o)
    return (xp, xs, *(stack(o) for o in outs_p), *(stack(o) for o in outs_s))
```
